```python
import math
import jax, jax.numpy as jnp
from jax import lax
import numpy as np

D_MODEL = 1024
BATCH = 16
SEQ = 4096
DEPTH = 4

CTX_LEN = 256
GRID_W = 64
N_MIXERS = 3
N_LAYERS_A = (DEPTH + 2) // 3
N_LAYERS_B = (DEPTH + 1) // 3
N_LAYERS_C = DEPTH // 3
SSM_CH_PER_GROUP = 16
SSM_GROUPS = D_MODEL // SSM_CH_PER_GROUP
SSM_STATE = 64
DT_MIN = 1e-3
DT_MAX = 1e-1
CONV_WIDTH = 31
HEAD_DIM = 64
N_HEADS = D_MODEL // HEAD_DIM
N_KV_HEADS = N_HEADS // 4
KV_REP = N_HEADS // N_KV_HEADS
WINDOW = 128
BLOCK = 128
ROPE_BASE = 10000.0
N_EXPERT_GROUPS = 4
EXPERTS_PER_GROUP = 8
N_EXPERTS = N_EXPERT_GROUPS * EXPERTS_PER_GROUP
TOP_K = 2
D_EXPERT = D_MODEL // 2
ALPHA = (2 * DEPTH) ** 0.25
BETA = (8 * DEPTH) ** -0.25
LN_EPS = 1e-5

kernel_name = 'hybrid_s5_conformer_swa_hmoe_dit'

F32 = jnp.float32


def _layer_norm(x, g, b):
    xf = x.astype(F32)
    mu = jnp.mean(xf, axis=-1, keepdims=True)
    var = jnp.mean(jnp.square(xf - mu), axis=-1, keepdims=True)
    return ((xf - mu) * lax.rsqrt(var + LN_EPS) * g + b).astype(x.dtype)


def _ssm_combine(e1, e2):
    a1, b1 = e1
    a2, b2 = e2
    return a1 * a2, a2 * b1 + b2


def _s5_scan(ug, lam_re, lam_im, log_dt, b_re, b_im, reverse, h0):
    L = ug.shape[1]
    lam = lax.complex(jnp.minimum(lam_re.astype(F32), -1e-4), lam_im.astype(F32))
    dt = jnp.exp(log_dt.astype(F32))[:, None]
    a_bar = jnp.exp(lam * dt)
    b_bar = ((a_bar - 1.0) / lam)[..., None] * lax.complex(b_re.astype(F32), b_im.astype(F32))
    bu = jnp.einsum('blgc,gpc->blgp', ug, b_bar)
    if h0 is not None:
        edge = L - 1 if reverse else 0
        bu = bu.at[:, edge].add(a_bar * h0)
    a_seq = jnp.broadcast_to(a_bar, (L,) + a_bar.shape)
    scan_one = lambda bu_b: lax.associative_scan(_ssm_combine, (a_seq, bu_b), reverse=reverse, axis=0)[1]
    return jax.vmap(scan_one)(bu)


def _s5_readout(hs, c_re, c_im):
    y = jnp.einsum('blgp,gcp->blgc', hs.real, c_re.astype(F32)) - jnp.einsum('blgp,gcp->blgc', hs.imag, c_im.astype(F32))
    return y.reshape(y.shape[0], y.shape[1], -1)


def _s5_output(u, y, d, w_glu):
    y = jax.nn.gelu(y + d.astype(F32) * u.astype(F32)).astype(u.dtype)
    val, gate = jnp.split(y @ w_glu, 2, axis=-1)
    return val * jax.nn.sigmoid(gate)


def _s5_mixer(u_ctx, u_lat, lam_re, lam_im, log_dt, b_re, b_im, c_re, c_im, d, w_glu, with_ctx_out):
    group = lambda u: u.astype(F32).reshape(u.shape[0], u.shape[1], SSM_GROUPS, SSM_CH_PER_GROUP)
    ug_ctx, ug_lat = group(u_ctx), group(u_lat)
    y_ctx, y_lat = 0.0, 0.0
    for k, reverse in enumerate((False, True)):
        p = (lam_re[k], lam_im[k], log_dt[k], b_re[k], b_im[k])
        hs_ctx = _s5_scan(ug_ctx, *p, reverse, None)
        h0 = hs_ctx[:, 0] if reverse else hs_ctx[:, -1]
        hs_lat = _s5_scan(ug_lat, *p, reverse, h0)
        y_lat = y_lat + _s5_readout(hs_lat, c_re[k], c_im[k])
        if with_ctx_out:
            y_ctx = y_ctx + _s5_readout(hs_ctx, c_re[k], c_im[k])
    out_ctx = _s5_output(u_ctx, y_ctx, d, w_glu) if with_ctx_out else None
    return out_ctx, _s5_output(u_lat, y_lat, d, w_glu)


def _conv_mixer(u_ctx, u_lat, w_pw1, b_pw1, w_dw, b_dw, g, b, w_pw2, b_pw2, with_ctx_out):
    def branch(u):
        a, gt = jnp.split(u @ w_pw1 + b_pw1, 2, axis=-1)
        z = a * jax.nn.sigmoid(gt)
        z = lax.conv_general_dilated(z, w_dw[:, None, :], (1,), [(CONV_WIDTH // 2, CONV_WIDTH // 2)],
                                     dimension_numbers=('NWC', 'WIO', 'NWC'),
                                     feature_group_count=z.shape[-1]) + b_dw
        z = jax.nn.silu(_layer_norm(z, g, b))
        return z @ w_pw2 + b_pw2
    return (branch(u_ctx) if with_ctx_out else None), branch(u_lat)


def _rope_2d(t, rows, cols):
    half = HEAD_DIM // 2
    quarter = half // 2
    inv_freq = ROPE_BASE ** (-jnp.arange(quarter, dtype=F32) / quarter)
    def rot(xh, pos):
        ang = pos.astype(F32)[:, None] * inv_freq
        cos, sin = jnp.cos(ang)[None, :, None, :], jnp.sin(ang)[None, :, None, :]
        x1, x2 = xh[..., :quarter], xh[..., quarter:]
        return jnp.concatenate([x1 * cos - x2 * sin, x2 * cos + x1 * sin], axis=-1)
    tf = t.astype(F32)
    return jnp.concatenate([rot(tf[..., :half], rows), rot(tf[..., half:], cols)], axis=-1).astype(t.dtype)


def _sink_softmax(scores, sink_gr):
    ref = scores[0]
    sink_col = jnp.broadcast_to(sink_gr[:, :, None, None], ref.shape[:-1] + (1,))
    p = jax.nn.softmax(jnp.concatenate(scores + [sink_col], axis=-1), axis=-1)
    cuts = list(np.cumsum([s.shape[-1] for s in scores]))
    return jnp.split(p, cuts, axis=-1)[:-1]


def _attn_mixer(u_ctx, u_lat, w_qkv, w_o, sink, rows, cols, with_ctx_out):
    B, L, _ = u_lat.shape
    n_blocks = L // BLOCK
    scale = HEAD_DIM ** -0.5
    def project(u):
        b_, l_, _ = u.shape
        q, k, v = jnp.split(u @ w_qkv, [N_HEADS * HEAD_DIM, (N_HEADS + N_KV_HEADS) * HEAD_DIM], axis=-1)
        return (q.reshape(b_, l_, N_HEADS, HEAD_DIM), k.reshape(b_, l_, N_KV_HEADS, HEAD_DIM),
                v.reshape(b_, l_, N_KV_HEADS, HEAD_DIM))
    q_l, k_l, v_l = project(u_lat)
    q_l = _rope_2d(q_l, rows, cols).reshape(B, L, N_KV_HEADS, KV_REP, HEAD_DIM)
    k_l = _rope_2d(k_l, rows, cols)
    q_c, k_c, v_c = project(u_ctx)
    sink_gr = sink.astype(F32).reshape(N_KV_HEADS, KV_REP)
    pad = ((0, 0), (BLOCK, BLOCK), (0, 0), (0, 0))
    k_pad, v_pad = jnp.pad(k_l, pad), jnp.pad(v_l, pad)

    def block(bi):
        start = bi * BLOCK
        q_b = lax.dynamic_slice_in_dim(q_l, start, BLOCK, axis=1)
        k_b = lax.dynamic_slice_in_dim(k_pad, start, 3 * BLOCK, axis=1)
        v_b = lax.dynamic_slice_in_dim(v_pad, start, 3 * BLOCK, axis=1)
        qi = start + jnp.arange(BLOCK)
        kj = start - BLOCK + jnp.arange(3 * BLOCK)
        valid = (jnp.abs(qi[:, None] - kj[None, :]) <= WINDOW) & (kj[None, :] >= 0) & (kj[None, :] < L)
        s_w = jnp.einsum('bigrd,bjgd->bgrij', q_b, k_b).astype(F32) * scale
        s_w = jnp.where(valid, s_w, -jnp.inf)
        s_c = jnp.einsum('bigrd,bjgd->bgrij', q_b, k_c).astype(F32) * scale
        p_w, p_c = _sink_softmax([s_w, s_c], sink_gr)
        return (jnp.einsum('bgrij,bjgd->bigrd', p_w.astype(v_b.dtype), v_b)
                + jnp.einsum('bgrij,bjgd->bigrd', p_c.astype(v_c.dtype), v_c))

    o_blocks = lax.map(block, jnp.arange(n_blocks))
    o_lat = jnp.moveaxis(o_blocks, 0, 1).reshape(B, L, N_HEADS * HEAD_DIM) @ w_o
    o_ctx = None
    if with_ctx_out:
        Lc = u_ctx.shape[1]
        qg = q_c.reshape(B, Lc, N_KV_HEADS, KV_REP, HEAD_DIM)
        s_cc = jnp.einsum('bigrd,bjgd->bgrij', qg, k_c).astype(F32) * scale
        (p_cc,) = _sink_softmax([s_cc], sink_gr)
        o = jnp.einsum('bgrij,bjgd->bigrd', p_cc.astype(v_c.dtype), v_c)
        o_ctx = o.reshape(B, Lc, N_HEADS * HEAD_DIM) @ w_o
    return o_ctx, o_lat


def _hier_moe(tok, wg, bg, we, be, w1, w2):
    T = tok.shape[0]
    g_prob = jax.nn.softmax((tok @ wg).astype(F32) + bg.astype(F32), axis=-1)
    p_g, g_idx = lax.top_k(g_prob, 1)
    e_logits = ((tok @ we).astype(F32) + be.astype(F32)).reshape(T, N_EXPERT_GROUPS, EXPERTS_PER_GROUP)
    e_in = jnp.take_along_axis(e_logits, g_idx[:, :, None], axis=1)[:, 0]
    top_val, top_idx = lax.top_k(e_in, TOP_K)
    weights = p_g * jax.nn.softmax(top_val, axis=-1)
    expert_id = g_idx * EXPERTS_PER_GROUP + top_idx
    combine = jnp.sum(jax.nn.one_hot(expert_id, N_EXPERTS, dtype=F32) * weights[..., None], axis=1)
    out = jnp.zeros(tok.shape, F32)
    for e in range(N_EXPERTS):
        gate, up = jnp.split(tok @ w1[e], 2, axis=-1)
        y = (jax.nn.silu(gate) * up) @ w2[e]
        out = out + combine[:, e:e + 1] * y.astype(F32)
    return out.astype(tok.dtype)


def setup_inputs(seed: int = 0) -> dict:
    key = jax.random.key(seed)
    ks = iter(jax.random.split(key, 48))
    def nrm(shape, scale):
        return scale * jax.random.normal(next(ks), shape, F32)
    D = D_MODEL
    x = nrm((BATCH, SEQ, D), 1.0)
    c = nrm((BATCH, D), 1.0)
    ctx = nrm((BATCH, CTX_LEN, D), 1.0)
    c_ctx = nrm((D,), 1.0)
    ada_w = nrm((DEPTH, D, 6 * D), 0.5 * D ** -0.5)
    ada_b = nrm((DEPTH, 6 * D), 0.02)
    ln_g = 1.0 + nrm((DEPTH, 2, D), 0.05)
    ln_b = nrm((DEPTH, 2, D), 0.02)
    ssm_shape = (N_LAYERS_A, 2, SSM_GROUPS, SSM_STATE)
    s5_lam_re = -0.5 + nrm(ssm_shape, 0.01)
    s5_lam_im = math.pi * jnp.arange(SSM_STATE, dtype=F32) + nrm(ssm_shape, 0.01)
    s5_log_dt = jax.random.uniform(next(ks), (N_LAYERS_A, 2, SSM_GROUPS), F32, math.log(DT_MIN), math.log(DT_MAX))
    b_shape = (N_LAYERS_A, 2, SSM_GROUPS, SSM_STATE, SSM_CH_PER_GROUP)
    s5_b_re = nrm(b_shape, (2 * SSM_CH_PER_GROUP) ** -0.5)
    s5_b_im = nrm(b_shape, (2 * SSM_CH_PER_GROUP) ** -0.5)
    c_shape = (N_LAYERS_A, 2, SSM_GROUPS, SSM_CH_PER_GROUP, SSM_STATE)
    s5_c_re = nrm(c_shape, SSM_STATE ** -0.5)
    s5_c_im = nrm(c_shape, SSM_STATE ** -0.5)
    s5_d = nrm((N_LAYERS_A, D), 1.0)
    s5_w_glu = jnp.concatenate([nrm((N_LAYERS_A, D, D), BETA * D ** -0.5), nrm((N_LAYERS_A, D, D), D ** -0.5)], axis=-1)
    cv_w_pw1 = nrm((N_LAYERS_B, D, 2 * D), D ** -0.5)
    cv_b_pw1 = nrm((N_LAYERS_B, 2 * D), 0.02)
    cv_w_dw = nrm((N_LAYERS_B, CONV_WIDTH, D), CONV_WIDTH ** -0.5)
    cv_b_dw = nrm((N_LAYERS_B, D), 0.02)
    cv_ln_g = 1.0 + nrm((N_LAYERS_B, D), 0.05)
    cv_ln_b = nrm((N_LAYERS_B, D), 0.02)
    cv_w_pw2 = nrm((N_LAYERS_B, D, D), BETA * D ** -0.5)
    cv_b_pw2 = nrm((N_LAYERS_B, D), 0.02)
    qkv_width = (N_HEADS + 2 * N_KV_HEADS) * HEAD_DIM
    at_w_qkv = nrm((N_LAYERS_C, D, qkv_width), D ** -0.5)
    at_w_o = nrm((N_LAYERS_C, N_HEADS * HEAD_DIM, D), BETA * (N_HEADS * HEAD_DIM) ** -0.5)
    at_sink = nrm((N_LAYERS_C, N_HEADS), 0.5)
    moe_wg = nrm((DEPTH, D, N_EXPERT_GROUPS), D ** -0.5)
    moe_bg = nrm((DEPTH, N_EXPERT_GROUPS), 0.01)
    moe_we = nrm((DEPTH, D, N_EXPERTS), D ** -0.5)
    moe_be = nrm((DEPTH, N_EXPERTS), 0.01)
    moe_w1 = nrm((DEPTH, N_EXPERTS, D, 2 * D_EXPERT), D ** -0.5)
    moe_w2 = nrm((DEPTH, N_EXPERTS, D_EXPERT, D), BETA * D_EXPERT ** -0.5)
    return {'x': x, 'c': c, 'ctx': ctx, 'c_ctx': c_ctx, 'ada_w': ada_w, 'ada_b': ada_b, 'ln_g': ln_g, 'ln_b': ln_b,
            's5_lam_re': s5_lam_re, 's5_lam_im': s5_lam_im, 's5_log_dt': s5_log_dt, 's5_b_re': s5_b_re,
            's5_b_im': s5_b_im, 's5_c_re': s5_c_re, 's5_c_im': s5_c_im, 's5_d': s5_d, 's5_w_glu': s5_w_glu,
            'cv_w_pw1': cv_w_pw1, 'cv_b_pw1': cv_b_pw1, 'cv_w_dw': cv_w_dw, 'cv_b_dw': cv_b_dw,
            'cv_ln_g': cv_ln_g, 'cv_ln_b': cv_ln_b, 'cv_w_pw2': cv_w_pw2, 'cv_b_pw2': cv_b_pw2,
            'at_w_qkv': at_w_qkv, 'at_w_o': at_w_o, 'at_sink': at_sink,
            'moe_wg': moe_wg, 'moe_bg': moe_bg, 'moe_we': moe_we, 'moe_be': moe_be, 'moe_w1': moe_w1, 'moe_w2': moe_w2}


def reference(x, c, ctx, c_ctx, ada_w, ada_b, ln_g, ln_b,
              s5_lam_re, s5_lam_im, s5_log_dt, s5_b_re, s5_b_im, s5_c_re, s5_c_im, s5_d, s5_w_glu,
              cv_w_pw1, cv_b_pw1, cv_w_dw, cv_b_dw, cv_ln_g, cv_ln_b, cv_w_pw2, cv_b_pw2,
              at_w_qkv, at_w_o, at_sink,
              moe_wg, moe_bg, moe_we, moe_be, moe_w1, moe_w2):
    B, L, D = x.shape
    Lc = ctx.shape[1]
    ROWS = L // GRID_W
    rows = jnp.repeat(jnp.arange(ROWS), GRID_W)
    cols = jnp.tile(jnp.arange(GRID_W), ROWS)
    h, hc = x, ctx
    silu_c, silu_cc = jax.nn.silu(c), jax.nn.silu(c_ctx)
    for i in range(DEPTH):
        last = i == DEPTH - 1
        kind, j = i % N_MIXERS, i // N_MIXERS
        m_lat = (silu_c @ ada_w[i] + ada_b[i])[:, None, :]
        m_ctx = silu_cc @ ada_w[i] + ada_b[i]
        sh1, sc1, g1, sh2, sc2, g2 = jnp.split(m_lat, 6, axis=-1)
        csh1, csc1, cg1, csh2, csc2, cg2 = jnp.split(m_ctx, 6, axis=-1)
        u_lat = h * (1.0 + sc1) + sh1
        u_ctx = hc * (1.0 + csc1) + csh1
        if kind == 0:
            y_ctx, y_lat = _s5_mixer(u_ctx, u_lat, s5_lam_re[j], s5_lam_im[j], s5_log_dt[j], s5_b_re[j], s5_b_im[j],
                                     s5_c_re[j], s5_c_im[j], s5_d[j], s5_w_glu[j], not last)
        elif kind == 1:
            y_ctx, y_lat = _conv_mixer(u_ctx, u_lat, cv_w_pw1[j], cv_b_pw1[j], cv_w_dw[j], cv_b_dw[j],
                                       cv_ln_g[j], cv_ln_b[j], cv_w_pw2[j], cv_b_pw2[j], not last)
        else:
            y_ctx, y_lat = _attn_mixer(u_ctx, u_lat, at_w_qkv[j], at_w_o[j], at_sink[j], rows, cols, not last)
        h = _layer_norm(ALPHA * h + g1 * y_lat, ln_g[i, 0], ln_b[i, 0])
        v_lat = h * (1.0 + sc2) + sh2
        moe_p = (moe_wg[i], moe_bg[i], moe_we[i], moe_be[i], moe_w1[i], moe_w2[i])
        if not last:
            hc = _layer_norm(ALPHA * hc + cg1 * y_ctx, ln_g[i, 0], ln_b[i, 0])
            v_ctx = hc * (1.0 + csc2) + csh2
            tok = jnp.concatenate([v_ctx, v_lat], axis=1).reshape(-1, D)
            f = _hier_moe(tok, *moe_p).reshape(B, Lc + L, D)
            f_ctx, f_lat = f[:, :Lc], f[:, Lc:]
            hc = _layer_norm(ALPHA * hc + cg2 * f_ctx, ln_g[i, 1], ln_b[i, 1])
        else:
            f_lat = _hier_moe(v_lat.reshape(-1, D), *moe_p).reshape(B, L, D)
        h = _layer_norm(ALPHA * h + g2 * f_lat, ln_g[i, 1], ln_b[i, 1])
    return h
```

```python
import functools
import math

import jax
import jax.numpy as jnp
from jax import lax
from jax.experimental import pallas as pl
from jax.experimental.pallas import tpu as pltpu

F32 = jnp.float32
BF16 = jnp.bfloat16
HIGHEST = lax.Precision.HIGHEST

D_MODEL = 1024
DEPTH = 4
GRID_W = 64
SSM_CH_PER_GROUP = 16
SSM_GROUPS = D_MODEL // SSM_CH_PER_GROUP
SSM_STATE = 64
CONV_WIDTH = 31
HEAD_DIM = 64
N_HEADS = D_MODEL // HEAD_DIM
N_KV_HEADS = N_HEADS // 4
KV_REP = N_HEADS // N_KV_HEADS
WINDOW = 128
ATT_BLOCK = 128
ROPE_BASE = 10000.0
N_EXPERT_GROUPS = 4
EXPERTS_PER_GROUP = 8
N_EXPERTS = N_EXPERT_GROUPS * EXPERTS_PER_GROUP
D_EXPERT = D_MODEL // 2
ALPHA = (2 * DEPTH) ** 0.25
LN_EPS = 1e-5

SUBLANES = 8
LANES = 128

TOK_TILE = 256
EXPERT_TILE = 512
S5_CHUNK = 32
CONV_HALO = 16
ROUTE_LANES = LANES
META_COLS = 8
VMEM_LIMIT = 56 * 1024 * 1024


def _cparams(*sem):
    return pltpu.CompilerParams(dimension_semantics=sem, vmem_limit_bytes=VMEM_LIMIT)


def _round_up(n, m):
    return (n + m - 1) // m * m


def _mod_body(c_ref, w_ref, b_ref, o_ref):
    c = c_ref[...]
    s = c * jax.nn.sigmoid(c)
    o_ref[0] = jnp.dot(s, w_ref[0], preferred_element_type=F32, precision=HIGHEST) + b_ref[0]


def _modulation(c_all, ada_w, ada_b):
    depth, d, n = ada_w.shape
    r = c_all.shape[0]
    tn = 1024
    out = pl.pallas_call(
        _mod_body,
        grid=(depth, n // tn),
        in_specs=[
            pl.BlockSpec((r, d), lambda i, j: (0, 0)),
            pl.BlockSpec((1, d, tn), lambda i, j: (i, 0, j)),
            pl.BlockSpec((1, 1, tn), lambda i, j: (i, 0, j)),
        ],
        out_specs=pl.BlockSpec((1, r, tn), lambda i, j: (i, 0, j)),
        out_shape=jax.ShapeDtypeStruct((depth, r, n), F32),
        compiler_params=_cparams("arbitrary", "arbitrary"),
        name="adaln_modulation",
    )(c_all, ada_w, ada_b.reshape(depth, 1, n))
    return out.reshape(depth, r, 6, d)


def _layer_norm_rows(t, g, b):
    mu = jnp.mean(t, axis=-1, keepdims=True)
    dev = t - mu
    var = jnp.mean(dev * dev, axis=-1, keepdims=True)
    return dev * lax.rsqrt(var + LN_EPS) * g + b


def _post_norm_and_route(h, y, mod, lng, lnb, wr, br, cnt_ref):
    tm = h.shape[0]
    ng, ne = N_EXPERT_GROUPS, N_EXPERTS
    h1 = _layer_norm_rows(ALPHA * h + mod[2:3] * y, lng, lnb)
    v = h1 * (1.0 + mod[4:5]) + mod[3:4]
    logits = jnp.dot(v, wr, preferred_element_type=F32, precision=HIGHEST) + br

    lane = lax.broadcasted_iota(jnp.int32, (tm, ROUTE_LANES), 1)
    lane_f = lane.astype(F32)
    neg = -jnp.inf
    no_lane = float(ROUTE_LANES)
    is_group = lane < ng
    gl = jnp.where(is_group, logits, neg)
    gmax = jnp.max(gl, axis=-1, keepdims=True)
    gsum = jnp.sum(jnp.where(is_group, jnp.exp(logits - gmax), 0.0), axis=-1, keepdims=True)
    p_group = 1.0 / gsum
    g_idx = jnp.min(jnp.where(gl == gmax, lane_f, no_lane), axis=-1, keepdims=True)
    expert_group = ((lane - ng) // EXPERTS_PER_GROUP).astype(F32)
    in_group = (lane >= ng) & (lane < ng + ne) & (expert_group == g_idx)
    el = jnp.where(in_group, logits, neg)
    v1 = jnp.max(el, axis=-1, keepdims=True)
    i1 = jnp.min(jnp.where(el == v1, lane_f, no_lane), axis=-1, keepdims=True)
    el2 = jnp.where(lane_f == i1, neg, el)
    v2 = jnp.max(el2, axis=-1, keepdims=True)
    i2 = jnp.min(jnp.where(el2 == v2, lane_f, no_lane), axis=-1, keepdims=True)
    e21 = jnp.exp(v2 - v1)
    w1 = p_group / (1.0 + e21)
    w2 = p_group * e21 / (1.0 + e21)

    hit1 = lane_f == i1
    hit2 = lane_f == i2
    one1 = hit1.astype(BF16)
    one2 = hit2.astype(BF16)
    rr = lax.broadcasted_iota(jnp.int32, (tm, tm), 0)
    cc = lax.broadcasted_iota(jnp.int32, (tm, tm), 1)
    before = (cc < rr).astype(BF16)
    cum1 = jnp.dot(before, one1, preferred_element_type=F32)
    cum2 = jnp.dot(before, one2, preferred_element_type=F32)
    tot1 = jnp.sum(one1.astype(F32), axis=0, keepdims=True)
    tot2 = jnp.sum(one2.astype(F32), axis=0, keepdims=True)
    cnt = cnt_ref[...]
    rank1 = jnp.sum(jnp.where(hit1, cnt + cum1, 0.0), axis=-1, keepdims=True)
    rank2 = jnp.sum(jnp.where(hit2, cnt + tot1 + cum2, 0.0), axis=-1, keepdims=True)
    cnt_ref[...] = cnt + tot1 + tot2

    col = lax.broadcasted_iota(jnp.int32, (tm, META_COLS), 1)
    meta = jnp.where(col == 0, i1 - ng,
           jnp.where(col == 1, i2 - ng,
           jnp.where(col == 2, rank1,
           jnp.where(col == 3, rank2,
           jnp.where(col == 4, w1,
           jnp.where(col == 5, w2, 0.0))))))
    return h1, v, meta


def _gelu_tanh(x):
    return 0.5 * x * (1.0 + jnp.tanh(math.sqrt(2.0 / math.pi) * (x + 0.044715 * (x * x * x))))


class _Layout:
    def __init__(self, batch, seq, ctx_len):
        self.batch, self.seq, self.ctx_len = batch, seq, ctx_len
        self.n_lat = batch * seq
        self.n_ctx = batch * ctx_len
        self.n_tok = self.n_lat + self.n_ctx
        assert seq % TOK_TILE == 0 and ctx_len % TOK_TILE == 0
        self.lat_tiles = self.n_lat // TOK_TILE
        self.tiles = self.n_tok // TOK_TILE
        self.tiles_per_seq = seq // TOK_TILE
        self.tiles_per_ctx = ctx_len // TOK_TILE
        self.mod_rows = _round_up(batch + 1, SUBLANES)

    def mod_row(self, i):
        return jnp.minimum(i // self.tiles_per_seq, self.batch)


def _mod_spec(lay, layer):
    return pl.BlockSpec((1, 1, 6, D_MODEL), lambda i, *_: (layer, lay.mod_row(i), 0, 0))


def _row_spec(tm=TOK_TILE, d=D_MODEL):
    return pl.BlockSpec((tm, d), lambda i, *_: (i, 0))


def _const_spec(shape):
    nd = len(shape)
    return pl.BlockSpec(shape, lambda i, *_: (0,) * nd)


def _modulate_body(h_ref, mod_ref, u_ref):
    mod = mod_ref[0, 0]
    u_ref[...] = (h_ref[...] * (1.0 + mod[1:2]) + mod[0:1]).astype(u_ref.dtype)


def _modulate(lay, h, mods, layer):
    return pl.pallas_call(
        _modulate_body,
        grid=(lay.tiles,),
        in_specs=[_row_spec(), _mod_spec(lay, layer)],
        out_specs=_row_spec(),
        out_shape=jax.ShapeDtypeStruct((lay.n_tok, D_MODEL), BF16),
        compiler_params=_cparams("arbitrary"),
        name="input_modulate",
    )(h, mods)


def _dispatch_body(pos_ref, v_ref, xs_ref, sem):
    tm = v_ref.shape[0]

    def issue(t, carry):
        for k in range(2):
            p = pos_ref[0, 0, 2 * t + k]
            pltpu.make_async_copy(v_ref.at[pl.ds(t, 1)], xs_ref.at[pl.ds(p, 1)], sem).start()
        return carry

    lax.fori_loop(0, tm, issue, 0)
    for k in range(2):
        pltpu.make_async_copy(v_ref, xs_ref.at[pl.ds(0, tm)], sem).wait()


def _moe_dispatch(v, pos_tiles, n_rows):
    n_tiles = pos_tiles.shape[0]
    return pl.pallas_call(
        _dispatch_body,
        grid=(n_tiles,),
        in_specs=[
            pl.BlockSpec((1, 1, 2 * TOK_TILE), lambda i: (i, 0, 0), memory_space=pltpu.SMEM),
            _row_spec(),
        ],
        out_specs=pl.BlockSpec(memory_space=pl.ANY),
        out_shape=jax.ShapeDtypeStruct((n_rows, D_MODEL), F32),
        scratch_shapes=[pltpu.SemaphoreType.DMA(())],
        compiler_params=_cparams("arbitrary"),
        name="moe_dispatch",
    )(pos_tiles, v)


def _expert_body(tile_ref, exp_ref, lo_ref, hi_ref, xs_ref, w1_ref, w2_ref, ys_ref, w1b, w2b):
    j = pl.program_id(0)
    jp = jnp.maximum(j - 1, 0)
    new_expert = (j == 0) | (exp_ref[j] != exp_ref[jp])
    first_visit = (j == 0) | (tile_ref[j] != tile_ref[jp])
    lo, hi = lo_ref[j], hi_ref[j]

    @pl.when(new_expert)
    def _():
        w1b[...] = w1_ref[0].astype(BF16)
        w2b[...] = w2_ref[0].astype(BF16)

    def expert_rows():
        x = xs_ref[...].astype(BF16)
        gu = jnp.dot(x, w1b[...], preferred_element_type=F32)
        gate, up = gu[:, :D_EXPERT], gu[:, D_EXPERT:]
        a = (gate * jax.nn.sigmoid(gate) * up).astype(BF16)
        y = jnp.dot(a, w2b[...], preferred_element_type=F32)
        rows = lax.broadcasted_iota(jnp.int32, (xs_ref.shape[0], 1), 0)
        return y, (rows >= lo) & (rows < hi)

    @pl.when(first_visit)
    def _():
        y, mine = expert_rows()
        ys_ref[...] = jnp.where(mine, y, 0.0)

    @pl.when(jnp.logical_not(first_visit) & (hi > lo))
    def _():
        y, mine = expert_rows()
        ys_ref[...] = jnp.where(mine, y, ys_ref[...])


def _moe_experts(xs, w1, w2, items):
    tile_j, exp_j, lo_j, hi_j = items
    n_items = tile_j.shape[0]
    grid_spec = pltpu.PrefetchScalarGridSpec(
        num_scalar_prefetch=4,
        grid=(n_items,),
        in_specs=[
            pl.BlockSpec((EXPERT_TILE, D_MODEL), lambda j, t, e, lo, hi: (t[j], 0)),
            pl.BlockSpec((1, D_MODEL, 2 * D_EXPERT), lambda j, t, e, lo, hi: (e[j], 0, 0)),
            pl.BlockSpec((1, D_EXPERT, D_MODEL), lambda j, t, e, lo, hi: (e[j], 0, 0)),
        ],
        out_specs=pl.BlockSpec((EXPERT_TILE, D_MODEL), lambda j, t, e, lo, hi: (t[j], 0)),
        scratch_shapes=[pltpu.VMEM((D_MODEL, 2 * D_EXPERT), BF16), pltpu.VMEM((D_EXPERT, D_MODEL), BF16)],
    )
    return pl.pallas_call(
        _expert_body,
        grid_spec=grid_spec,
        out_shape=jax.ShapeDtypeStruct(xs.shape, F32),
        compiler_params=_cparams("arbitrary"),
        name="moe_experts",
    )(tile_j, exp_j, lo_j, hi_j, xs, w1, w2)


def _combine_body(has_next, pos_ref, posn_ref, meta_ref, h1_ref, mod_ref, modn_ref, lng_ref, lnb_ref,
                  ys_ref, *rest):
    if has_next:
        h2_ref, u_ref, buf, sem = rest
    else:
        h2_ref, buf, sem = rest
    tm = h1_ref.shape[0]
    i = pl.program_id(0)
    n = pl.num_programs(0)
    slot = i % 2

    def gather(pref, s):
        def issue(t, carry):
            for k in range(2):
                p = pref[0, 0, 2 * t + k]
                pltpu.make_async_copy(ys_ref.at[pl.ds(p, 1)], buf.at[s, k, pl.ds(t, 1)], sem.at[s]).start()
            return carry
        lax.fori_loop(0, tm, issue, 0)

    @pl.when(i == 0)
    def _():
        gather(pos_ref, 0)

    @pl.when(i + 1 < n)
    def _():
        gather(posn_ref, 1 - slot)

    for k in range(2):
        pltpu.make_async_copy(ys_ref.at[pl.ds(0, tm)], buf.at[slot, k], sem.at[slot]).wait()

    meta = meta_ref[...]
    f = meta[:, 4:5] * buf[slot, 0] + meta[:, 5:6] * buf[slot, 1]
    mod = mod_ref[0, 0]
    h2 = _layer_norm_rows(ALPHA * h1_ref[...] + mod[5:6] * f, lng_ref[...], lnb_ref[...])
    h2_ref[...] = h2
    if has_next:
        modn = modn_ref[0, 0]
        u_ref[...] = (h2 * (1.0 + modn[1:2]) + modn[0:1]).astype(u_ref.dtype)


def _moe_combine(lay, n_tiles, ys, pos_tiles, meta, h1, mods, layer, lng, lnb, has_next):
    n_rows = n_tiles * TOK_TILE
    nxt = min(layer + 1, DEPTH - 1)
    pos_spec = lambda shift: pl.BlockSpec(
        (1, 1, 2 * TOK_TILE), lambda i: (jnp.minimum(i + shift, n_tiles - 1), 0, 0), memory_space=pltpu.SMEM)
    out_shape = [jax.ShapeDtypeStruct((n_rows, D_MODEL), F32)]
    out_specs = [_row_spec()]
    if has_next:
        out_shape.append(jax.ShapeDtypeStruct((n_rows, D_MODEL), BF16))
        out_specs.append(_row_spec())
    outs = pl.pallas_call(
        functools.partial(_combine_body, has_next),
        grid=(n_tiles,),
        in_specs=[
            pos_spec(0), pos_spec(1),
            _row_spec(d=META_COLS), _row_spec(), _mod_spec(lay, layer), _mod_spec(lay, nxt),
            _const_spec((1, D_MODEL)), _const_spec((1, D_MODEL)),
            pl.BlockSpec(memory_space=pl.ANY),
        ],
        out_specs=out_specs,
        out_shape=out_shape,
        scratch_shapes=[pltpu.VMEM((2, 2, TOK_TILE, D_MODEL), F32), pltpu.SemaphoreType.DMA((2,))],
        compiler_params=_cparams("arbitrary"),
        name="moe_combine",
    )(pos_tiles, pos_tiles, meta, h1, mods, mods, lng, lnb, ys)
    return outs if has_next else (outs[0], None)


def _moe_plan(meta, counts_f, n_tiles):
    n_assign = n_tiles * TOK_TILE * 2
    assert n_assign % EXPERT_TILE == 0
    n_etiles = n_assign // EXPERT_TILE
    counts = counts_f[0, N_EXPERT_GROUPS:N_EXPERT_GROUPS + N_EXPERTS].astype(jnp.int32)
    ends = jnp.cumsum(counts)
    starts = ends - counts
    eid = meta[:, 0:2].astype(jnp.int32)
    rank = meta[:, 2:4].astype(jnp.int32)
    pos = jnp.take(starts, eid) + rank
    pos_tiles = pos.reshape(n_tiles, 1, 2 * TOK_TILE)

    first_tile = starts // EXPERT_TILE
    last_tile = jnp.maximum(ends - 1, 0) // EXPERT_TILE
    n_items_e = jnp.where(counts > 0, last_tile - first_tile + 1, 0)
    item_end = jnp.cumsum(n_items_e)
    item_start = item_end - n_items_e
    n_items = n_etiles + N_EXPERTS
    j = jnp.arange(n_items, dtype=jnp.int32)
    e_j = jnp.minimum(jnp.searchsorted(item_end, j, side="right"), N_EXPERTS - 1).astype(jnp.int32)
    active = j < item_end[-1]
    tile_j = jnp.take(first_tile, e_j) + (j - jnp.take(item_start, e_j))
    lo = jnp.maximum(jnp.take(starts, e_j), tile_j * EXPERT_TILE) - tile_j * EXPERT_TILE
    hi = jnp.minimum(jnp.take(ends, e_j), (tile_j + 1) * EXPERT_TILE) - tile_j * EXPERT_TILE
    last = jnp.maximum(item_end[-1] - 1, 0)
    tile_j = jnp.where(active, tile_j, tile_j[last]).astype(jnp.int32)
    e_j = jnp.where(active, e_j, e_j[last]).astype(jnp.int32)
    lo = jnp.where(active, lo, 0).astype(jnp.int32)
    hi = jnp.where(active, hi, 0).astype(jnp.int32)
    return pos_tiles, (tile_j, e_j, lo, hi), n_assign


def _route_outputs(n_rows):
    shapes = [jax.ShapeDtypeStruct((n_rows, D_MODEL), F32),
              jax.ShapeDtypeStruct((n_rows, D_MODEL), F32),
              jax.ShapeDtypeStruct((n_rows, META_COLS), F32),
              jax.ShapeDtypeStruct((1, ROUTE_LANES), F32)]
    specs = [_row_spec(), _row_spec(), _row_spec(d=META_COLS), _const_spec((1, ROUTE_LANES))]
    return shapes, specs


def _route_inputs(lay, layer):
    return [_row_spec(), _mod_spec(lay, layer), _const_spec((1, D_MODEL)), _const_spec((1, D_MODEL)),
            _const_spec((D_MODEL, ROUTE_LANES)), _const_spec((1, ROUTE_LANES))]


def _finish_route(y, h_ref, mod_ref, lng_ref, lnb_ref, wr_ref, br_ref, h1_ref, v_ref, meta_ref, cnt_ref, acc):
    @pl.when(pl.program_id(0) == 0)
    def _():
        acc[...] = jnp.zeros_like(acc)

    h1, v, meta = _post_norm_and_route(h_ref[...], y, mod_ref[0, 0], lng_ref[...], lnb_ref[...],
                                       wr_ref[...], br_ref[...], acc)
    h1_ref[...] = h1
    v_ref[...] = v
    meta_ref[...] = meta
    cnt_ref[...] = acc[...]


def _attn_out_body(o_ref, wo_ref, *rest):
    y = jnp.dot(o_ref[...], wo_ref[...], preferred_element_type=F32)
    _finish_route(y, *rest)


def _attn_out(lay, n_tiles, layer, o, w_o, h, mods, lng, lnb, wr, br):
    shapes, specs = _route_outputs(n_tiles * TOK_TILE)
    return pl.pallas_call(
        _attn_out_body,
        grid=(n_tiles,),
        in_specs=[_row_spec(), _const_spec((D_MODEL, D_MODEL))] + _route_inputs(lay, layer),
        out_specs=specs,
        out_shape=shapes,
        scratch_shapes=[pltpu.VMEM((1, ROUTE_LANES), F32)],
        compiler_params=_cparams("arbitrary"),
        name="attn_out_route",
    )(o, w_o, h, mods, lng, lnb, wr, br)


def _rope_tables(seq):
    quarter = HEAD_DIM // 4
    inv_freq = ROPE_BASE ** (-jnp.arange(quarter, dtype=F32) / quarter)
    t = jnp.arange(seq)
    rows = (t // GRID_W).astype(F32)
    cols = (t % GRID_W).astype(F32)
    ang_r = rows[:, None] * inv_freq
    ang_c = cols[:, None] * inv_freq
    cos_h = jnp.concatenate([jnp.cos(ang_r), jnp.cos(ang_r), jnp.cos(ang_c), jnp.cos(ang_c)], axis=-1)
    sin_h = jnp.concatenate([-jnp.sin(ang_r), jnp.sin(ang_r), -jnp.sin(ang_c), jnp.sin(ang_c)], axis=-1)
    rep = LANES // HEAD_DIM
    return jnp.tile(cos_h, (1, rep)), jnp.tile(sin_h, (1, rep))


def _rope(x, cos, sin):
    quarter = HEAD_DIM // 4
    width = x.shape[1]
    n = width // LANES
    c = jnp.tile(cos, (1, n))
    s = jnp.tile(sin, (1, n))
    lane = lax.broadcasted_iota(jnp.int32, x.shape, 1)
    is_lo = (lane % (2 * quarter)) < quarter
    partner = jnp.where(is_lo, pltpu.roll(x, width - quarter, 1), pltpu.roll(x, quarter, 1))
    return x * c + partner * s


def _qkv_body(lat_tiles, u_ref, w_ref, cos_ref, sin_ref, q_ref, k_ref, v_ref):
    is_ctx = pl.program_id(0) >= lat_tiles
    qkv = jnp.dot(u_ref[...], w_ref[...], preferred_element_type=F32)
    nq, nkv = N_HEADS * HEAD_DIM, N_KV_HEADS * HEAD_DIM
    q, k, v = qkv[:, :nq], qkv[:, nq:nq + nkv], qkv[:, nq + nkv:]
    cos, sin = cos_ref[...], sin_ref[...]
    q = jnp.where(is_ctx, q, _rope(q, cos, sin))
    k = jnp.where(is_ctx, k, _rope(k, cos, sin))
    q_ref[...] = (q * HEAD_DIM ** -0.5).astype(q_ref.dtype)
    k_ref[...] = k.astype(k_ref.dtype)
    v_ref[...] = v.astype(v_ref.dtype)


def _qkv_rope(lay, u, w_qkv, cos_t, sin_t):
    nq, nkv = N_HEADS * HEAD_DIM, N_KV_HEADS * HEAD_DIM
    tab = pl.BlockSpec((TOK_TILE, LANES),
                       lambda i: (jnp.where(i < lay.lat_tiles, i % lay.tiles_per_seq, 0), 0))
    return pl.pallas_call(
        functools.partial(_qkv_body, lay.lat_tiles),
        grid=(lay.tiles,),
        in_specs=[_row_spec(), _const_spec((D_MODEL, nq + 2 * nkv)), tab, tab],
        out_specs=[_row_spec(), _row_spec(d=nkv), _row_spec(d=nkv)],
        out_shape=[jax.ShapeDtypeStruct((lay.n_tok, nq), BF16),
                   jax.ShapeDtypeStruct((lay.n_tok, nkv), BF16),
                   jax.ShapeDtypeStruct((lay.n_tok, nkv), BF16)],
        compiler_params=_cparams("arbitrary"),
        name="attn_qkv_rope",
    )(u, w_qkv, cos_t, sin_t)


def _attn_body(nqb, sink_ref, q_ref, kp_ref, kc_ref, kn_ref, kx_ref, vp_ref, vc_ref, vn_ref, vx_ref,
               o_ref, kcat, vcat):
    j = pl.program_id(1)
    blk = ATT_BLOCK
    n_win = 3 * blk
    for dst, parts in ((kcat, (kp_ref, kc_ref, kn_ref)), (vcat, (vp_ref, vc_ref, vn_ref))):
        for n, part in enumerate(parts):
            dst[n * blk:(n + 1) * blk] = part[...]
    kcat[n_win:] = kx_ref[...]
    vcat[n_win:] = vx_ref[...]
    r = lax.broadcasted_iota(jnp.int32, (blk, n_win), 0)
    c = lax.broadcasted_iota(jnp.int32, (blk, n_win), 1)
    valid = (jnp.abs(r + blk - c) <= WINDOW) & (j < nqb)
    valid = valid & ((c >= blk) | (j > 0)) & ((c < 2 * blk) | (j < nqb - 1))
    for hd in range(N_HEADS):
        g = hd // KV_REP
        head = slice(hd * HEAD_DIM, (hd + 1) * HEAD_DIM)
        kv_head = slice(g * HEAD_DIM, (g + 1) * HEAD_DIM)
        s = lax.dot_general(q_ref[:, head], kcat[:, kv_head], (((1,), (1,)), ((), ())),
                            preferred_element_type=F32)
        sink = sink_ref[0, hd]
        sw = jnp.where(valid, s[:, :n_win], -jnp.inf)
        sc = s[:, n_win:]
        m = jnp.maximum(jnp.max(sw, axis=-1, keepdims=True), jnp.max(sc, axis=-1, keepdims=True))
        m = jnp.maximum(m, sink)
        pw = jnp.exp(sw - m)
        pc = jnp.exp(sc - m)
        denom = (jnp.sum(pw, axis=-1, keepdims=True) + jnp.sum(pc, axis=-1, keepdims=True)
                 + jnp.exp(sink - m))
        acc = jnp.dot(pw.astype(BF16), vcat[:n_win, kv_head], preferred_element_type=F32)
        acc = acc + jnp.dot(pc.astype(BF16), vcat[n_win:, kv_head], preferred_element_type=F32)
        o_ref[:, head] = (acc / denom).astype(o_ref.dtype)


def _attention(lay, q, k, v, sink, with_ctx_out):
    nq, nkv = N_HEADS * HEAD_DIM, N_KV_HEADS * HEAD_DIM
    blk = ATT_BLOCK
    nqb = lay.seq // blk
    lc = lay.ctx_len
    assert lay.n_lat % lc == 0 and lay.seq % blk == 0 and lc % blk == 0
    ctx0 = lay.n_lat // lc
    ncb = lc // blk
    steps = nqb + (ncb if with_ctx_out else 0)
    q_blk = pl.BlockSpec(
        (blk, nq), lambda b, j: (jnp.where(j < nqb, b * nqb + j, (ctx0 + b) * ncb + j - nqb), 0))
    kv_blk = lambda shift: pl.BlockSpec(
        (blk, nkv), lambda b, j: (b * nqb + jnp.clip(j + shift, 0, nqb - 1), 0))
    kv_ctx = pl.BlockSpec((lc, nkv), lambda b, j: (ctx0 + b, 0))
    return pl.pallas_call(
        functools.partial(_attn_body, nqb),
        grid=(lay.batch, steps),
        in_specs=[pl.BlockSpec(memory_space=pltpu.SMEM), q_blk,
                  kv_blk(-1), kv_blk(0), kv_blk(1), kv_ctx,
                  kv_blk(-1), kv_blk(0), kv_blk(1), kv_ctx],
        out_specs=q_blk,
        out_shape=jax.ShapeDtypeStruct((lay.n_tok if with_ctx_out else lay.n_lat, nq), BF16),
        scratch_shapes=[pltpu.VMEM((3 * blk + lc, nkv), BF16), pltpu.VMEM((3 * blk + lc, nkv), BF16)],
        compiler_params=_cparams("arbitrary", "arbitrary"),
        name="attn_window",
    )(sink, q, k, k, k, k, v, v, v, v)


def _pw1_body(u_ref, wa_ref, wg_ref, ba_ref, bg_ref, z_ref):
    u = u_ref[...]
    a = jnp.dot(u, wa_ref[...], preferred_element_type=F32) + ba_ref[...]
    g = jnp.dot(u, wg_ref[...], preferred_element_type=F32) + bg_ref[...]
    z_ref[...] = (a * jax.nn.sigmoid(g)).astype(z_ref.dtype)


def _conv_pw1(lay, u, wa, wg, ba, bg):
    return pl.pallas_call(
        _pw1_body,
        grid=(lay.tiles,),
        in_specs=[_row_spec(), _const_spec((D_MODEL, D_MODEL)), _const_spec((D_MODEL, D_MODEL)),
                  _const_spec((1, D_MODEL)), _const_spec((1, D_MODEL))],
        out_specs=_row_spec(),
        out_shape=jax.ShapeDtypeStruct((lay.n_tok, D_MODEL), BF16),
        compiler_params=_cparams("arbitrary"),
        name="conv_pw1_glu",
    )(u, wa, wg, ba, bg)


def _conv_body(lat_tiles, tps, tpc, zp_ref, z_ref, zn_ref, wdw_ref, bdw_ref, cg_ref, cb_ref, w2_ref, b2_ref,
               h_ref, mod_ref, lng_ref, lnb_ref, wr_ref, br_ref, h1_ref, v_ref, meta_ref, cnt_ref, acc, win):
    i = pl.program_id(0)
    tm = z_ref.shape[0]
    halo = CONV_HALO
    is_ctx = i >= lat_tiles
    pos = jnp.where(is_ctx, (i - lat_tiles) % tpc, i % tps)
    first = pos == 0
    last = pos == jnp.where(is_ctx, tpc, tps) - 1
    win[0:halo] = jnp.where(first, 0.0, zp_ref[...].astype(F32))
    win[halo:halo + tm] = z_ref[...].astype(F32)
    win[halo + tm:] = jnp.where(last, 0.0, zn_ref[...].astype(F32))
    off = halo - CONV_WIDTH // 2
    conv = jnp.zeros((tm, D_MODEL), F32)
    for tap in range(CONV_WIDTH):
        conv = conv + wdw_ref[tap:tap + 1, :] * win[off + tap:off + tap + tm, :]
    conv = conv + bdw_ref[...]
    nrm = _layer_norm_rows(conv, cg_ref[...], cb_ref[...])
    act = (nrm * jax.nn.sigmoid(nrm)).astype(BF16)
    y = jnp.dot(act, w2_ref[...], preferred_element_type=F32) + b2_ref[...]
    _finish_route(y, h_ref, mod_ref, lng_ref, lnb_ref, wr_ref, br_ref, h1_ref, v_ref, meta_ref, cnt_ref, acc)


def _conv_out(lay, n_tiles, layer, z, w_dw, b_dw, cg, cb, w2, b2, h, mods, lng, lnb, wr, br):
    per_tile = TOK_TILE // CONV_HALO
    n_halo = lay.n_tok // CONV_HALO
    shapes, specs = _route_outputs(n_tiles * TOK_TILE)
    halo_spec = lambda f: pl.BlockSpec((CONV_HALO, D_MODEL), lambda i: (jnp.clip(f(i), 0, n_halo - 1), 0))
    return pl.pallas_call(
        functools.partial(_conv_body, lay.lat_tiles, lay.tiles_per_seq, lay.tiles_per_ctx),
        grid=(n_tiles,),
        in_specs=[halo_spec(lambda i: i * per_tile - 1), _row_spec(), halo_spec(lambda i: (i + 1) * per_tile),
                  _const_spec((CONV_WIDTH, D_MODEL)), _const_spec((1, D_MODEL)), _const_spec((1, D_MODEL)),
                  _const_spec((1, D_MODEL)), _const_spec((D_MODEL, D_MODEL)), _const_spec((1, D_MODEL))]
                 + _route_inputs(lay, layer),
        out_specs=specs,
        out_shape=shapes,
        scratch_shapes=[pltpu.VMEM((1, ROUTE_LANES), F32), pltpu.VMEM((TOK_TILE + 2 * CONV_HALO, D_MODEL), F32)],
        compiler_params=_cparams("arbitrary"),
        name="conv_dw_out_route",
    )(z, z, z, w_dw, b_dw, cg, cb, w2, b2, h, mods, lng, lnb, wr, br)


def _s5_prep_body(lrow_ref, lcol_ref, bt_ref, ct_ref, mintra_ref, min_ref, mout_ref, arec_ref):
    tc, ch, ns = S5_CHUNK, SSM_CH_PER_GROUP, SSM_STATE
    width = tc * ch
    lrow = lrow_ref[0]
    lcol = lcol_ref[0]
    sel = (lax.broadcasted_iota(jnp.int32, (ch, width), 0)
           == lax.broadcasted_iota(jnp.int32, (ch, width), 1) % ch).astype(F32)
    lane_t = (lax.broadcasted_iota(jnp.int32, (1, width), 1) // ch).astype(F32)
    row_t = (lax.broadcasted_iota(jnp.int32, (width, 1), 0) // ch).astype(F32)
    sign = jnp.where(lax.broadcasted_iota(jnp.int32, (1, 2 * ns), 1) < ns, -1.0, 1.0)
    krows = []
    for d in range(2):
        lr2 = jnp.minimum(lrow[3 * d:3 * d + 1], -1e-4)
        li2 = lrow[3 * d + 1:3 * d + 2]
        dt2 = jnp.exp(lrow[3 * d + 2:3 * d + 3])
        lr, li, dt = lr2[:, :ns], li2[:, :ns], dt2[:, :ns]
        lrc = jnp.minimum(lcol[:, 3 * d:3 * d + 1], -1e-4)
        lic = lcol[:, 3 * d + 1:3 * d + 2]
        dtc = jnp.exp(lcol[:, 3 * d + 2:3 * d + 3])
        er = jnp.exp(lr * dt)
        xr = er * jnp.cos(li * dt) - 1.0
        xi = er * jnp.sin(li * dt)
        den = lr * lr + li * li
        qr = (xr * lr + xi * li) / den
        qi = (xi * lr - xr * li) / den
        btr, bti = bt_ref[0, 2 * d], bt_ref[0, 2 * d + 1]
        bbr = qr * btr - qi * bti
        bbi = qr * bti + qi * btr
        n_in = (tc - 1.0 - row_t) if d == 0 else row_t
        mag = jnp.exp(n_in * (lr * dt))
        ang = n_in * (li * dt)
        pr, pi = mag * jnp.cos(ang), mag * jnp.sin(ang)
        bbr_t, bbi_t = jnp.tile(bbr, (tc, 1)), jnp.tile(bbi, (tc, 1))
        base = 2 * ns * d
        min_ref[0, :, base:base + ns] = (pr * bbr_t - pi * bbi_t).astype(BF16)
        min_ref[0, :, base + ns:base + 2 * ns] = (pr * bbi_t + pi * bbr_t).astype(BF16)
        ctr = jnp.dot(ct_ref[0, 2 * d], sel, preferred_element_type=F32, precision=HIGHEST)
        cti = jnp.dot(ct_ref[0, 2 * d + 1], sel, preferred_element_type=F32, precision=HIGHEST)

        def readout(n_lane):
            mg = jnp.exp(n_lane * (lrc * dtc))
            an = n_lane * (lic * dtc)
            wr_, wi_ = mg * jnp.cos(an), mg * jnp.sin(an)
            return ctr * wr_ - cti * wi_, -(ctr * wi_ + cti * wr_)

        o_re, o_imneg = readout(lane_t + 1.0 if d == 0 else tc - lane_t)
        mout_ref[0, base:base + ns, :] = o_re.astype(BF16)
        mout_ref[0, base + ns:base + 2 * ns, :] = o_imneg.astype(BF16)
        k_re, k_imneg = readout(lane_t if d == 0 else tc - 1.0 - lane_t)
        krows.append(jnp.dot(bbr, k_re, preferred_element_type=F32, precision=HIGHEST)
                     + jnp.dot(bbi, k_imneg, preferred_element_type=F32, precision=HIGHEST))
        e2 = jnp.exp(tc * lr2 * dt2)
        arec_ref[0, 2 * d:2 * d + 1] = e2 * jnp.cos(tc * li2 * dt2)
        arec_ref[0, 2 * d + 1:2 * d + 2] = e2 * jnp.sin(tc * li2 * dt2) * sign
    lane = lax.broadcasted_iota(jnp.int32, (ch, width), 1)
    for s in range(tc):
        fwd = krows[0] if s == 0 else jnp.where(lane >= ch * s, pltpu.roll(krows[0], ch * s, 1), 0.0)
        back = ch * (tc - 1 - s)
        rev = krows[1] if back == 0 else jnp.where(lane < ch * (s + 1), pltpu.roll(krows[1], width - back, 1), 0.0)
        mintra_ref[0, ch * s:ch * (s + 1), :] = (fwd + rev).astype(BF16)


def _s5_prepare(lam_re, lam_im, log_dt, b_re, b_im, c_re, c_im):
    g, ns, ch, tc = SSM_GROUPS, SSM_STATE, SSM_CH_PER_GROUP, S5_CHUNK
    width = tc * ch
    ldt = jnp.broadcast_to(log_dt[:, :, None], lam_re.shape)
    stack = jnp.stack([lam_re[0], lam_im[0], ldt[0], lam_re[1], lam_im[1], ldt[1]], axis=1)
    lrow = jnp.concatenate([stack, stack], axis=-1)
    lcol = jnp.swapaxes(stack, 1, 2)
    bt = jnp.stack([b_re[0], b_im[0], b_re[1], b_im[1]], axis=1).swapaxes(2, 3)
    ct = jnp.stack([c_re[0], c_im[0], c_re[1], c_im[1]], axis=1).swapaxes(2, 3)
    blk = lambda *s: pl.BlockSpec((1,) + s, lambda i: (i,) + (0,) * len(s))
    return pl.pallas_call(
        _s5_prep_body,
        grid=(g,),
        in_specs=[blk(6, 2 * ns), blk(ns, 6), blk(4, ch, ns), blk(4, ns, ch)],
        out_specs=[blk(width, width), blk(width, 4 * ns), blk(4 * ns, width), blk(4, 2 * ns)],
        out_shape=[jax.ShapeDtypeStruct((g, width, width), BF16),
                   jax.ShapeDtypeStruct((g, width, 4 * ns), BF16),
                   jax.ShapeDtypeStruct((g, 4 * ns, width), BF16),
                   jax.ShapeDtypeStruct((g, 4, 2 * ns), F32)],
        compiler_params=_cparams("arbitrary"),
        name="s5_prepare",
    )(lrow, lcol, bt, ct)


def _s5_core_body(nb, nkc, v_ref, mintra_ref, min_ref, mout_ref, arec_ref, y_ref, s_ref, h_ref):
    ns2 = 2 * SSM_STATE
    v = v_ref[0]
    s_ref[...] = jnp.dot(v, min_ref[0], preferred_element_type=F32)
    nk = v.shape[0] // nb
    arec = arec_ref[0]
    a1f, a2f, a1r, a2r = arec[0:1], arec[1:2], arec[2:3], arec[3:4]

    def step(i, carry):
        hf, hr = carry
        kr = jnp.where(i < nkc, nkc - 1 - i, nk - 1 - (i - nkc))
        of = pl.multiple_of(i * nb, nb)
        orv = pl.multiple_of(kr * nb, nb)
        h_ref[pl.ds(of, nb), 0:ns2] = hf
        h_ref[pl.ds(orv, nb), ns2:2 * ns2] = hr
        hf = a1f * hf + a2f * pltpu.roll(hf, SSM_STATE, 1) + s_ref[pl.ds(of, nb), 0:ns2]
        hr = a1r * hr + a2r * pltpu.roll(hr, SSM_STATE, 1) + s_ref[pl.ds(orv, nb), ns2:2 * ns2]
        return hf, hr

    zero = jnp.zeros((nb, ns2), F32)
    lax.fori_loop(0, nk, step, (zero, zero))
    y = jnp.dot(v, mintra_ref[0], preferred_element_type=F32)
    y = y + jnp.dot(h_ref[...].astype(BF16), mout_ref[0], preferred_element_type=F32)
    y_ref[0] = y.astype(y_ref.dtype)


def _s5_core(lay, u, mats):
    mintra, m_in, m_out, arec = mats
    g, ch, tc = SSM_GROUPS, SSM_CH_PER_GROUP, S5_CHUNK
    width = tc * ch
    b = lay.batch
    nb = _round_up(b, SUBLANES)
    nkl, nkc = lay.seq // tc, lay.ctx_len // tc
    nk = nkl + nkc
    lat = u[:lay.n_lat].reshape(b, nkl, tc, g, ch)
    ctx = u[lay.n_lat:].reshape(b, nkc, tc, g, ch)
    x = jnp.concatenate([ctx, lat], axis=1).transpose(3, 1, 0, 2, 4)
    x = jnp.pad(x, ((0, 0), (0, 0), (0, nb - b), (0, 0), (0, 0))).reshape(g, nk * nb, width)
    rows = nk * nb
    blk = lambda *s: pl.BlockSpec((1,) + s, lambda i: (i,) + (0,) * len(s))
    y = pl.pallas_call(
        functools.partial(_s5_core_body, nb, nkc),
        grid=(g,),
        in_specs=[blk(rows, width), blk(width, width), blk(width, 4 * SSM_STATE), blk(4 * SSM_STATE, width),
                  blk(4, 2 * SSM_STATE)],
        out_specs=blk(rows, width),
        out_shape=jax.ShapeDtypeStruct((g, rows, width), BF16),
        scratch_shapes=[pltpu.VMEM((rows, 4 * SSM_STATE), F32), pltpu.VMEM((rows, 4 * SSM_STATE), F32)],
        compiler_params=_cparams("arbitrary"),
        name="s5_scan",
    )(x, mintra, m_in, m_out, arec)
    y = y.reshape(g, nk, nb, tc, ch)[:, :, :b].transpose(2, 1, 3, 0, 4)
    y_ctx = y[:, :nkc].reshape(lay.n_ctx, D_MODEL)
    y_lat = y[:, nkc:].reshape(lay.n_lat, D_MODEL)
    return jnp.concatenate([y_lat, y_ctx], axis=0)


def _s5_out_body(y_ref, u_ref, d_ref, wv_ref, wg_ref, *rest):
    u = u_ref[...].astype(F32)
    act = _gelu_tanh(y_ref[...].astype(F32) + d_ref[...] * u).astype(BF16)
    val = jnp.dot(act, wv_ref[...], preferred_element_type=F32)
    gate = jnp.dot(act, wg_ref[...], preferred_element_type=F32)
    _finish_route(val * jax.nn.sigmoid(gate), *rest)


def _s5_out(lay, n_tiles, layer, y, u, d, wv, wg, h, mods, lng, lnb, wr, br):
    shapes, specs = _route_outputs(n_tiles * TOK_TILE)
    return pl.pallas_call(
        _s5_out_body,
        grid=(n_tiles,),
        in_specs=[_row_spec(), _row_spec(), _const_spec((1, D_MODEL)), _const_spec((D_MODEL, D_MODEL)),
                  _const_spec((D_MODEL, D_MODEL))] + _route_inputs(lay, layer),
        out_specs=specs,
        out_shape=shapes,
        scratch_shapes=[pltpu.VMEM((1, ROUTE_LANES), F32)],
        compiler_params=_cparams("arbitrary"),
        name="s5_out_route",
    )(y, u, d, wv, wg, h, mods, lng, lnb, wr, br)


def kernel(x, c, ctx, c_ctx, ada_w, ada_b, ln_g, ln_b, s5_lam_re, s5_lam_im, s5_log_dt, s5_b_re, s5_b_im, s5_c_re, s5_c_im, s5_d, s5_w_glu, cv_w_pw1, cv_b_pw1, cv_w_dw, cv_b_dw, cv_ln_g, cv_ln_b, cv_w_pw2, cv_b_pw2, at_w_qkv, at_w_o, at_sink, moe_wg, moe_bg, moe_we, moe_be, moe_w1, moe_w2):
    batch, seq, d = x.shape
    lay = _Layout(batch, seq, ctx.shape[1])
    depth = ada_w.shape[0]
    row = lambda a: a.reshape(1, -1)

    h = jnp.concatenate([x.reshape(lay.n_lat, d), ctx.reshape(lay.n_ctx, d)], axis=0)
    c_all = jnp.concatenate([c, c_ctx[None], jnp.zeros((lay.mod_rows - batch - 1, d), F32)], axis=0)
    mods = _modulation(c_all, ada_w, ada_b)
    cos_t, sin_t = _rope_tables(seq)
    ng, ne = N_EXPERT_GROUPS, N_EXPERTS
    pad = jnp.zeros((d, ROUTE_LANES - ng - ne), F32)

    u = _modulate(lay, h, mods, 0)
    for i in range(depth):
        last = i == depth - 1
        kind, j = i % 3, i // 3
        n_tiles = lay.lat_tiles if last else lay.tiles
        wr = jnp.concatenate([moe_wg[i], moe_we[i], pad], axis=1)
        br = jnp.concatenate([moe_bg[i], moe_be[i], pad[0]], axis=0)[None]
        route_args = (h, mods, row(ln_g[i, 0]), row(ln_b[i, 0]), wr, br)
        if kind == 0:
            mats = _s5_prepare(s5_lam_re[j], s5_lam_im[j], s5_log_dt[j], s5_b_re[j], s5_b_im[j],
                               s5_c_re[j], s5_c_im[j])
            y = _s5_core(lay, u, mats)
            wglu = s5_w_glu[j].astype(BF16)
            h1, v, meta, cnt = _s5_out(lay, n_tiles, i, y, u, row(s5_d[j]), wglu[:, :d], wglu[:, d:], *route_args)
        elif kind == 1:
            w1 = cv_w_pw1[j].astype(BF16)
            z = _conv_pw1(lay, u, w1[:, :d], w1[:, d:], row(cv_b_pw1[j, :d]), row(cv_b_pw1[j, d:]))
            h1, v, meta, cnt = _conv_out(lay, n_tiles, i, z, cv_w_dw[j], row(cv_b_dw[j]), row(cv_ln_g[j]),
                                         row(cv_ln_b[j]), cv_w_pw2[j].astype(BF16), row(cv_b_pw2[j]),
                                         *route_args)
        else:
            q, k, vv = _qkv_rope(lay, u, at_w_qkv[j].astype(BF16), cos_t, sin_t)
            o = _attention(lay, q, k, vv, row(at_sink[j]), not last)
            h1, v, meta, cnt = _attn_out(lay, n_tiles, i, o, at_w_o[j].astype(BF16), *route_args)
        pos_tiles, items, n_assign = _moe_plan(meta, cnt, n_tiles)
        xs = _moe_dispatch(v, pos_tiles, n_assign)
        ys = _moe_experts(xs, moe_w1[i], moe_w2[i], items)
        h, u = _moe_combine(lay, n_tiles, ys, pos_tiles, meta, h1, mods, i, row(ln_g[i, 1]), row(ln_b[i, 1]),
                            not last)
    return h.reshape(batch, seq, d)
```

```python
import functools
import math

import jax
import jax.numpy as jnp
from jax import lax
from jax.experimental import pallas as pl
from jax.experimental.pallas import tpu as pltpu

F32 = jnp.float32
BF16 = jnp.bfloat16
HIGHEST = lax.Precision.HIGHEST

D_MODEL = 1024
DEPTH = 4
GRID_W = 64
SSM_CH_PER_GROUP = 16
SSM_GROUPS = D_MODEL // SSM_CH_PER_GROUP
SSM_STATE = 64
CONV_WIDTH = 31
HEAD_DIM = 64
N_HEADS = D_MODEL // HEAD_DIM
N_KV_HEADS = N_HEADS // 4
KV_REP = N_HEADS // N_KV_HEADS
WINDOW = 128
ATT_BLOCK = 128
ROPE_BASE = 10000.0
N_EXPERT_GROUPS = 4
EXPERTS_PER_GROUP = 8
N_EXPERTS = N_EXPERT_GROUPS * EXPERTS_PER_GROUP
D_EXPERT = D_MODEL // 2
ALPHA = (2 * DEPTH) ** 0.25
LN_EPS = 1e-5

SUBLANES = 8
LANES = 128

TOK_TILE = 256
EXPERT_TILE = 512
S5_CHUNK = 32
CONV_HALO = 16
ROUTE_LANES = LANES
META_COLS = 8
RUN_PAD = SUBLANES
LOCAL_ROWS = -(-(2 * TOK_TILE + N_EXPERTS * (RUN_PAD - 1)) // LANES) * LANES
MAX_PIECES = LOCAL_ROWS // RUN_PAD
VMEM_LIMIT = 56 * 1024 * 1024


def _cparams(*sem):
    return pltpu.CompilerParams(dimension_semantics=sem, vmem_limit_bytes=VMEM_LIMIT)


def _round_up(n, m):
    return (n + m - 1) // m * m


def _split_bf16(w):
    hi = w.astype(BF16)
    return jnp.stack([hi, (w - hi.astype(F32)).astype(BF16)])


def _mod_body(c_ref, w_ref, b_ref, o_ref):
    c = c_ref[...]
    s = c * jax.nn.sigmoid(c)
    o_ref[0] = jnp.dot(s, w_ref[0], preferred_element_type=F32, precision=HIGHEST) + b_ref[0]


def _modulation(c_all, ada_w, ada_b):
    depth, d, n = ada_w.shape
    r = c_all.shape[0]
    tn = 1024
    out = pl.pallas_call(
        _mod_body,
        grid=(depth, n // tn),
        in_specs=[
            pl.BlockSpec((r, d), lambda i, j: (0, 0)),
            pl.BlockSpec((1, d, tn), lambda i, j: (i, 0, j)),
            pl.BlockSpec((1, 1, tn), lambda i, j: (i, 0, j)),
        ],
        out_specs=pl.BlockSpec((1, r, tn), lambda i, j: (i, 0, j)),
        out_shape=jax.ShapeDtypeStruct((depth, r, n), F32),
        compiler_params=_cparams("arbitrary", "arbitrary"),
        name="adaln_modulation",
    )(c_all, ada_w, ada_b.reshape(depth, 1, n))
    return out.reshape(depth, r, 6, d)


def _layer_norm_rows(t, g, b):
    mu = jnp.mean(t, axis=-1, keepdims=True)
    dev = t - mu
    var = jnp.mean(dev * dev, axis=-1, keepdims=True)
    return dev * lax.rsqrt(var + LN_EPS) * g + b


def _post_norm_and_route(h, y, mod, lng, lnb, wr, br):
    tm = h.shape[0]
    ng, ne = N_EXPERT_GROUPS, N_EXPERTS
    h1 = _layer_norm_rows(ALPHA * h + mod[2:3] * y, lng, lnb)
    v = h1 * (1.0 + mod[4:5]) + mod[3:4]
    v_hi = v.astype(BF16)
    v_lo = (v - v_hi.astype(F32)).astype(BF16)
    logits = (jnp.dot(v_hi, wr[0], preferred_element_type=F32) + jnp.dot(v_lo, wr[0], preferred_element_type=F32)
              + jnp.dot(v_hi, wr[1], preferred_element_type=F32) + br)

    lane = lax.broadcasted_iota(jnp.int32, (tm, ROUTE_LANES), 1)
    lane_f = lane.astype(F32)
    neg = -jnp.inf
    no_lane = float(ROUTE_LANES)
    is_group = lane < ng
    gl = jnp.where(is_group, logits, neg)
    gmax = jnp.max(gl, axis=-1, keepdims=True)
    gsum = jnp.sum(jnp.where(is_group, jnp.exp(logits - gmax), 0.0), axis=-1, keepdims=True)
    p_group = 1.0 / gsum
    g_idx = jnp.min(jnp.where(gl == gmax, lane_f, no_lane), axis=-1, keepdims=True)
    expert_group = ((lane - ng) // EXPERTS_PER_GROUP).astype(F32)
    in_group = (lane >= ng) & (lane < ng + ne) & (expert_group == g_idx)
    el = jnp.where(in_group, logits, neg)
    v1 = jnp.max(el, axis=-1, keepdims=True)
    i1 = jnp.min(jnp.where(el == v1, lane_f, no_lane), axis=-1, keepdims=True)
    el2 = jnp.where(lane_f == i1, neg, el)
    v2 = jnp.max(el2, axis=-1, keepdims=True)
    i2 = jnp.min(jnp.where(el2 == v2, lane_f, no_lane), axis=-1, keepdims=True)
    e21 = jnp.exp(v2 - v1)
    w1 = p_group / (1.0 + e21)
    w2 = p_group * e21 / (1.0 + e21)

    hit1 = lane_f == i1
    hit2 = lane_f == i2
    one1 = hit1.astype(BF16)
    one2 = hit2.astype(BF16)
    rr = lax.broadcasted_iota(jnp.int32, (tm, tm), 0)
    cc = lax.broadcasted_iota(jnp.int32, (tm, tm), 1)
    before = (cc < rr).astype(BF16)
    cum1 = jnp.dot(before, one1, preferred_element_type=F32)
    cum2 = jnp.dot(before, one2, preferred_element_type=F32)
    tot1 = jnp.sum(one1.astype(F32), axis=0, keepdims=True)
    tot2 = jnp.sum(one2.astype(F32), axis=0, keepdims=True)
    rank1 = jnp.sum(jnp.where(hit1, cum1, 0.0), axis=-1, keepdims=True)
    rank2 = jnp.sum(jnp.where(hit2, tot1 + cum2, 0.0), axis=-1, keepdims=True)

    col = lax.broadcasted_iota(jnp.int32, (tm, META_COLS), 1)
    meta = jnp.where(col == 0, i1 - ng,
           jnp.where(col == 1, i2 - ng,
           jnp.where(col == 2, rank1,
           jnp.where(col == 3, rank2,
           jnp.where(col == 4, w1,
           jnp.where(col == 5, w2, 0.0))))))
    return h1, v, meta, tot1 + tot2


def _gelu_tanh(x):
    return 0.5 * x * (1.0 + jnp.tanh(math.sqrt(2.0 / math.pi) * (x + 0.044715 * (x * x * x))))


class _Layout:
    def __init__(self, batch, seq, ctx_len):
        self.batch, self.seq, self.ctx_len = batch, seq, ctx_len
        self.n_lat = batch * seq
        self.n_ctx = batch * ctx_len
        self.n_tok = self.n_lat + self.n_ctx
        assert seq % TOK_TILE == 0 and ctx_len % TOK_TILE == 0
        self.lat_tiles = self.n_lat // TOK_TILE
        self.tiles = self.n_tok // TOK_TILE
        self.tiles_per_seq = seq // TOK_TILE
        self.tiles_per_ctx = ctx_len // TOK_TILE
        self.mod_rows = _round_up(batch + 1, SUBLANES)

    def mod_row(self, i):
        return jnp.minimum(i // self.tiles_per_seq, self.batch)


def _mod_spec(lay, layer):
    return pl.BlockSpec((1, 1, 6, D_MODEL), lambda i, *_: (layer, lay.mod_row(i), 0, 0))


def _row_spec(tm=TOK_TILE, d=D_MODEL):
    return pl.BlockSpec((tm, d), lambda i, *_: (i, 0))


def _const_spec(shape):
    nd = len(shape)
    return pl.BlockSpec(shape, lambda i, *_: (0,) * nd)


def _modulate_body(h_ref, mod_ref, u_ref):
    mod = mod_ref[0, 0]
    u_ref[...] = (h_ref[...] * (1.0 + mod[1:2]) + mod[0:1]).astype(u_ref.dtype)


def _modulate(lay, h, mods, layer):
    return pl.pallas_call(
        _modulate_body,
        grid=(lay.tiles,),
        in_specs=[_row_spec(), _mod_spec(lay, layer)],
        out_specs=_row_spec(),
        out_shape=jax.ShapeDtypeStruct((lay.n_tok, D_MODEL), BF16),
        compiler_params=_cparams("arbitrary"),
        name="input_modulate",
    )(h, mods)


def _piece_spec(n_tiles, shift=0):
    return pl.BlockSpec((1, 1, MAX_PIECES), lambda i, *_: (jnp.minimum(i + shift, n_tiles - 1), 0, 0),
                        memory_space=pltpu.SMEM)


def _dispatch_body(np_ref, tail_ref, dst_ref, lp_ref, v_ref, xs_ref, loc, zeros, sem):
    i = pl.program_id(0)
    lp = lp_ref[0]
    slot_row = lax.broadcasted_iota(jnp.int32, (LOCAL_ROWS, v_ref.shape[0]), 0).astype(F32)
    pick = ((slot_row == lp[0:1]) | (slot_row == lp[1:2])).astype(BF16)
    loc[...] = jnp.dot(pick, v_ref[...], preferred_element_type=F32)

    def piece(src, dst):
        return pltpu.make_async_copy(src, xs_ref.at[pl.ds(pl.multiple_of(dst, RUN_PAD), RUN_PAD)], sem)

    def issue(q, carry):
        piece(loc.at[pl.ds(pl.multiple_of(q * RUN_PAD, RUN_PAD), RUN_PAD)], dst_ref[0, 0, q]).start()
        return carry

    def drain(q, carry):
        piece(loc.at[pl.ds(0, RUN_PAD)], 0).wait()
        return carry

    n = np_ref[i]
    lax.fori_loop(0, n, issue, 0)
    lax.fori_loop(0, n, drain, 0)

    @pl.when(i == pl.num_programs(0) - 1)
    def _():
        zeros[...] = jnp.zeros_like(zeros)

        def issue_zero(q, carry):
            piece(zeros, tail_ref[0] + q * RUN_PAD).start()
            return carry

        def drain_zero(q, carry):
            piece(zeros, 0).wait()
            return carry

        lax.fori_loop(0, tail_ref[1], issue_zero, 0)
        lax.fori_loop(0, tail_ref[1], drain_zero, 0)


def _moe_dispatch(v, plan):
    n_tiles = plan.n_pieces.shape[0]
    grid_spec = pltpu.PrefetchScalarGridSpec(
        num_scalar_prefetch=2,
        grid=(n_tiles,),
        in_specs=[_piece_spec(n_tiles),
                  pl.BlockSpec((1, 2, TOK_TILE), lambda i, *_: (i, 0, 0)),
                  _row_spec()],
        out_specs=pl.BlockSpec(memory_space=pl.ANY),
        scratch_shapes=[pltpu.VMEM((LOCAL_ROWS, D_MODEL), F32), pltpu.VMEM((RUN_PAD, D_MODEL), F32),
                        pltpu.SemaphoreType.DMA(())],
    )
    return pl.pallas_call(
        _dispatch_body,
        grid_spec=grid_spec,
        out_shape=jax.ShapeDtypeStruct((plan.max_rows, D_MODEL), F32),
        compiler_params=_cparams("arbitrary"),
        name="moe_dispatch",
    )(plan.n_pieces, plan.tail, plan.dst, plan.lp_rows, v)


def _expert_body(tile_ref, exp_ref, lo_ref, hi_ref, xs_ref, w1_ref, w2_ref, ys_ref, w1b, w2b):
    j = pl.program_id(0)
    jp = jnp.maximum(j - 1, 0)
    new_expert = (j == 0) | (exp_ref[j] != exp_ref[jp])
    first_visit = (j == 0) | (tile_ref[j] != tile_ref[jp])
    lo, hi = lo_ref[j], hi_ref[j]

    @pl.when(new_expert)
    def _():
        w1b[...] = w1_ref[0].astype(BF16)
        w2b[...] = w2_ref[0].astype(BF16)

    def expert_rows():
        x = xs_ref[...].astype(BF16)
        gu = jnp.dot(x, w1b[...], preferred_element_type=F32)
        gate, up = gu[:, :D_EXPERT], gu[:, D_EXPERT:]
        a = (gate * jax.nn.sigmoid(gate) * up).astype(BF16)
        y = jnp.dot(a, w2b[...], preferred_element_type=F32)
        rows = lax.broadcasted_iota(jnp.int32, (xs_ref.shape[0], 1), 0)
        return y, (rows >= lo) & (rows < hi)

    @pl.when(first_visit)
    def _():
        y, mine = expert_rows()
        ys_ref[...] = jnp.where(mine, y, 0.0)

    @pl.when(jnp.logical_not(first_visit) & (hi > lo))
    def _():
        y, mine = expert_rows()
        ys_ref[...] = jnp.where(mine, y, ys_ref[...])


def _moe_experts(xs, w1, w2, items):
    tile_j, exp_j, lo_j, hi_j = items
    n_items = tile_j.shape[0]
    grid_spec = pltpu.PrefetchScalarGridSpec(
        num_scalar_prefetch=4,
        grid=(n_items,),
        in_specs=[
            pl.BlockSpec((EXPERT_TILE, D_MODEL), lambda j, t, e, lo, hi: (t[j], 0)),
            pl.BlockSpec((1, D_MODEL, 2 * D_EXPERT), lambda j, t, e, lo, hi: (e[j], 0, 0)),
            pl.BlockSpec((1, D_EXPERT, D_MODEL), lambda j, t, e, lo, hi: (e[j], 0, 0)),
        ],
        out_specs=pl.BlockSpec((EXPERT_TILE, D_MODEL), lambda j, t, e, lo, hi: (t[j], 0)),
        scratch_shapes=[pltpu.VMEM((D_MODEL, 2 * D_EXPERT), BF16), pltpu.VMEM((D_EXPERT, D_MODEL), BF16)],
    )
    return pl.pallas_call(
        _expert_body,
        grid_spec=grid_spec,
        out_shape=jax.ShapeDtypeStruct(xs.shape, F32),
        compiler_params=_cparams("arbitrary"),
        name="moe_experts",
    )(tile_j, exp_j, lo_j, hi_j, xs, w1, w2)


def _combine_body(has_next, np_ref, src_ref, srcn_ref, lpw_ref, h1_ref, mod_ref, modn_ref, lng_ref, lnb_ref,
                  ys_ref, *rest):
    if has_next:
        h2_ref, u_ref, buf, sem = rest
    else:
        h2_ref, buf, sem = rest
    tm = h1_ref.shape[0]
    i = pl.program_id(0)
    n = pl.num_programs(0)
    slot = i % 2

    def piece(src, s, q):
        return pltpu.make_async_copy(ys_ref.at[pl.ds(pl.multiple_of(src, RUN_PAD), RUN_PAD)],
                                     buf.at[s, pl.ds(pl.multiple_of(q * RUN_PAD, RUN_PAD), RUN_PAD)], sem.at[s])

    def gather(sref, s, count):
        def issue(q, carry):
            piece(sref[0, 0, q], s, q).start()
            return carry
        lax.fori_loop(0, count, issue, 0)

    @pl.when(i == 0)
    def _():
        buf[...] = jnp.zeros_like(buf)
        gather(src_ref, 0, np_ref[0])

    @pl.when(i + 1 < n)
    def _():
        gather(srcn_ref, 1 - slot, np_ref[jnp.minimum(i + 1, n - 1)])

    def drain(q, carry):
        piece(0, slot, 0).wait()
        return carry
    lax.fori_loop(0, np_ref[i], drain, 0)

    lpw = lpw_ref[...]
    lane = lax.broadcasted_iota(jnp.int32, (tm, LOCAL_ROWS), 1).astype(F32)
    wmat = jnp.where(lane == lpw[:, 0:1], lpw[:, 2:3], 0.0) + jnp.where(lane == lpw[:, 1:2], lpw[:, 3:4], 0.0)
    w_hi = wmat.astype(BF16)
    w_lo = (wmat - w_hi.astype(F32)).astype(BF16)
    yb = buf[slot].astype(BF16)
    f = jnp.dot(w_hi, yb, preferred_element_type=F32) + jnp.dot(w_lo, yb, preferred_element_type=F32)
    mod = mod_ref[0, 0]
    h2 = _layer_norm_rows(ALPHA * h1_ref[...] + mod[5:6] * f, lng_ref[...], lnb_ref[...])
    h2_ref[...] = h2
    if has_next:
        modn = modn_ref[0, 0]
        u_ref[...] = (h2 * (1.0 + modn[1:2]) + modn[0:1]).astype(u_ref.dtype)


def _moe_combine(lay, n_tiles, ys, plan, h1, mods, layer, lng, lnb, has_next):
    n_rows = n_tiles * TOK_TILE
    nxt = min(layer + 1, DEPTH - 1)
    out_shape = [jax.ShapeDtypeStruct((n_rows, D_MODEL), F32)]
    out_specs = [_row_spec()]
    if has_next:
        out_shape.append(jax.ShapeDtypeStruct((n_rows, D_MODEL), BF16))
        out_specs.append(_row_spec())
    grid_spec = pltpu.PrefetchScalarGridSpec(
        num_scalar_prefetch=1,
        grid=(n_tiles,),
        in_specs=[
            _piece_spec(n_tiles), _piece_spec(n_tiles, 1),
            _row_spec(d=4), _row_spec(), _mod_spec(lay, layer), _mod_spec(lay, nxt),
            _const_spec((1, D_MODEL)), _const_spec((1, D_MODEL)),
            pl.BlockSpec(memory_space=pl.ANY),
        ],
        out_specs=out_specs,
        scratch_shapes=[pltpu.VMEM((2, LOCAL_ROWS, D_MODEL), F32), pltpu.SemaphoreType.DMA((2,))],
    )
    outs = pl.pallas_call(
        functools.partial(_combine_body, has_next),
        grid_spec=grid_spec,
        out_shape=out_shape,
        compiler_params=_cparams("arbitrary"),
        name="moe_combine",
    )(plan.n_pieces, plan.dst, plan.dst, plan.lp_w, h1, mods, mods, lng, lnb, ys)
    return outs if has_next else (outs[0], None)


class _MoePlan:
    pass


def _exclusive_cumsum(a, axis):
    return jnp.cumsum(a, axis=axis) - a


def _moe_plan(meta, tile_counts, n_tiles):
    plan = _MoePlan()
    experts = jnp.arange(N_EXPERTS, dtype=jnp.int32)
    plan.max_rows = _round_up(n_tiles * (2 * TOK_TILE + N_EXPERTS * (RUN_PAD - 1)), EXPERT_TILE)
    cnt = tile_counts[:, 0, N_EXPERT_GROUPS:N_EXPERT_GROUPS + N_EXPERTS].astype(jnp.int32)
    run = (cnt + RUN_PAD - 1) // RUN_PAD * RUN_PAD
    local_start = _exclusive_cumsum(run, 1)
    counts = jnp.sum(run, axis=0)
    total = jnp.sum(counts)
    tail = (-total) % EXPERT_TILE
    counts = counts.at[N_EXPERTS - 1].add(tail)
    ends = jnp.cumsum(counts)
    starts = ends - counts
    run_start = starts[None, :] + _exclusive_cumsum(run, 0)
    plan.tail = jnp.stack([total, tail // RUN_PAD]).astype(jnp.int32)

    eid = meta[:, 0:2].astype(jnp.int32).reshape(n_tiles, TOK_TILE, 2)
    onehot = eid[..., None] == experts
    lp = meta[:, 2:4].reshape(n_tiles, TOK_TILE, 2) + jnp.sum(
        jnp.where(onehot, local_start[:, None, None, :], 0), axis=-1).astype(F32)
    plan.lp_rows = lp.transpose(0, 2, 1)
    plan.lp_w = jnp.concatenate([lp.reshape(-1, 2), meta[:, 4:6]], axis=1)

    pieces = run // RUN_PAD
    piece_end = jnp.cumsum(pieces, axis=1)
    plan.n_pieces = piece_end[:, -1].astype(jnp.int32)
    q = jnp.arange(MAX_PIECES, dtype=jnp.int32)
    owner = jnp.minimum(jnp.sum(q[None, :, None] >= piece_end[:, None, :], axis=-1), N_EXPERTS - 1)
    pick = owner[..., None] == experts
    first = jnp.sum(jnp.where(pick, (piece_end - pieces)[:, None, :], 0), axis=-1)
    base = jnp.sum(jnp.where(pick, run_start[:, None, :], 0), axis=-1)
    dst = jnp.where(q[None, :] < plan.n_pieces[:, None], base + (q[None, :] - first) * RUN_PAD, 0)
    plan.dst = dst.astype(jnp.int32).reshape(n_tiles, 1, MAX_PIECES)

    n_etiles = plan.max_rows // EXPERT_TILE
    first_tile = starts // EXPERT_TILE
    last_tile = jnp.maximum(ends - 1, 0) // EXPERT_TILE
    n_items_e = jnp.where(counts > 0, last_tile - first_tile + 1, 0)
    item_end = jnp.cumsum(n_items_e)
    item_start = item_end - n_items_e
    n_items = n_etiles + N_EXPERTS
    j = jnp.arange(n_items, dtype=jnp.int32)
    e_j = jnp.minimum(jnp.sum(j[:, None] >= item_end[None, :], axis=1), N_EXPERTS - 1).astype(jnp.int32)
    active = j < item_end[-1]
    tile_j = jnp.take(first_tile, e_j) + (j - jnp.take(item_start, e_j))
    lo = jnp.maximum(jnp.take(starts, e_j), tile_j * EXPERT_TILE) - tile_j * EXPERT_TILE
    hi = jnp.minimum(jnp.take(ends, e_j), (tile_j + 1) * EXPERT_TILE) - tile_j * EXPERT_TILE
    last = jnp.maximum(item_end[-1] - 1, 0)
    tile_j = jnp.where(active, tile_j, tile_j[last]).astype(jnp.int32)
    e_j = jnp.where(active, e_j, e_j[last]).astype(jnp.int32)
    lo = jnp.where(active, lo, 0).astype(jnp.int32)
    hi = jnp.where(active, hi, 0).astype(jnp.int32)
    plan.items = (tile_j, e_j, lo, hi)
    return plan


def _route_outputs(n_rows):
    n_tiles = n_rows // TOK_TILE
    shapes = [jax.ShapeDtypeStruct((n_rows, D_MODEL), F32),
              jax.ShapeDtypeStruct((n_rows, D_MODEL), BF16),
              jax.ShapeDtypeStruct((n_rows, META_COLS), F32),
              jax.ShapeDtypeStruct((n_tiles, 1, ROUTE_LANES), F32)]
    specs = [_row_spec(), _row_spec(), _row_spec(d=META_COLS),
             pl.BlockSpec((1, 1, ROUTE_LANES), lambda i: (i, 0, 0))]
    return shapes, specs


def _route_inputs(lay, layer):
    return [_row_spec(), _mod_spec(lay, layer), _const_spec((1, D_MODEL)), _const_spec((1, D_MODEL)),
            _const_spec((2, D_MODEL, ROUTE_LANES)), _const_spec((1, ROUTE_LANES))]


def _finish_route(y, h_ref, mod_ref, lng_ref, lnb_ref, wr_ref, br_ref, h1_ref, v_ref, meta_ref, cnt_ref):
    h1, v, meta, counts = _post_norm_and_route(h_ref[...], y, mod_ref[0, 0], lng_ref[...], lnb_ref[...],
                                               wr_ref[...], br_ref[...])
    h1_ref[...] = h1
    v_ref[...] = v.astype(v_ref.dtype)
    meta_ref[...] = meta
    cnt_ref[0] = counts


def _attn_out_body(o_ref, wo_ref, *rest):
    y = jnp.dot(o_ref[...], wo_ref[...], preferred_element_type=F32)
    _finish_route(y, *rest)


def _attn_out(lay, n_tiles, layer, o, w_o, h, mods, lng, lnb, wr, br):
    shapes, specs = _route_outputs(n_tiles * TOK_TILE)
    return pl.pallas_call(
        _attn_out_body,
        grid=(n_tiles,),
        in_specs=[_row_spec(), _const_spec((D_MODEL, D_MODEL))] + _route_inputs(lay, layer),
        out_specs=specs,
        out_shape=shapes,
        compiler_params=_cparams("arbitrary"),
        name="attn_out_route",
    )(o, w_o, h, mods, lng, lnb, wr, br)


def _rope_tables(seq):
    quarter = HEAD_DIM // 4
    inv_freq = ROPE_BASE ** (-jnp.arange(quarter, dtype=F32) / quarter)
    t = jnp.arange(seq)
    rows = (t // GRID_W).astype(F32)
    cols = (t % GRID_W).astype(F32)
    ang_r = rows[:, None] * inv_freq
    ang_c = cols[:, None] * inv_freq
    cos_h = jnp.concatenate([jnp.cos(ang_r), jnp.cos(ang_r), jnp.cos(ang_c), jnp.cos(ang_c)], axis=-1)
    sin_h = jnp.concatenate([-jnp.sin(ang_r), jnp.sin(ang_r), -jnp.sin(ang_c), jnp.sin(ang_c)], axis=-1)
    rep = LANES // HEAD_DIM
    return jnp.tile(cos_h, (1, rep)), jnp.tile(sin_h, (1, rep))


def _rope(x, cos, sin):
    quarter = HEAD_DIM // 4
    width = x.shape[1]
    n = width // LANES
    c = jnp.tile(cos, (1, n))
    s = jnp.tile(sin, (1, n))
    lane = lax.broadcasted_iota(jnp.int32, x.shape, 1)
    is_lo = (lane % (2 * quarter)) < quarter
    partner = jnp.where(is_lo, pltpu.roll(x, width - quarter, 1), pltpu.roll(x, quarter, 1))
    return x * c + partner * s


def _qkv_body(lat_tiles, u_ref, w_ref, cos_ref, sin_ref, q_ref, k_ref, v_ref):
    is_ctx = pl.program_id(0) >= lat_tiles
    qkv = jnp.dot(u_ref[...], w_ref[...], preferred_element_type=F32)
    nq, nkv = N_HEADS * HEAD_DIM, N_KV_HEADS * HEAD_DIM
    q, k, v = qkv[:, :nq], qkv[:, nq:nq + nkv], qkv[:, nq + nkv:]
    cos, sin = cos_ref[...], sin_ref[...]
    q = jnp.where(is_ctx, q, _rope(q, cos, sin))
    k = jnp.where(is_ctx, k, _rope(k, cos, sin))
    q_ref[...] = (q * HEAD_DIM ** -0.5).astype(q_ref.dtype)
    k_ref[...] = k.astype(k_ref.dtype)
    v_ref[...] = v.astype(v_ref.dtype)


def _qkv_rope(lay, u, w_qkv, cos_t, sin_t):
    nq, nkv = N_HEADS * HEAD_DIM, N_KV_HEADS * HEAD_DIM
    tab = pl.BlockSpec((TOK_TILE, LANES),
                       lambda i: (jnp.where(i < lay.lat_tiles, i % lay.tiles_per_seq, 0), 0))
    return pl.pallas_call(
        functools.partial(_qkv_body, lay.lat_tiles),
        grid=(lay.tiles,),
        in_specs=[_row_spec(), _const_spec((D_MODEL, nq + 2 * nkv)), tab, tab],
        out_specs=[_row_spec(), _row_spec(d=nkv), _row_spec(d=nkv)],
        out_shape=[jax.ShapeDtypeStruct((lay.n_tok, nq), BF16),
                   jax.ShapeDtypeStruct((lay.n_tok, nkv), BF16),
                   jax.ShapeDtypeStruct((lay.n_tok, nkv), BF16)],
        compiler_params=_cparams("arbitrary"),
        name="attn_qkv_rope",
    )(u, w_qkv, cos_t, sin_t)


def _attn_body(nqb, sink_ref, q_ref, kp_ref, kc_ref, kn_ref, kx_ref, vp_ref, vc_ref, vn_ref, vx_ref,
               o_ref, kcat, vcat):
    j = pl.program_id(1)
    blk = ATT_BLOCK
    n_win = 3 * blk
    for dst, parts in ((kcat, (kp_ref, kc_ref, kn_ref)), (vcat, (vp_ref, vc_ref, vn_ref))):
        for n, part in enumerate(parts):
            dst[n * blk:(n + 1) * blk] = part[...]
    kcat[n_win:] = kx_ref[...]
    vcat[n_win:] = vx_ref[...]
    r = lax.broadcasted_iota(jnp.int32, (blk, n_win), 0)
    c = lax.broadcasted_iota(jnp.int32, (blk, n_win), 1)
    valid = (jnp.abs(r + blk - c) <= WINDOW) & (j < nqb)
    valid = valid & ((c >= blk) | (j > 0)) & ((c < 2 * blk) | (j < nqb - 1))
    for hd in range(N_HEADS):
        g = hd // KV_REP
        head = slice(hd * HEAD_DIM, (hd + 1) * HEAD_DIM)
        kv_head = slice(g * HEAD_DIM, (g + 1) * HEAD_DIM)
        s = lax.dot_general(q_ref[:, head], kcat[:, kv_head], (((1,), (1,)), ((), ())),
                            preferred_element_type=F32)
        sink = sink_ref[0, hd]
        sw = jnp.where(valid, s[:, :n_win], -jnp.inf)
        sc = s[:, n_win:]
        m = jnp.maximum(jnp.max(sw, axis=-1, keepdims=True), jnp.max(sc, axis=-1, keepdims=True))
        m = jnp.maximum(m, sink)
        pw = jnp.exp(sw - m)
        pc = jnp.exp(sc - m)
        denom = (jnp.sum(pw, axis=-1, keepdims=True) + jnp.sum(pc, axis=-1, keepdims=True)
                 + jnp.exp(sink - m))
        acc = jnp.dot(pw.astype(BF16), vcat[:n_win, kv_head], preferred_element_type=F32)
        acc = acc + jnp.dot(pc.astype(BF16), vcat[n_win:, kv_head], preferred_element_type=F32)
        o_ref[:, head] = (acc / denom).astype(o_ref.dtype)


def _attention(lay, q, k, v, sink, with_ctx_out):
    nq, nkv = N_HEADS * HEAD_DIM, N_KV_HEADS * HEAD_DIM
    blk = ATT_BLOCK
    nqb = lay.seq // blk
    lc = lay.ctx_len
    assert lay.n_lat % lc == 0 and lay.seq % blk == 0 and lc % blk == 0
    ctx0 = lay.n_lat // lc
    ncb = lc // blk
    steps = nqb + (ncb if with_ctx_out else 0)
    q_blk = pl.BlockSpec(
        (blk, nq), lambda b, j: (jnp.where(j < nqb, b * nqb + j, (ctx0 + b) * ncb + j - nqb), 0))
    kv_blk = lambda shift: pl.BlockSpec(
        (blk, nkv), lambda b, j: (b * nqb + jnp.clip(j + shift, 0, nqb - 1), 0))
    kv_ctx = pl.BlockSpec((lc, nkv), lambda b, j: (ctx0 + b, 0))
    return pl.pallas_call(
        functools.partial(_attn_body, nqb),
        grid=(lay.batch, steps),
        in_specs=[pl.BlockSpec(memory_space=pltpu.SMEM), q_blk,
                  kv_blk(-1), kv_blk(0), kv_blk(1), kv_ctx,
                  kv_blk(-1), kv_blk(0), kv_blk(1), kv_ctx],
        out_specs=q_blk,
        out_shape=jax.ShapeDtypeStruct((lay.n_tok if with_ctx_out else lay.n_lat, nq), BF16),
        scratch_shapes=[pltpu.VMEM((3 * blk + lc, nkv), BF16), pltpu.VMEM((3 * blk + lc, nkv), BF16)],
        compiler_params=_cparams("arbitrary", "arbitrary"),
        name="attn_window",
    )(sink, q, k, k, k, k, v, v, v, v)


def _pw1_body(u_ref, wa_ref, wg_ref, ba_ref, bg_ref, z_ref):
    u = u_ref[...]
    a = jnp.dot(u, wa_ref[...], preferred_element_type=F32) + ba_ref[...]
    g = jnp.dot(u, wg_ref[...], preferred_element_type=F32) + bg_ref[...]
    z_ref[...] = (a * jax.nn.sigmoid(g)).astype(z_ref.dtype)


def _conv_pw1(lay, u, wa, wg, ba, bg):
    return pl.pallas_call(
        _pw1_body,
        grid=(lay.tiles,),
        in_specs=[_row_spec(), _const_spec((D_MODEL, D_MODEL)), _const_spec((D_MODEL, D_MODEL)),
                  _const_spec((1, D_MODEL)), _const_spec((1, D_MODEL))],
        out_specs=_row_spec(),
        out_shape=jax.ShapeDtypeStruct((lay.n_tok, D_MODEL), BF16),
        compiler_params=_cparams("arbitrary"),
        name="conv_pw1_glu",
    )(u, wa, wg, ba, bg)


def _conv_body(lat_tiles, tps, tpc, zp_ref, z_ref, zn_ref, wdw_ref, bdw_ref, cg_ref, cb_ref, w2_ref, b2_ref,
               h_ref, mod_ref, lng_ref, lnb_ref, wr_ref, br_ref, h1_ref, v_ref, meta_ref, cnt_ref, win):
    i = pl.program_id(0)
    tm = z_ref.shape[0]
    halo = CONV_HALO
    is_ctx = i >= lat_tiles
    pos = jnp.where(is_ctx, (i - lat_tiles) % tpc, i % tps)
    first = pos == 0
    last = pos == jnp.where(is_ctx, tpc, tps) - 1
    win[0:halo] = jnp.where(first, 0.0, zp_ref[...].astype(F32))
    win[halo:halo + tm] = z_ref[...].astype(F32)
    win[halo + tm:] = jnp.where(last, 0.0, zn_ref[...].astype(F32))
    off = halo - CONV_WIDTH // 2
    conv = jnp.zeros((tm, D_MODEL), F32)
    for tap in range(CONV_WIDTH):
        conv = conv + wdw_ref[tap:tap + 1, :] * win[off + tap:off + tap + tm, :]
    conv = conv + bdw_ref[...]
    nrm = _layer_norm_rows(conv, cg_ref[...], cb_ref[...])
    act = (nrm * jax.nn.sigmoid(nrm)).astype(BF16)
    y = jnp.dot(act, w2_ref[...], preferred_element_type=F32) + b2_ref[...]
    _finish_route(y, h_ref, mod_ref, lng_ref, lnb_ref, wr_ref, br_ref, h1_ref, v_ref, meta_ref, cnt_ref)


def _conv_out(lay, n_tiles, layer, z, w_dw, b_dw, cg, cb, w2, b2, h, mods, lng, lnb, wr, br):
    per_tile = TOK_TILE // CONV_HALO
    n_halo = lay.n_tok // CONV_HALO
    shapes, specs = _route_outputs(n_tiles * TOK_TILE)
    halo_spec = lambda f: pl.BlockSpec((CONV_HALO, D_MODEL), lambda i: (jnp.clip(f(i), 0, n_halo - 1), 0))
    return pl.pallas_call(
        functools.partial(_conv_body, lay.lat_tiles, lay.tiles_per_seq, lay.tiles_per_ctx),
        grid=(n_tiles,),
        in_specs=[halo_spec(lambda i: i * per_tile - 1), _row_spec(), halo_spec(lambda i: (i + 1) * per_tile),
                  _const_spec((CONV_WIDTH, D_MODEL)), _const_spec((1, D_MODEL)), _const_spec((1, D_MODEL)),
                  _const_spec((1, D_MODEL)), _const_spec((D_MODEL, D_MODEL)), _const_spec((1, D_MODEL))]
                 + _route_inputs(lay, layer),
        out_specs=specs,
        out_shape=shapes,
        scratch_shapes=[pltpu.VMEM((TOK_TILE + 2 * CONV_HALO, D_MODEL), F32)],
        compiler_params=_cparams("arbitrary"),
        name="conv_dw_out_route",
    )(z, z, z, w_dw, b_dw, cg, cb, w2, b2, h, mods, lng, lnb, wr, br)


def _s5_prep_body(lrow_ref, lcol_ref, bt_ref, ct_ref, mintra_ref, min_ref, mout_ref, arec_ref):
    tc, ch, ns = S5_CHUNK, SSM_CH_PER_GROUP, SSM_STATE
    width = tc * ch
    npow = LANES
    assert tc < npow
    lrow = lrow_ref[0]
    lcol = lcol_ref[0]
    sel = (lax.broadcasted_iota(jnp.int32, (ch, width), 0)
           == lax.broadcasted_iota(jnp.int32, (ch, width), 1) % ch).astype(F32)
    lane_t = lax.broadcasted_iota(jnp.int32, (npow, width), 1) // ch
    pow_row = lax.broadcasted_iota(jnp.int32, (npow, width), 0)
    row_t = lax.broadcasted_iota(jnp.int32, (width, npow), 0) // ch
    pow_lane = lax.broadcasted_iota(jnp.int32, (width, npow), 1)
    n_lane = lax.broadcasted_iota(jnp.int32, (1, npow), 1).astype(F32)
    n_row = lax.broadcasted_iota(jnp.int32, (npow, 1), 0).astype(F32)
    exact_dot = functools.partial(jnp.dot, preferred_element_type=F32, precision=HIGHEST)
    krows, a_re, a_im = [], [], []
    for d in range(2):
        lr2 = jnp.minimum(lrow[3 * d:3 * d + 1], -1e-4)
        li2 = lrow[3 * d + 1:3 * d + 2]
        dt2 = jnp.exp(lrow[3 * d + 2:3 * d + 3])
        lr, li, dt = lr2[:, :ns], li2[:, :ns], dt2[:, :ns]
        lrc = jnp.minimum(lcol[:, 3 * d:3 * d + 1], -1e-4)
        lic = lcol[:, 3 * d + 1:3 * d + 2]
        dtc = jnp.exp(lcol[:, 3 * d + 2:3 * d + 3])
        er = jnp.exp(lr * dt)
        xr = er * jnp.cos(li * dt) - 1.0
        xi = er * jnp.sin(li * dt)
        den = lr * lr + li * li
        qr = (xr * lr + xi * li) / den
        qi = (xi * lr - xr * li) / den
        btr, bti = bt_ref[0, 2 * d], bt_ref[0, 2 * d + 1]
        bbr = qr * btr - qi * bti
        bbi = qr * bti + qi * btr
        mg = jnp.exp(n_lane * (lrc * dtc))
        an = n_lane * (lic * dtc)
        pw_re, pw_im = mg * jnp.cos(an), mg * jnp.sin(an)
        mg_t = jnp.exp(n_row * (lr * dt))
        an_t = n_row * (li * dt)
        pt_re, pt_im = mg_t * jnp.cos(an_t), mg_t * jnp.sin(an_t)
        n_in = (tc - 1 - row_t) if d == 0 else row_t
        pick = (pow_lane == n_in).astype(F32)
        pr, pi = exact_dot(pick, pt_re), exact_dot(pick, pt_im)
        bbr_t, bbi_t = jnp.tile(bbr, (tc, 1)), jnp.tile(bbi, (tc, 1))
        re0, im0 = ns * d, 2 * ns + ns * d
        min_ref[0, :, re0:re0 + ns] = (pr * bbr_t - pi * bbi_t).astype(BF16)
        min_ref[0, :, im0:im0 + ns] = (pr * bbi_t + pi * bbr_t).astype(BF16)
        ctr = exact_dot(ct_ref[0, 2 * d], sel)
        cti = exact_dot(ct_ref[0, 2 * d + 1], sel)

        def readout(n_of_lane):
            spread = (pow_row == n_of_lane).astype(F32)
            wr_, wi_ = exact_dot(pw_re, spread), exact_dot(pw_im, spread)
            return ctr * wr_ - cti * wi_, -(ctr * wi_ + cti * wr_)

        o_re, o_imneg = readout(lane_t + 1 if d == 0 else tc - lane_t)
        mout_ref[0, re0:re0 + ns, :] = o_re.astype(BF16)
        mout_ref[0, im0:im0 + ns, :] = o_imneg.astype(BF16)
        k_re, k_imneg = readout(lane_t if d == 0 else tc - 1 - lane_t)
        krows.append(exact_dot(bbr, k_re) + exact_dot(bbi, k_imneg))
        e2 = jnp.exp(tc * lr2 * dt2)
        a_re.append(e2 * jnp.cos(tc * li2 * dt2))
        a_im.append(e2 * jnp.sin(tc * li2 * dt2))
    is_fwd = lax.broadcasted_iota(jnp.int32, (1, 2 * ns), 1) < ns
    arec_ref[0, 0:1] = jnp.where(is_fwd, a_re[0], a_re[1])
    arec_ref[0, 1:2] = jnp.where(is_fwd, a_im[0], a_im[1])
    lane = lax.broadcasted_iota(jnp.int32, (ch, width), 1)
    for s in range(tc):
        fwd = krows[0] if s == 0 else jnp.where(lane >= ch * s, pltpu.roll(krows[0], ch * s, 1), 0.0)
        back = ch * (tc - 1 - s)
        rev = krows[1] if back == 0 else jnp.where(lane < ch * (s + 1), pltpu.roll(krows[1], width - back, 1), 0.0)
        mintra_ref[0, ch * s:ch * (s + 1), :] = (fwd + rev).astype(BF16)


def _s5_prepare(lam_re, lam_im, log_dt, b_re, b_im, c_re, c_im):
    g, ns, ch, tc = SSM_GROUPS, SSM_STATE, SSM_CH_PER_GROUP, S5_CHUNK
    width = tc * ch
    ldt = jnp.broadcast_to(log_dt[:, :, None], lam_re.shape)
    stack = jnp.stack([lam_re[0], lam_im[0], ldt[0], lam_re[1], lam_im[1], ldt[1]], axis=1)
    lrow = jnp.concatenate([stack, stack], axis=-1)
    lcol = jnp.swapaxes(stack, 1, 2)
    bt = jnp.stack([b_re[0], b_im[0], b_re[1], b_im[1]], axis=1).swapaxes(2, 3)
    ct = jnp.stack([c_re[0], c_im[0], c_re[1], c_im[1]], axis=1).swapaxes(2, 3)
    blk = lambda *s: pl.BlockSpec((1,) + s, lambda i: (i,) + (0,) * len(s))
    return pl.pallas_call(
        _s5_prep_body,
        grid=(g,),
        in_specs=[blk(6, 2 * ns), blk(ns, 6), blk(4, ch, ns), blk(4, ns, ch)],
        out_specs=[blk(width, width), blk(width, 4 * ns), blk(4 * ns, width), blk(2, 2 * ns)],
        out_shape=[jax.ShapeDtypeStruct((g, width, width), BF16),
                   jax.ShapeDtypeStruct((g, width, 4 * ns), BF16),
                   jax.ShapeDtypeStruct((g, 4 * ns, width), BF16),
                   jax.ShapeDtypeStruct((g, 2, 2 * ns), F32)],
        compiler_params=_cparams("arbitrary"),
        name="s5_prepare",
    )(lrow, lcol, bt, ct)


def _s5_core_body(nb, nkc, v_ref, mintra_ref, min_ref, mout_ref, arec_ref, y_ref, s_ref, h_ref):
    ns2 = 2 * SSM_STATE
    v = v_ref[0]
    s_ref[...] = jnp.dot(v, min_ref[0], preferred_element_type=F32)
    nk = v.shape[0] // nb
    ns = SSM_STATE
    arec = arec_ref[0]
    a_re, a_im = arec[0:1], arec[1:2]
    is_fwd = lax.broadcasted_iota(jnp.int32, (nb, ns2), 1) < ns

    def step(i, carry):
        h_re, h_im = carry
        kr = jnp.where(i < nkc, nkc - 1 - i, nk - 1 - (i - nkc))
        of = pl.multiple_of(i * nb, nb)
        orv = pl.multiple_of(kr * nb, nb)
        h_ref[pl.ds(of, nb), 0:ns] = h_re[:, :ns]
        h_ref[pl.ds(orv, nb), ns:ns2] = h_re[:, ns:]
        h_ref[pl.ds(of, nb), ns2:ns2 + ns] = h_im[:, :ns]
        h_ref[pl.ds(orv, nb), ns2 + ns:2 * ns2] = h_im[:, ns:]
        s_re = jnp.where(is_fwd, s_ref[pl.ds(of, nb), 0:ns2], s_ref[pl.ds(orv, nb), 0:ns2])
        s_im = jnp.where(is_fwd, s_ref[pl.ds(of, nb), ns2:2 * ns2], s_ref[pl.ds(orv, nb), ns2:2 * ns2])
        return a_re * h_re - a_im * h_im + s_re, a_re * h_im + a_im * h_re + s_im

    zero = jnp.zeros((nb, ns2), F32)
    lax.fori_loop(0, nk, step, (zero, zero))
    y = jnp.dot(v, mintra_ref[0], preferred_element_type=F32)
    y = y + jnp.dot(h_ref[...].astype(BF16), mout_ref[0], preferred_element_type=F32)
    y_ref[0] = y.astype(y_ref.dtype)


def _s5_core(lay, u, mats):
    mintra, m_in, m_out, arec = mats
    g, ch, tc = SSM_GROUPS, SSM_CH_PER_GROUP, S5_CHUNK
    width = tc * ch
    b = lay.batch
    nb = _round_up(b, SUBLANES)
    nkl, nkc = lay.seq // tc, lay.ctx_len // tc
    nk = nkl + nkc
    lat = u[:lay.n_lat].reshape(b, nkl, tc, g, ch)
    ctx = u[lay.n_lat:].reshape(b, nkc, tc, g, ch)
    x = jnp.concatenate([ctx, lat], axis=1).transpose(3, 1, 0, 2, 4)
    x = jnp.pad(x, ((0, 0), (0, 0), (0, nb - b), (0, 0), (0, 0))).reshape(g, nk * nb, width)
    rows = nk * nb
    blk = lambda *s: pl.BlockSpec((1,) + s, lambda i: (i,) + (0,) * len(s))
    y = pl.pallas_call(
        functools.partial(_s5_core_body, nb, nkc),
        grid=(g,),
        in_specs=[blk(rows, width), blk(width, width), blk(width, 4 * SSM_STATE), blk(4 * SSM_STATE, width),
                  blk(2, 2 * SSM_STATE)],
        out_specs=blk(rows, width),
        out_shape=jax.ShapeDtypeStruct((g, rows, width), BF16),
        scratch_shapes=[pltpu.VMEM((rows, 4 * SSM_STATE), F32), pltpu.VMEM((rows, 4 * SSM_STATE), F32)],
        compiler_params=_cparams("arbitrary"),
        name="s5_scan",
    )(x, mintra, m_in, m_out, arec)
    y = y.reshape(g, nk, nb, tc, ch)[:, :, :b].transpose(2, 1, 3, 0, 4)
    y_ctx = y[:, :nkc].reshape(lay.n_ctx, D_MODEL)
    y_lat = y[:, nkc:].reshape(lay.n_lat, D_MODEL)
    return jnp.concatenate([y_lat, y_ctx], axis=0)


def _s5_out_body(y_ref, u_ref, d_ref, wv_ref, wg_ref, *rest):
    u = u_ref[...].astype(F32)
    act = _gelu_tanh(y_ref[...].astype(F32) + d_ref[...] * u).astype(BF16)
    val = jnp.dot(act, wv_ref[...], preferred_element_type=F32)
    gate = jnp.dot(act, wg_ref[...], preferred_element_type=F32)
    _finish_route(val * jax.nn.sigmoid(gate), *rest)


def _s5_out(lay, n_tiles, layer, y, u, d, wv, wg, h, mods, lng, lnb, wr, br):
    shapes, specs = _route_outputs(n_tiles * TOK_TILE)
    return pl.pallas_call(
        _s5_out_body,
        grid=(n_tiles,),
        in_specs=[_row_spec(), _row_spec(), _const_spec((1, D_MODEL)), _const_spec((D_MODEL, D_MODEL)),
                  _const_spec((D_MODEL, D_MODEL))] + _route_inputs(lay, layer),
        out_specs=specs,
        out_shape=shapes,
        compiler_params=_cparams("arbitrary"),
        name="s5_out_route",
    )(y, u, d, wv, wg, h, mods, lng, lnb, wr, br)


def kernel(x, c, ctx, c_ctx, ada_w, ada_b, ln_g, ln_b, s5_lam_re, s5_lam_im, s5_log_dt, s5_b_re, s5_b_im, s5_c_re, s5_c_im, s5_d, s5_w_glu, cv_w_pw1, cv_b_pw1, cv_w_dw, cv_b_dw, cv_ln_g, cv_ln_b, cv_w_pw2, cv_b_pw2, at_w_qkv, at_w_o, at_sink, moe_wg, moe_bg, moe_we, moe_be, moe_w1, moe_w2):
    batch, seq, d = x.shape
    lay = _Layout(batch, seq, ctx.shape[1])
    depth = ada_w.shape[0]
    row = lambda a: a.reshape(1, -1)

    h = jnp.concatenate([x.reshape(lay.n_lat, d), ctx.reshape(lay.n_ctx, d)], axis=0)
    c_all = jnp.concatenate([c, c_ctx[None], jnp.zeros((lay.mod_rows - batch - 1, d), F32)], axis=0)
    mods = _modulation(c_all, ada_w, ada_b)
    cos_t, sin_t = _rope_tables(seq)
    ng, ne = N_EXPERT_GROUPS, N_EXPERTS
    pad = jnp.zeros((d, ROUTE_LANES - ng - ne), F32)

    u = _modulate(lay, h, mods, 0)
    for i in range(depth):
        last = i == depth - 1
        kind, j = i % 3, i // 3
        n_tiles = lay.lat_tiles if last else lay.tiles
        wr = _split_bf16(jnp.concatenate([moe_wg[i], moe_we[i], pad], axis=1))
        br =jnp.concatenate([moe_bg[i], moe_be[i], pad[0]], axis=0)[None]
        route_args = (h, mods, row(ln_g[i, 0]), row(ln_b[i, 0]), wr, br)
        if kind == 0:
            mats = _s5_prepare(s5_lam_re[j], s5_lam_im[j], s5_log_dt[j], s5_b_re[j], s5_b_im[j],
                               s5_c_re[j], s5_c_im[j])
            y = _s5_core(lay, u, mats)
            wglu = s5_w_glu[j].astype(BF16)
            h1, v, meta, cnt = _s5_out(lay, n_tiles, i, y, u, row(s5_d[j]), wglu[:, :d], wglu[:, d:], *route_args)
        elif kind == 1:
            w1 = cv_w_pw1[j].astype(BF16)
            z = _conv_pw1(lay, u, w1[:, :d], w1[:, d:], row(cv_b_pw1[j, :d]), row(cv_b_pw1[j, d:]))
            h1, v, meta, cnt = _conv_out(lay, n_tiles, i, z, cv_w_dw[j], row(cv_b_dw[j]), row(cv_ln_g[j]),
                                         row(cv_ln_b[j]), cv_w_pw2[j].astype(BF16), row(cv_b_pw2[j]),
                                         *route_args)
        else:
            q, k, vv = _qkv_rope(lay, u, at_w_qkv[j].astype(BF16), cos_t, sin_t)
            o = _attention(lay, q, k, vv, row(at_sink[j]), not last)
            h1, v, meta, cnt = _attn_out(lay, n_tiles, i, o, at_w_o[j].astype(BF16), *route_args)
        plan = _moe_plan(meta, cnt, n_tiles)
        xs = _moe_dispatch(v, plan)
        ys = _moe_experts(xs, moe_w1[i], moe_w2[i], plan.items)
        h, u = _moe_combine(lay, n_tiles, ys, plan, h1, mods, i, row(ln_g[i, 1]), row(ln_b[i, 1]), not last)
    return h.reshape(batch, seq, d)
```

```python
import functools
import math

import jax
import jax.numpy as jnp
from jax import lax
from jax.experimental import pallas as pl
from jax.experimental.pallas import tpu as pltpu

F32 = jnp.float32
BF16 = jnp.bfloat16
HIGHEST = lax.Precision.HIGHEST

D_MODEL = 1024
DEPTH = 4
GRID_W = 64
SSM_CH_PER_GROUP = 16
SSM_GROUPS = D_MODEL // SSM_CH_PER_GROUP
SSM_STATE = 64
CONV_WIDTH = 31
HEAD_DIM = 64
N_HEADS = D_MODEL // HEAD_DIM
N_KV_HEADS = N_HEADS // 4
KV_REP = N_HEADS // N_KV_HEADS
WINDOW = 128
ATT_BLOCK = 128
ROPE_BASE = 10000.0
N_EXPERT_GROUPS = 4
EXPERTS_PER_GROUP = 8
N_EXPERTS = N_EXPERT_GROUPS * EXPERTS_PER_GROUP
D_EXPERT = D_MODEL // 2
ALPHA = (2 * DEPTH) ** 0.25
LN_EPS = 1e-5

SUBLANES = 8
LANES = 128

TOK_TILE = 256
EXPERT_TILE = 512
S5_CHUNK = LANES
CONV_HALO = 16
ROUTE_LANES = LANES
META_COLS = 8
RUN_PAD = SUBLANES
LOCAL_ROWS = -(-(2 * TOK_TILE + N_EXPERTS * (RUN_PAD - 1)) // LANES) * LANES
MAX_PIECES = LOCAL_ROWS // RUN_PAD
VMEM_LIMIT = 56 * 1024 * 1024


def _cparams(*sem):
    return pltpu.CompilerParams(dimension_semantics=sem, vmem_limit_bytes=VMEM_LIMIT)


def _round_up(n, m):
    return (n + m - 1) // m * m


def _split_bf16(w):
    hi = w.astype(BF16)
    return jnp.stack([hi, (w - hi.astype(F32)).astype(BF16)])


def _mod_body(c_ref, w_ref, b_ref, o_ref):
    c = c_ref[...]
    s = c * jax.nn.sigmoid(c)
    o_ref[0] = jnp.dot(s, w_ref[0], preferred_element_type=F32, precision=HIGHEST) + b_ref[0]


def _modulation(c_all, ada_w, ada_b):
    depth, d, n = ada_w.shape
    r = c_all.shape[0]
    tn = 1024
    out = pl.pallas_call(
        _mod_body,
        grid=(depth, n // tn),
        in_specs=[
            pl.BlockSpec((r, d), lambda i, j: (0, 0)),
            pl.BlockSpec((1, d, tn), lambda i, j: (i, 0, j)),
            pl.BlockSpec((1, 1, tn), lambda i, j: (i, 0, j)),
        ],
        out_specs=pl.BlockSpec((1, r, tn), lambda i, j: (i, 0, j)),
        out_shape=jax.ShapeDtypeStruct((depth, r, n), F32),
        compiler_params=_cparams("arbitrary", "arbitrary"),
        name="adaln_modulation",
    )(c_all, ada_w, ada_b.reshape(depth, 1, n))
    return out.reshape(depth, r, 6, d)


def _layer_norm_rows(t, g, b):
    mu = jnp.mean(t, axis=-1, keepdims=True)
    dev = t - mu
    var = jnp.mean(dev * dev, axis=-1, keepdims=True)
    return dev * lax.rsqrt(var + LN_EPS) * g + b


def _post_norm_and_route(h, y, mod, lng, lnb, wr, br):
    tm = h.shape[0]
    ng, ne = N_EXPERT_GROUPS, N_EXPERTS
    h1 = _layer_norm_rows(ALPHA * h + mod[2:3] * y, lng, lnb)
    v = h1 * (1.0 + mod[4:5]) + mod[3:4]
    v_hi = v.astype(BF16)
    v_lo = (v - v_hi.astype(F32)).astype(BF16)
    logits = (jnp.dot(v_hi, wr[0], preferred_element_type=F32) + jnp.dot(v_lo, wr[0], preferred_element_type=F32)
              + jnp.dot(v_hi, wr[1], preferred_element_type=F32) + br)

    lane = lax.broadcasted_iota(jnp.int32, (tm, ROUTE_LANES), 1)
    lane_f = lane.astype(F32)
    neg = -jnp.inf
    no_lane = float(ROUTE_LANES)
    is_group = lane < ng
    gl = jnp.where(is_group, logits, neg)
    gmax = jnp.max(gl, axis=-1, keepdims=True)
    gsum = jnp.sum(jnp.where(is_group, jnp.exp(logits - gmax), 0.0), axis=-1, keepdims=True)
    p_group = 1.0 / gsum
    g_idx = jnp.min(jnp.where(gl == gmax, lane_f, no_lane), axis=-1, keepdims=True)
    expert_group = ((lane - ng) // EXPERTS_PER_GROUP).astype(F32)
    in_group = (lane >= ng) & (lane < ng + ne) & (expert_group == g_idx)
    el = jnp.where(in_group, logits, neg)
    v1 = jnp.max(el, axis=-1, keepdims=True)
    i1 = jnp.min(jnp.where(el == v1, lane_f, no_lane), axis=-1, keepdims=True)
    el2 = jnp.where(lane_f == i1, neg, el)
    v2 = jnp.max(el2, axis=-1, keepdims=True)
    i2 = jnp.min(jnp.where(el2 == v2, lane_f, no_lane), axis=-1, keepdims=True)
    e21 = jnp.exp(v2 - v1)
    w1 = p_group / (1.0 + e21)
    w2 = p_group * e21 / (1.0 + e21)

    hit1 = lane_f == i1
    hit2 = lane_f == i2
    one1 = hit1.astype(BF16)
    one2 = hit2.astype(BF16)
    rr = lax.broadcasted_iota(jnp.int32, (tm, tm), 0)
    cc = lax.broadcasted_iota(jnp.int32, (tm, tm), 1)
    before = (cc < rr).astype(BF16)
    cum1 = jnp.dot(before, one1, preferred_element_type=F32)
    cum2 = jnp.dot(before, one2, preferred_element_type=F32)
    tot1 = jnp.sum(one1.astype(F32), axis=0, keepdims=True)
    tot2 = jnp.sum(one2.astype(F32), axis=0, keepdims=True)
    rank1 = jnp.sum(jnp.where(hit1, cum1, 0.0), axis=-1, keepdims=True)
    rank2 = jnp.sum(jnp.where(hit2, tot1 + cum2, 0.0), axis=-1, keepdims=True)

    col = lax.broadcasted_iota(jnp.int32, (tm, META_COLS), 1)
    meta = jnp.where(col == 0, i1 - ng,
           jnp.where(col == 1, i2 - ng,
           jnp.where(col == 2, rank1,
           jnp.where(col == 3, rank2,
           jnp.where(col == 4, w1,
           jnp.where(col == 5, w2, 0.0))))))
    return h1, v, meta, tot1 + tot2


def _gelu_tanh(x):
    return 0.5 * x * (1.0 + jnp.tanh(math.sqrt(2.0 / math.pi) * (x + 0.044715 * (x * x * x))))


class _Layout:
    def __init__(self, batch, seq, ctx_len):
        self.batch, self.seq, self.ctx_len = batch, seq, ctx_len
        self.n_lat = batch * seq
        self.n_ctx = batch * ctx_len
        self.n_tok = self.n_lat + self.n_ctx
        assert seq % TOK_TILE == 0 and ctx_len % TOK_TILE == 0
        self.lat_tiles = self.n_lat // TOK_TILE
        self.tiles = self.n_tok // TOK_TILE
        self.tiles_per_seq = seq // TOK_TILE
        self.tiles_per_ctx = ctx_len // TOK_TILE
        self.mod_rows = _round_up(batch + 1, SUBLANES)

    def mod_row(self, i):
        return jnp.minimum(i // self.tiles_per_seq, self.batch)


def _mod_spec(lay, layer):
    return pl.BlockSpec((1, 1, 6, D_MODEL), lambda i, *_: (layer, lay.mod_row(i), 0, 0))


def _row_spec(tm=TOK_TILE, d=D_MODEL):
    return pl.BlockSpec((tm, d), lambda i, *_: (i, 0))


def _const_spec(shape):
    nd = len(shape)
    return pl.BlockSpec(shape, lambda i, *_: (0,) * nd)


def _modulate_body(h_ref, mod_ref, u_ref):
    mod = mod_ref[0, 0]
    u_ref[...] = (h_ref[...] * (1.0 + mod[1:2]) + mod[0:1]).astype(u_ref.dtype)


def _modulate(lay, h, mods, layer):
    return pl.pallas_call(
        _modulate_body,
        grid=(lay.tiles,),
        in_specs=[_row_spec(), _mod_spec(lay, layer)],
        out_specs=_row_spec(),
        out_shape=jax.ShapeDtypeStruct((lay.n_tok, D_MODEL), BF16),
        compiler_params=_cparams("arbitrary"),
        name="input_modulate",
    )(h, mods)


def _piece_spec(n_tiles, shift=0):
    return pl.BlockSpec((1, 1, MAX_PIECES), lambda i, *_: (jnp.minimum(i + shift, n_tiles - 1), 0, 0),
                        memory_space=pltpu.SMEM)


def _dispatch_body(np_ref, tail_ref, dst_ref, lp_ref, v_ref, xs_ref, loc, zeros, sem):
    i = pl.program_id(0)
    lp = lp_ref[0]
    slot_row = lax.broadcasted_iota(jnp.int32, (LOCAL_ROWS, v_ref.shape[0]), 0).astype(F32)
    pick = ((slot_row == lp[0:1]) | (slot_row == lp[1:2])).astype(BF16)
    loc[...] = jnp.dot(pick, v_ref[...], preferred_element_type=F32)

    def piece(src, dst):
        return pltpu.make_async_copy(src, xs_ref.at[pl.ds(pl.multiple_of(dst, RUN_PAD), RUN_PAD)], sem)

    def issue(q, carry):
        piece(loc.at[pl.ds(pl.multiple_of(q * RUN_PAD, RUN_PAD), RUN_PAD)], dst_ref[0, 0, q]).start()
        return carry

    def drain(q, carry):
        piece(loc.at[pl.ds(0, RUN_PAD)], 0).wait()
        return carry

    n = np_ref[i]
    lax.fori_loop(0, n, issue, 0)
    lax.fori_loop(0, n, drain, 0)

    @pl.when(i == pl.num_programs(0) - 1)
    def _():
        zeros[...] = jnp.zeros_like(zeros)

        def issue_zero(q, carry):
            piece(zeros, tail_ref[0] + q * RUN_PAD).start()
            return carry

        def drain_zero(q, carry):
            piece(zeros, 0).wait()
            return carry

        lax.fori_loop(0, tail_ref[1], issue_zero, 0)
        lax.fori_loop(0, tail_ref[1], drain_zero, 0)


def _moe_dispatch(v, plan):
    n_tiles = plan.n_pieces.shape[0]
    grid_spec = pltpu.PrefetchScalarGridSpec(
        num_scalar_prefetch=2,
        grid=(n_tiles,),
        in_specs=[_piece_spec(n_tiles),
                  pl.BlockSpec((1, 2, TOK_TILE), lambda i, *_: (i, 0, 0)),
                  _row_spec()],
        out_specs=pl.BlockSpec(memory_space=pl.ANY),
        scratch_shapes=[pltpu.VMEM((LOCAL_ROWS, D_MODEL), F32), pltpu.VMEM((RUN_PAD, D_MODEL), F32),
                        pltpu.SemaphoreType.DMA(())],
    )
    return pl.pallas_call(
        _dispatch_body,
        grid_spec=grid_spec,
        out_shape=jax.ShapeDtypeStruct((plan.max_rows, D_MODEL), F32),
        compiler_params=_cparams("arbitrary"),
        name="moe_dispatch",
    )(plan.n_pieces, plan.tail, plan.dst, plan.lp_rows, v)


def _expert_body(tile_ref, exp_ref, lo_ref, hi_ref, xs_ref, w1_ref, w2_ref, ys_ref, w1b, w2b):
    j = pl.program_id(0)
    jp = jnp.maximum(j - 1, 0)
    new_expert = (j == 0) | (exp_ref[j] != exp_ref[jp])
    first_visit = (j == 0) | (tile_ref[j] != tile_ref[jp])
    lo, hi = lo_ref[j], hi_ref[j]

    @pl.when(new_expert)
    def _():
        w1b[...] = w1_ref[0].astype(BF16)
        w2b[...] = w2_ref[0].astype(BF16)

    def expert_rows():
        x = xs_ref[...].astype(BF16)
        gu = jnp.dot(x, w1b[...], preferred_element_type=F32)
        gate, up = gu[:, :D_EXPERT], gu[:, D_EXPERT:]
        a = (gate * jax.nn.sigmoid(gate) * up).astype(BF16)
        y = jnp.dot(a, w2b[...], preferred_element_type=F32)
        rows = lax.broadcasted_iota(jnp.int32, (xs_ref.shape[0], 1), 0)
        return y, (rows >= lo) & (rows < hi)

    @pl.when(first_visit)
    def _():
        y, mine = expert_rows()
        ys_ref[...] = jnp.where(mine, y, 0.0)

    @pl.when(jnp.logical_not(first_visit) & (hi > lo))
    def _():
        y, mine = expert_rows()
        ys_ref[...] = jnp.where(mine, y, ys_ref[...])


def _moe_experts(xs, w1, w2, items):
    tile_j, exp_j, lo_j, hi_j = items
    n_items = tile_j.shape[0]
    grid_spec = pltpu.PrefetchScalarGridSpec(
        num_scalar_prefetch=4,
        grid=(n_items,),
        in_specs=[
            pl.BlockSpec((EXPERT_TILE, D_MODEL), lambda j, t, e, lo, hi: (t[j], 0)),
            pl.BlockSpec((1, D_MODEL, 2 * D_EXPERT), lambda j, t, e, lo, hi: (e[j], 0, 0)),
            pl.BlockSpec((1, D_EXPERT, D_MODEL), lambda j, t, e, lo, hi: (e[j], 0, 0)),
        ],
        out_specs=pl.BlockSpec((EXPERT_TILE, D_MODEL), lambda j, t, e, lo, hi: (t[j], 0)),
        scratch_shapes=[pltpu.VMEM((D_MODEL, 2 * D_EXPERT), BF16), pltpu.VMEM((D_EXPERT, D_MODEL), BF16)],
    )
    return pl.pallas_call(
        _expert_body,
        grid_spec=grid_spec,
        out_shape=jax.ShapeDtypeStruct(xs.shape, F32),
        compiler_params=_cparams("arbitrary"),
        name="moe_experts",
    )(tile_j, exp_j, lo_j, hi_j, xs, w1, w2)


def _combine_body(has_next, np_ref, src_ref, srcn_ref, lpw_ref, h1_ref, mod_ref, modn_ref, lng_ref, lnb_ref,
                  ys_ref, *rest):
    if has_next:
        h2_ref, u_ref, buf, sem = rest
    else:
        h2_ref, buf, sem = rest
    tm = h1_ref.shape[0]
    i = pl.program_id(0)
    n = pl.num_programs(0)
    slot = i % 2

    def piece(src, s, q):
        return pltpu.make_async_copy(ys_ref.at[pl.ds(pl.multiple_of(src, RUN_PAD), RUN_PAD)],
                                     buf.at[s, pl.ds(pl.multiple_of(q * RUN_PAD, RUN_PAD), RUN_PAD)], sem.at[s])

    def gather(sref, s, count):
        def issue(q, carry):
            piece(sref[0, 0, q], s, q).start()
            return carry
        lax.fori_loop(0, count, issue, 0)

    @pl.when(i == 0)
    def _():
        buf[...] = jnp.zeros_like(buf)
        gather(src_ref, 0, np_ref[0])

    @pl.when(i + 1 < n)
    def _():
        gather(srcn_ref, 1 - slot, np_ref[jnp.minimum(i + 1, n - 1)])

    def drain(q, carry):
        piece(0, slot, 0).wait()
        return carry
    lax.fori_loop(0, np_ref[i], drain, 0)

    lpw = lpw_ref[...]
    lane = lax.broadcasted_iota(jnp.int32, (tm, LOCAL_ROWS), 1).astype(F32)
    wmat = jnp.where(lane == lpw[:, 0:1], lpw[:, 2:3], 0.0) + jnp.where(lane == lpw[:, 1:2], lpw[:, 3:4], 0.0)
    w_hi = wmat.astype(BF16)
    w_lo = (wmat - w_hi.astype(F32)).astype(BF16)
    yb = buf[slot].astype(BF16)
    f = jnp.dot(w_hi, yb, preferred_element_type=F32) + jnp.dot(w_lo, yb, preferred_element_type=F32)
    mod = mod_ref[0, 0]
    h2 = _layer_norm_rows(ALPHA * h1_ref[...] + mod[5:6] * f, lng_ref[...], lnb_ref[...])
    h2_ref[...] = h2
    if has_next:
        modn = modn_ref[0, 0]
        u_ref[...] = (h2 * (1.0 + modn[1:2]) + modn[0:1]).astype(u_ref.dtype)


def _moe_combine(lay, n_tiles, ys, plan, h1, mods, layer, lng, lnb, has_next):
    n_rows = n_tiles * TOK_TILE
    nxt = min(layer + 1, DEPTH - 1)
    out_shape = [jax.ShapeDtypeStruct((n_rows, D_MODEL), F32)]
    out_specs = [_row_spec()]
    if has_next:
        out_shape.append(jax.ShapeDtypeStruct((n_rows, D_MODEL), BF16))
        out_specs.append(_row_spec())
    grid_spec = pltpu.PrefetchScalarGridSpec(
        num_scalar_prefetch=1,
        grid=(n_tiles,),
        in_specs=[
            _piece_spec(n_tiles), _piece_spec(n_tiles, 1),
            _row_spec(d=4), _row_spec(), _mod_spec(lay, layer), _mod_spec(lay, nxt),
            _const_spec((1, D_MODEL)), _const_spec((1, D_MODEL)),
            pl.BlockSpec(memory_space=pl.ANY),
        ],
        out_specs=out_specs,
        scratch_shapes=[pltpu.VMEM((2, LOCAL_ROWS, D_MODEL), F32), pltpu.SemaphoreType.DMA((2,))],
    )
    outs = pl.pallas_call(
        functools.partial(_combine_body, has_next),
        grid_spec=grid_spec,
        out_shape=out_shape,
        compiler_params=_cparams("arbitrary"),
        name="moe_combine",
    )(plan.n_pieces, plan.dst, plan.dst, plan.lp_w, h1, mods, mods, lng, lnb, ys)
    return outs if has_next else (outs[0], None)


class _MoePlan:
    pass


def _exclusive_cumsum(a, axis):
    return jnp.cumsum(a, axis=axis) - a


def _moe_plan(meta, tile_counts, n_tiles):
    plan = _MoePlan()
    experts = jnp.arange(N_EXPERTS, dtype=jnp.int32)
    plan.max_rows = _round_up(n_tiles * (2 * TOK_TILE + N_EXPERTS * (RUN_PAD - 1)), EXPERT_TILE)
    cnt = tile_counts[:, 0, N_EXPERT_GROUPS:N_EXPERT_GROUPS + N_EXPERTS].astype(jnp.int32)
    run = (cnt + RUN_PAD - 1) // RUN_PAD * RUN_PAD
    local_start = _exclusive_cumsum(run, 1)
    counts = jnp.sum(run, axis=0)
    total = jnp.sum(counts)
    tail = (-total) % EXPERT_TILE
    counts = counts.at[N_EXPERTS - 1].add(tail)
    ends = jnp.cumsum(counts)
    starts = ends - counts
    run_start = starts[None, :] + _exclusive_cumsum(run, 0)
    plan.tail = jnp.stack([total, tail // RUN_PAD]).astype(jnp.int32)

    eid = meta[:, 0:2].astype(jnp.int32).reshape(n_tiles, TOK_TILE, 2)
    onehot = eid[..., None] == experts
    lp = meta[:, 2:4].reshape(n_tiles, TOK_TILE, 2) + jnp.sum(
        jnp.where(onehot, local_start[:, None, None, :], 0), axis=-1).astype(F32)
    plan.lp_rows = lp.transpose(0, 2, 1)
    plan.lp_w = jnp.concatenate([lp.reshape(-1, 2), meta[:, 4:6]], axis=1)

    pieces = run // RUN_PAD
    piece_end = jnp.cumsum(pieces, axis=1)
    plan.n_pieces = piece_end[:, -1].astype(jnp.int32)
    q = jnp.arange(MAX_PIECES, dtype=jnp.int32)
    owner = jnp.minimum(jnp.sum(q[None, :, None] >= piece_end[:, None, :], axis=-1), N_EXPERTS - 1)
    pick = owner[..., None] == experts
    first = jnp.sum(jnp.where(pick, (piece_end - pieces)[:, None, :], 0), axis=-1)
    base = jnp.sum(jnp.where(pick, run_start[:, None, :], 0), axis=-1)
    dst = jnp.where(q[None, :] < plan.n_pieces[:, None], base + (q[None, :] - first) * RUN_PAD, 0)
    plan.dst = dst.astype(jnp.int32).reshape(n_tiles, 1, MAX_PIECES)

    n_etiles = plan.max_rows // EXPERT_TILE
    first_tile = starts // EXPERT_TILE
    last_tile = jnp.maximum(ends - 1, 0) // EXPERT_TILE
    n_items_e = jnp.where(counts > 0, last_tile - first_tile + 1, 0)
    item_end = jnp.cumsum(n_items_e)
    item_start = item_end - n_items_e
    n_items = n_etiles + N_EXPERTS
    j = jnp.arange(n_items, dtype=jnp.int32)
    e_j = jnp.minimum(jnp.sum(j[:, None] >= item_end[None, :], axis=1), N_EXPERTS - 1).astype(jnp.int32)
    active = j < item_end[-1]
    tile_j = jnp.take(first_tile, e_j) + (j - jnp.take(item_start, e_j))
    lo = jnp.maximum(jnp.take(starts, e_j), tile_j * EXPERT_TILE) - tile_j * EXPERT_TILE
    hi = jnp.minimum(jnp.take(ends, e_j), (tile_j + 1) * EXPERT_TILE) - tile_j * EXPERT_TILE
    last = jnp.maximum(item_end[-1] - 1, 0)
    tile_j = jnp.where(active, tile_j, tile_j[last]).astype(jnp.int32)
    e_j = jnp.where(active, e_j, e_j[last]).astype(jnp.int32)
    lo = jnp.where(active, lo, 0).astype(jnp.int32)
    hi = jnp.where(active, hi, 0).astype(jnp.int32)
    plan.items = (tile_j, e_j, lo, hi)
    return plan


def _route_outputs(n_rows):
    n_tiles = n_rows // TOK_TILE
    shapes = [jax.ShapeDtypeStruct((n_rows, D_MODEL), F32),
              jax.ShapeDtypeStruct((n_rows, D_MODEL), BF16),
              jax.ShapeDtypeStruct((n_rows, META_COLS), F32),
              jax.ShapeDtypeStruct((n_tiles, 1, ROUTE_LANES), F32)]
    specs = [_row_spec(), _row_spec(), _row_spec(d=META_COLS),
             pl.BlockSpec((1, 1, ROUTE_LANES), lambda i: (i, 0, 0))]
    return shapes, specs


def _route_inputs(lay, layer):
    return [_row_spec(), _mod_spec(lay, layer), _const_spec((1, D_MODEL)), _const_spec((1, D_MODEL)),
            _const_spec((2, D_MODEL, ROUTE_LANES)), _const_spec((1, ROUTE_LANES))]


def _finish_route(y, h_ref, mod_ref, lng_ref, lnb_ref, wr_ref, br_ref, h1_ref, v_ref, meta_ref, cnt_ref):
    h1, v, meta, counts = _post_norm_and_route(h_ref[...], y, mod_ref[0, 0], lng_ref[...], lnb_ref[...],
                                               wr_ref[...], br_ref[...])
    h1_ref[...] = h1
    v_ref[...] = v.astype(v_ref.dtype)
    meta_ref[...] = meta
    cnt_ref[0] = counts


def _attn_out_body(o_ref, wo_ref, *rest):
    y = jnp.dot(o_ref[...], wo_ref[...], preferred_element_type=F32)
    _finish_route(y, *rest)


def _attn_out(lay, n_tiles, layer, o, w_o, h, mods, lng, lnb, wr, br):
    shapes, specs = _route_outputs(n_tiles * TOK_TILE)
    return pl.pallas_call(
        _attn_out_body,
        grid=(n_tiles,),
        in_specs=[_row_spec(), _const_spec((D_MODEL, D_MODEL))] + _route_inputs(lay, layer),
        out_specs=specs,
        out_shape=shapes,
        compiler_params=_cparams("arbitrary"),
        name="attn_out_route",
    )(o, w_o, h, mods, lng, lnb, wr, br)


def _rope_tables(seq):
    quarter = HEAD_DIM // 4
    inv_freq = ROPE_BASE ** (-jnp.arange(quarter, dtype=F32) / quarter)
    t = jnp.arange(seq)
    rows = (t // GRID_W).astype(F32)
    cols = (t % GRID_W).astype(F32)
    ang_r = rows[:, None] * inv_freq
    ang_c = cols[:, None] * inv_freq
    cos_h = jnp.concatenate([jnp.cos(ang_r), jnp.cos(ang_r), jnp.cos(ang_c), jnp.cos(ang_c)], axis=-1)
    sin_h = jnp.concatenate([-jnp.sin(ang_r), jnp.sin(ang_r), -jnp.sin(ang_c), jnp.sin(ang_c)], axis=-1)
    rep = LANES // HEAD_DIM
    return jnp.tile(cos_h, (1, rep)), jnp.tile(sin_h, (1, rep))


def _rope(x, cos, sin):
    quarter = HEAD_DIM // 4
    width = x.shape[1]
    n = width // LANES
    c = jnp.tile(cos, (1, n))
    s = jnp.tile(sin, (1, n))
    lane = lax.broadcasted_iota(jnp.int32, x.shape, 1)
    is_lo = (lane % (2 * quarter)) < quarter
    partner = jnp.where(is_lo, pltpu.roll(x, width - quarter, 1), pltpu.roll(x, quarter, 1))
    return x * c + partner * s


def _qkv_body(lat_tiles, u_ref, w_ref, cos_ref, sin_ref, q_ref, k_ref, v_ref):
    is_ctx = pl.program_id(0) >= lat_tiles
    qkv = jnp.dot(u_ref[...], w_ref[...], preferred_element_type=F32)
    nq, nkv = N_HEADS * HEAD_DIM, N_KV_HEADS * HEAD_DIM
    q, k, v = qkv[:, :nq], qkv[:, nq:nq + nkv], qkv[:, nq + nkv:]
    cos, sin = cos_ref[...], sin_ref[...]
    q = jnp.where(is_ctx, q, _rope(q, cos, sin))
    k = jnp.where(is_ctx, k, _rope(k, cos, sin))
    q_ref[...] = (q * HEAD_DIM ** -0.5).astype(q_ref.dtype)
    k_ref[...] = k.astype(k_ref.dtype)
    v_ref[...] = v.astype(v_ref.dtype)


def _qkv_rope(lay, u, w_qkv, cos_t, sin_t):
    nq, nkv = N_HEADS * HEAD_DIM, N_KV_HEADS * HEAD_DIM
    tab = pl.BlockSpec((TOK_TILE, LANES),
                       lambda i: (jnp.where(i < lay.lat_tiles, i % lay.tiles_per_seq, 0), 0))
    return pl.pallas_call(
        functools.partial(_qkv_body, lay.lat_tiles),
        grid=(lay.tiles,),
        in_specs=[_row_spec(), _const_spec((D_MODEL, nq + 2 * nkv)), tab, tab],
        out_specs=[_row_spec(), _row_spec(d=nkv), _row_spec(d=nkv)],
        out_shape=[jax.ShapeDtypeStruct((lay.n_tok, nq), BF16),
                   jax.ShapeDtypeStruct((lay.n_tok, nkv), BF16),
                   jax.ShapeDtypeStruct((lay.n_tok, nkv), BF16)],
        compiler_params=_cparams("arbitrary"),
        name="attn_qkv_rope",
    )(u, w_qkv, cos_t, sin_t)


def _attn_body(nqb, sink_ref, q_ref, kp_ref, kc_ref, kn_ref, kx_ref, vp_ref, vc_ref, vn_ref, vx_ref,
               o_ref, kcat, vcat):
    j = pl.program_id(1)
    blk = ATT_BLOCK
    n_win = 3 * blk
    for dst, parts in ((kcat, (kp_ref, kc_ref, kn_ref)), (vcat, (vp_ref, vc_ref, vn_ref))):
        for n, part in enumerate(parts):
            dst[n * blk:(n + 1) * blk] = part[...]
    kcat[n_win:] = kx_ref[...]
    vcat[n_win:] = vx_ref[...]
    rows = KV_REP * blk
    r = lax.broadcasted_iota(jnp.int32, (rows, n_win), 0) % blk
    c = lax.broadcasted_iota(jnp.int32, (rows, n_win), 1)
    valid = (jnp.abs(r + blk - c) <= WINDOW) & (j < nqb)
    valid = valid & ((c >= blk) | (j > 0)) & ((c < 2 * blk) | (j < nqb - 1))
    rep = lax.broadcasted_iota(jnp.int32, (rows, 1), 0) // blk
    for g in range(N_KV_HEADS):
        heads = [slice((g * KV_REP + n) * HEAD_DIM, (g * KV_REP + n + 1) * HEAD_DIM) for n in range(KV_REP)]
        kv_head = slice(g * HEAD_DIM, (g + 1) * HEAD_DIM)
        q4 = jnp.concatenate([q_ref[:, head] for head in heads], axis=0)
        s = lax.dot_general(q4, kcat[:, kv_head], (((1,), (1,)), ((), ())), preferred_element_type=F32)
        sink = jnp.zeros((rows, 1), F32)
        for n in range(KV_REP):
            sink = jnp.where(rep == n, sink_ref[0, g * KV_REP + n], sink)
        sw = jnp.where(valid, s[:, :n_win], -jnp.inf)
        sc = s[:, n_win:]
        m = jnp.maximum(jnp.max(sw, axis=-1, keepdims=True), jnp.max(sc, axis=-1, keepdims=True))
        m = jnp.maximum(m, sink)
        pw = jnp.exp(sw - m)
        pc = jnp.exp(sc - m)
        denom = (jnp.sum(pw, axis=-1, keepdims=True) + jnp.sum(pc, axis=-1, keepdims=True)
                 + jnp.exp(sink - m))
        acc = jnp.dot(pw.astype(BF16), vcat[:n_win, kv_head], preferred_element_type=F32)
        acc = acc + jnp.dot(pc.astype(BF16), vcat[n_win:, kv_head], preferred_element_type=F32)
        out = (acc / denom).astype(o_ref.dtype)
        for n, head in enumerate(heads):
            o_ref[:, head] = out[n * blk:(n + 1) * blk]


def _attention(lay, q, k, v, sink, with_ctx_out):
    nq, nkv = N_HEADS * HEAD_DIM, N_KV_HEADS * HEAD_DIM
    blk = ATT_BLOCK
    nqb = lay.seq // blk
    lc = lay.ctx_len
    assert lay.n_lat % lc == 0 and lay.seq % blk == 0 and lc % blk == 0
    ctx0 = lay.n_lat // lc
    ncb = lc // blk
    steps = nqb + (ncb if with_ctx_out else 0)
    q_blk = pl.BlockSpec(
        (blk, nq), lambda b, j: (jnp.where(j < nqb, b * nqb + j, (ctx0 + b) * ncb + j - nqb), 0))
    kv_blk = lambda shift: pl.BlockSpec(
        (blk, nkv), lambda b, j: (b * nqb + jnp.clip(j + shift, 0, nqb - 1), 0))
    kv_ctx = pl.BlockSpec((lc, nkv), lambda b, j: (ctx0 + b, 0))
    return pl.pallas_call(
        functools.partial(_attn_body, nqb),
        grid=(lay.batch, steps),
        in_specs=[pl.BlockSpec(memory_space=pltpu.SMEM), q_blk,
                  kv_blk(-1), kv_blk(0), kv_blk(1), kv_ctx,
                  kv_blk(-1), kv_blk(0), kv_blk(1), kv_ctx],
        out_specs=q_blk,
        out_shape=jax.ShapeDtypeStruct((lay.n_tok if with_ctx_out else lay.n_lat, nq), BF16),
        scratch_shapes=[pltpu.VMEM((3 * blk + lc, nkv), BF16), pltpu.VMEM((3 * blk + lc, nkv), BF16)],
        compiler_params=_cparams("arbitrary", "arbitrary"),
        name="attn_window",
    )(sink, q, k, k, k, k, v, v, v, v)


def _pw1_body(u_ref, wa_ref, wg_ref, ba_ref, bg_ref, z_ref):
    u = u_ref[...]
    a = jnp.dot(u, wa_ref[...], preferred_element_type=F32) + ba_ref[...]
    g = jnp.dot(u, wg_ref[...], preferred_element_type=F32) + bg_ref[...]
    z_ref[...] = (a * jax.nn.sigmoid(g)).astype(z_ref.dtype)


def _conv_pw1(lay, u, wa, wg, ba, bg):
    return pl.pallas_call(
        _pw1_body,
        grid=(lay.tiles,),
        in_specs=[_row_spec(), _const_spec((D_MODEL, D_MODEL)), _const_spec((D_MODEL, D_MODEL)),
                  _const_spec((1, D_MODEL)), _const_spec((1, D_MODEL))],
        out_specs=_row_spec(),
        out_shape=jax.ShapeDtypeStruct((lay.n_tok, D_MODEL), BF16),
        compiler_params=_cparams("arbitrary"),
        name="conv_pw1_glu",
    )(u, wa, wg, ba, bg)


def _conv_body(lat_tiles, tps, tpc, zp_ref, z_ref, zn_ref, wdw_ref, bdw_ref, cg_ref, cb_ref, w2_ref, b2_ref,
               h_ref, mod_ref, lng_ref, lnb_ref, wr_ref, br_ref, h1_ref, v_ref, meta_ref, cnt_ref, win, shifted):
    i = pl.program_id(0)
    tm = z_ref.shape[0]
    halo = CONV_HALO
    is_ctx = i >= lat_tiles
    pos = jnp.where(is_ctx, (i - lat_tiles) % tpc, i % tps)
    first = pos == 0
    last = pos == jnp.where(is_ctx, tpc, tps) - 1
    win[0:halo] = jnp.where(first, 0.0, zp_ref[...].astype(F32))
    win[halo:halo + tm] = z_ref[...].astype(F32)
    win[halo + tm:] = jnp.where(last, 0.0, zn_ref[...].astype(F32))
    off = halo - CONV_WIDTH // 2
    span = _round_up(off + CONV_WIDTH - 1, SUBLANES) - SUBLANES
    conv = jnp.zeros((tm, D_MODEL), F32)
    for res in range(SUBLANES):
        taps = [o - off for o in range(res, off + CONV_WIDTH, SUBLANES) if o >= off]
        if not taps:
            continue
        shifted[...] = win[res:res + tm + span, :]
        for tap in taps:
            base = off + tap - res
            conv = conv + wdw_ref[tap:tap + 1, :] * shifted[base:base + tm, :]
    conv = conv + bdw_ref[...]
    nrm = _layer_norm_rows(conv, cg_ref[...], cb_ref[...])
    act = (nrm * jax.nn.sigmoid(nrm)).astype(BF16)
    y = jnp.dot(act, w2_ref[...], preferred_element_type=F32) + b2_ref[...]
    _finish_route(y, h_ref, mod_ref, lng_ref, lnb_ref, wr_ref, br_ref, h1_ref, v_ref, meta_ref, cnt_ref)


def _conv_out(lay, n_tiles, layer, z, w_dw, b_dw, cg, cb, w2, b2, h, mods, lng, lnb, wr, br):
    per_tile = TOK_TILE // CONV_HALO
    n_halo = lay.n_tok // CONV_HALO
    shapes, specs = _route_outputs(n_tiles * TOK_TILE)
    halo_spec = lambda f: pl.BlockSpec((CONV_HALO, D_MODEL), lambda i: (jnp.clip(f(i), 0, n_halo - 1), 0))
    return pl.pallas_call(
        functools.partial(_conv_body, lay.lat_tiles, lay.tiles_per_seq, lay.tiles_per_ctx),
        grid=(n_tiles,),
        in_specs=[halo_spec(lambda i: i * per_tile - 1), _row_spec(), halo_spec(lambda i: (i + 1) * per_tile),
                  _const_spec((CONV_WIDTH, D_MODEL)), _const_spec((1, D_MODEL)), _const_spec((1, D_MODEL)),
                  _const_spec((1, D_MODEL)), _const_spec((D_MODEL, D_MODEL)), _const_spec((1, D_MODEL))]
                 + _route_inputs(lay, layer),
        out_specs=specs,
        out_shape=shapes,
        scratch_shapes=[pltpu.VMEM((TOK_TILE + 2 * CONV_HALO, D_MODEL), F32),
                        pltpu.VMEM((TOK_TILE + 2 * CONV_HALO - SUBLANES, D_MODEL), F32)],
        compiler_params=_cparams("arbitrary"),
        name="conv_dw_out_route",
    )(z, z, z, w_dw, b_dw, cg, cb, w2, b2, h, mods, lng, lnb, wr, br)


def _iota(shape, axis):
    return lax.broadcasted_iota(jnp.int32, shape, axis)


def _s5_prep_body(lrow_ref, lcol_ref, bt_ref, ct_ref, kk_ref, min_ref, mout_ref, arec_ref):
    tc, ch, ns = S5_CHUNK, SSM_CH_PER_GROUP, SSM_STATE
    kw = 2 * tc
    lrow = lrow_ref[0]
    lcol = lcol_ref[0]
    exact_dot = functools.partial(jnp.dot, preferred_element_type=F32, precision=HIGHEST)
    t_lane = _iota((1, tc), 1).astype(F32)
    s_row = _iota((tc, 1), 0).astype(F32)
    lag_lane = _iota((1, kw), 1) - (tc - 1)
    mout_ref[...] = jnp.zeros_like(mout_ref)
    kk = jnp.zeros((ch, ch * kw), F32)
    a_re, a_im = [], []
    for d in range(2):
        lr2 = jnp.minimum(lrow[3 * d:3 * d + 1], -1e-4)
        li2 = lrow[3 * d + 1:3 * d + 2]
        dt2 = jnp.exp(lrow[3 * d + 2:3 * d + 3])
        lr, li, dt = lr2[:, :ns], li2[:, :ns], dt2[:, :ns]
        lrc = jnp.minimum(lcol[:, 3 * d:3 * d + 1], -1e-4)
        lic = lcol[:, 3 * d + 1:3 * d + 2]
        dtc = jnp.exp(lcol[:, 3 * d + 2:3 * d + 3])
        er = jnp.exp(lr * dt)
        xr = er * jnp.cos(li * dt) - 1.0
        xi = er * jnp.sin(li * dt)
        den = lr * lr + li * li
        qr = (xr * lr + xi * li) / den
        qi = (xi * lr - xr * li) / den
        btr, bti = bt_ref[0, 2 * d], bt_ref[0, 2 * d + 1]
        bbr = qr * btr - qi * bti
        bbi = qr * bti + qi * btr
        n_in = (tc - 1.0 - s_row) if d == 0 else s_row
        mg = jnp.exp(n_in * (lr * dt))
        pr, pi = mg * jnp.cos(n_in * (li * dt)), mg * jnp.sin(n_in * (li * dt))
        re0, im0 = ns * d, 2 * ns + ns * d
        for c in range(ch):
            br = jnp.broadcast_to(bbr[c:c + 1], (tc, ns))
            bi = jnp.broadcast_to(bbi[c:c + 1], (tc, ns))
            min_ref[0, tc * c:tc * (c + 1), re0:re0 + ns] = (pr * br - pi * bi).astype(BF16)
            min_ref[0, tc * c:tc * (c + 1), im0:im0 + ns] = (pr * bi + pi * br).astype(BF16)

        ctr, cti = ct_ref[0, 2 * d], ct_ref[0, 2 * d + 1]

        def readout(n_of_lane, keep=None):
            width = n_of_lane.shape[1]
            mag = jnp.exp(n_of_lane * (lrc * dtc))
            w_re, w_im = mag * jnp.cos(n_of_lane * (lic * dtc)), mag * jnp.sin(n_of_lane * (lic * dtc))
            if keep is not None:
                w_re, w_im = jnp.where(keep, w_re, 0.0), jnp.where(keep, w_im, 0.0)
            res_re, res_im = [], []
            for c in range(ch):
                cr = jnp.broadcast_to(ctr[:, c:c + 1], (ns, width))
                ci = jnp.broadcast_to(cti[:, c:c + 1], (ns, width))
                res_re.append(cr * w_re - ci * w_im)
                res_im.append(-(cr * w_im + ci * w_re))
            return jnp.concatenate(res_re, axis=1), jnp.concatenate(res_im, axis=1)

        o_re, o_imneg = readout(t_lane + 1.0 if d == 0 else tc - t_lane)
        base = 4 * ns * d
        mout_ref[0, base + re0:base + re0 + ns, :] = o_re.astype(BF16)
        mout_ref[0, base + im0:base + im0 + ns, :] = o_imneg.astype(BF16)
        lag = lag_lane if d == 0 else -lag_lane
        k_re, k_imneg = readout(jnp.maximum(lag, 0).astype(F32), lag >= 0)
        kk = kk + exact_dot(bbr, k_re) + exact_dot(bbi, k_imneg)
        e2 = jnp.exp(tc * lr2 * dt2)
        a_re.append(e2 * jnp.cos(tc * li2 * dt2))
        a_im.append(e2 * jnp.sin(tc * li2 * dt2))
    kk_ref[0] = kk
    is_fwd = _iota((1, 2 * ns), 1) < ns
    arec_ref[0, 0:1] = jnp.where(is_fwd, a_re[0], a_re[1])
    arec_ref[0, 1:2] = jnp.where(is_fwd, a_im[0], a_im[1])


def _s5_prepare(lam_re, lam_im, log_dt, b_re, b_im, c_re, c_im):
    g, ns, ch, tc = SSM_GROUPS, SSM_STATE, SSM_CH_PER_GROUP, S5_CHUNK
    ldt = jnp.broadcast_to(log_dt[:, :, None], lam_re.shape)
    stack = jnp.stack([lam_re[0], lam_im[0], ldt[0], lam_re[1], lam_im[1], ldt[1]], axis=1)
    lrow = jnp.concatenate([stack, stack], axis=-1)
    lcol = jnp.swapaxes(stack, 1, 2)
    bt = jnp.stack([b_re[0], b_im[0], b_re[1], b_im[1]], axis=1).swapaxes(2, 3)
    ct = jnp.stack([c_re[0], c_im[0], c_re[1], c_im[1]], axis=1).swapaxes(2, 3)
    blk = lambda *s: pl.BlockSpec((1,) + s, lambda i: (i,) + (0,) * len(s))
    return pl.pallas_call(
        _s5_prep_body,
        grid=(g,),
        in_specs=[blk(6, 2 * ns), blk(ns, 6), blk(4, ch, ns), blk(4, ns, ch)],
        out_specs=[blk(ch, 2 * tc * ch), blk(tc * ch, 4 * ns), blk(8 * ns, tc * ch), blk(2, 2 * ns)],
        out_shape=[jax.ShapeDtypeStruct((g, ch, 2 * tc * ch), F32),
                   jax.ShapeDtypeStruct((g, tc * ch, 4 * ns), BF16),
                   jax.ShapeDtypeStruct((g, 8 * ns, tc * ch), BF16),
                   jax.ShapeDtypeStruct((g, 2, 2 * ns), F32)],
        compiler_params=_cparams("arbitrary"),
        name="s5_prepare",
    )(lrow, lcol, bt, ct)


def _s5_core_body(nb, nk, nkc, x_ref, kk_ref, min_ref, mout_ref, arec_ref, y_ref, trow, yacc, s_ref, h_ref):
    tc, ch, ns = S5_CHUNK, SSM_CH_PER_GROUP, SSM_STATE
    ns2, kw = 2 * ns, 2 * tc
    n_pairs = ch // 2

    def plane_pair(i):
        return jnp.concatenate([x_ref[2 * i], x_ref[2 * i + 1]], axis=1)

    def add_state(i, acc):
        rows = pl.ds(pl.multiple_of(i * kw, kw), kw)
        return acc + jnp.dot(plane_pair(i), min_ref[0, rows, :], preferred_element_type=F32)
    contrib = lax.fori_loop(0, n_pairs, add_state, jnp.zeros((nb * nk, 2 * ns2), F32))
    s_ref[0] = contrib[:, :ns2]
    s_ref[1] = contrib[:, ns2:]

    arec = arec_ref[0]
    a_re, a_im = arec[0:1], arec[1:2]
    is_fwd = _iota((nb, ns2), 1) < ns

    def step(i, carry):
        h_re, h_im = carry
        kr = jnp.where(i < nkc, nkc - 1 - i, nk - 1 - (i - nkc))
        fwd_rows = pl.ds(i, nb, stride=nk)
        rev_rows = pl.ds(kr, nb, stride=nk)
        h_ref[0, 0, fwd_rows, :] = h_re
        h_ref[0, 1, fwd_rows, :] = h_im
        h_ref[1, 0, rev_rows, :] = h_re
        h_ref[1, 1, rev_rows, :] = h_im
        s_re = jnp.where(is_fwd, s_ref[0, fwd_rows, :], s_ref[0, rev_rows, :])
        s_im = jnp.where(is_fwd, s_ref[1, fwd_rows, :], s_ref[1, rev_rows, :])
        return a_re * h_re - a_im * h_im + s_re, a_re * h_im + a_im * h_re + s_im

    zero = jnp.zeros((nb, ns2), F32)
    lax.fori_loop(0, nk, step, (zero, zero))

    y0 = jnp.zeros((nb * nk, ch * tc), F32)
    for d in range(2):
        hd = jnp.concatenate([h_ref[d, 0], h_ref[d, 1]], axis=1).astype(BF16)
        y0 = y0 + jnp.dot(hd, mout_ref[0, 2 * ns2 * d:2 * ns2 * (d + 1), :], preferred_element_type=F32)
    yacc[...] = y0

    def add_pair(i, carry):
        for cc in range(2):
            lag_row = kk_ref[0, pl.ds(2 * i + cc, 1), :]
            for c in range(ch):
                lag = jnp.broadcast_to(lag_row[:, kw * c:kw * (c + 1)], (tc, kw))
                block = pltpu.roll(lag, tc + 1, 1, stride=1, stride_axis=0)[:, :tc]
                trow[tc * cc:tc * (cc + 1), tc * c:tc * (c + 1)] = block.astype(BF16)
        yacc[...] += jnp.dot(plane_pair(i), trow[...], preferred_element_type=F32)
        return carry
    lax.fori_loop(0, n_pairs, add_pair, 0)
    for c in range(ch):
        y_ref[c] = yacc[:, tc * c:tc * (c + 1)].astype(y_ref.dtype)


def _s5_core(lay, u, mats):
    kk, m_in, m_out, arec = mats
    g, ch, tc, ns = SSM_GROUPS, SSM_CH_PER_GROUP, S5_CHUNK, SSM_STATE
    b = lay.batch
    assert lay.seq % tc == 0 and lay.ctx_len % tc == 0
    nkl, nkc = lay.seq // tc, lay.ctx_len // tc
    nk = nkl + nkc
    rows = b * nk
    lat = u[:lay.n_lat].reshape(b, nkl, tc, D_MODEL)
    ctx = u[lay.n_lat:].reshape(b, nkc, tc, D_MODEL)
    x = jnp.concatenate([ctx, lat], axis=1).transpose(3, 0, 1, 2).reshape(D_MODEL, rows, tc)
    blk = lambda *s: pl.BlockSpec((1,) + s, lambda i: (i,) + (0,) * len(s))
    planes = pl.BlockSpec((ch, rows, tc), lambda i: (i, 0, 0))
    y = pl.pallas_call(
        functools.partial(_s5_core_body, b, nk, nkc),
        grid=(g,),
        in_specs=[planes, blk(ch, 2 * tc * ch), blk(tc * ch, 4 * ns), blk(8 * ns, tc * ch), blk(2, 2 * ns)],
        out_specs=planes,
        out_shape=jax.ShapeDtypeStruct((D_MODEL, rows, tc), BF16),
        scratch_shapes=[pltpu.VMEM((2 * tc, ch * tc), BF16), pltpu.VMEM((rows, ch * tc), F32),
                        pltpu.VMEM((2, rows, 2 * ns), F32), pltpu.VMEM((2, 2, rows, 2 * ns), F32)],
        compiler_params=_cparams("arbitrary"),
        name="s5_scan",
    )(x, kk, m_in, m_out, arec)
    y = y.reshape(D_MODEL, b, nk, tc).transpose(1, 2, 3, 0)
    y_ctx = y[:, :nkc].reshape(lay.n_ctx, D_MODEL)
    y_lat = y[:, nkc:].reshape(lay.n_lat, D_MODEL)
    return jnp.concatenate([y_lat, y_ctx], axis=0)


def _s5_out_body(y_ref, u_ref, d_ref, wv_ref, wg_ref, *rest):
    u = u_ref[...].astype(F32)
    act = _gelu_tanh(y_ref[...].astype(F32) + d_ref[...] * u).astype(BF16)
    val = jnp.dot(act, wv_ref[...], preferred_element_type=F32)
    gate = jnp.dot(act, wg_ref[...], preferred_element_type=F32)
    _finish_route(val * jax.nn.sigmoid(gate), *rest)


def _s5_out(lay, n_tiles, layer, y, u, d, wv, wg, h, mods, lng, lnb, wr, br):
    shapes, specs = _route_outputs(n_tiles * TOK_TILE)
    return pl.pallas_call(
        _s5_out_body,
        grid=(n_tiles,),
        in_specs=[_row_spec(), _row_spec(), _const_spec((1, D_MODEL)), _const_spec((D_MODEL, D_MODEL)),
                  _const_spec((D_MODEL, D_MODEL))] + _route_inputs(lay, layer),
        out_specs=specs,
        out_shape=shapes,
        compiler_params=_cparams("arbitrary"),
        name="s5_out_route",
    )(y, u, d, wv, wg, h, mods, lng, lnb, wr, br)


def kernel(x, c, ctx, c_ctx, ada_w, ada_b, ln_g, ln_b, s5_lam_re, s5_lam_im, s5_log_dt, s5_b_re, s5_b_im, s5_c_re, s5_c_im, s5_d, s5_w_glu, cv_w_pw1, cv_b_pw1, cv_w_dw, cv_b_dw, cv_ln_g, cv_ln_b, cv_w_pw2, cv_b_pw2, at_w_qkv, at_w_o, at_sink, moe_wg, moe_bg, moe_we, moe_be, moe_w1, moe_w2):
    batch, seq, d = x.shape
    lay = _Layout(batch, seq, ctx.shape[1])
    depth = ada_w.shape[0]
    row = lambda a: a.reshape(1, -1)

    h = jnp.concatenate([x.reshape(lay.n_lat, d), ctx.reshape(lay.n_ctx, d)], axis=0)
    c_all = jnp.concatenate([c, c_ctx[None], jnp.zeros((lay.mod_rows - batch - 1, d), F32)], axis=0)
    mods = _modulation(c_all, ada_w, ada_b)
    cos_t, sin_t = _rope_tables(seq)
    ng, ne = N_EXPERT_GROUPS, N_EXPERTS
    pad = jnp.zeros((d, ROUTE_LANES - ng - ne), F32)

    u = _modulate(lay, h, mods, 0)
    for i in range(depth):
        last = i == depth - 1
        kind, j = i % 3, i // 3
        n_tiles = lay.lat_tiles if last else lay.tiles
        wr = _split_bf16(jnp.concatenate([moe_wg[i], moe_we[i], pad], axis=1))
        br =jnp.concatenate([moe_bg[i], moe_be[i], pad[0]], axis=0)[None]
        route_args = (h, mods, row(ln_g[i, 0]), row(ln_b[i, 0]), wr, br)
        if kind == 0:
            mats = _s5_prepare(s5_lam_re[j], s5_lam_im[j], s5_log_dt[j], s5_b_re[j], s5_b_im[j],
                               s5_c_re[j], s5_c_im[j])
            y = _s5_core(lay, u, mats)
            wglu = s5_w_glu[j].astype(BF16)
            h1, v, meta, cnt = _s5_out(lay, n_tiles, i, y, u, row(s5_d[j]), wglu[:, :d], wglu[:, d:], *route_args)
        elif kind == 1:
            w1 = cv_w_pw1[j].astype(BF16)
            z = _conv_pw1(lay, u, w1[:, :d], w1[:, d:], row(cv_b_pw1[j, :d]), row(cv_b_pw1[j, d:]))
            h1, v, meta, cnt = _conv_out(lay, n_tiles, i, z, cv_w_dw[j], row(cv_b_dw[j]), row(cv_ln_g[j]),
                                         row(cv_ln_b[j]), cv_w_pw2[j].astype(BF16), row(cv_b_pw2[j]),
                                         *route_args)
        else:
            q, k, vv = _qkv_rope(lay, u, at_w_qkv[j].astype(BF16), cos_t, sin_t)
            o = _attention(lay, q, k, vv, row(at_sink[j]), not last)
            h1, v, meta, cnt = _attn_out(lay, n_tiles, i, o, at_w_o[j].astype(BF16), *route_args)
        plan = _moe_plan(meta, cnt, n_tiles)
        xs = _moe_dispatch(v, plan)
        ys = _moe_experts(xs, moe_w1[i], moe_w2[i], plan.items)
        h, u = _moe_combine(lay, n_tiles, ys, plan, h1, mods, i, row(ln_g[i, 1]), row(ln_b[i, 1]), not last)
    return h.reshape(batch, seq, d)
```

```python
import functools
import math

import jax
import jax.numpy as jnp
from jax import lax
from jax.experimental import pallas as pl
from jax.experimental.pallas import tpu as pltpu

F32 = jnp.float32
BF16 = jnp.bfloat16
HIGHEST = lax.Precision.HIGHEST

D_MODEL = 1024
DEPTH = 4
GRID_W = 64
SSM_CH_PER_GROUP = 16
SSM_GROUPS = D_MODEL // SSM_CH_PER_GROUP
SSM_STATE = 64
CONV_WIDTH = 31
HEAD_DIM = 64
N_HEADS = D_MODEL // HEAD_DIM
N_KV_HEADS = N_HEADS // 4
KV_REP = N_HEADS // N_KV_HEADS
WINDOW = 128
ATT_BLOCK = 128
ROPE_BASE = 10000.0
N_EXPERT_GROUPS = 4
EXPERTS_PER_GROUP = 8
N_EXPERTS = N_EXPERT_GROUPS * EXPERTS_PER_GROUP
D_EXPERT = D_MODEL // 2
ALPHA = (2 * DEPTH) ** 0.25
LN_EPS = 1e-5

SUBLANES = 8
LANES = 128

TOK_TILE = 256
EXPERT_TILE = 512
S5_CHUNK = LANES
CONV_HALO = 16
ROUTE_LANES = LANES
META_COLS = 8
RUN_PAD = SUBLANES
LOCAL_ROWS = -(-(2 * TOK_TILE + N_EXPERTS * (RUN_PAD - 1)) // LANES) * LANES
MAX_PIECES = LOCAL_ROWS // RUN_PAD
VMEM_LIMIT = 56 * 1024 * 1024


def _cparams(*sem):
    return pltpu.CompilerParams(dimension_semantics=sem, vmem_limit_bytes=VMEM_LIMIT)


def _round_up(n, m):
    return (n + m - 1) // m * m


def _split_bf16(w):
    hi = w.astype(BF16)
    return jnp.stack([hi, (w - hi.astype(F32)).astype(BF16)])


def _mod_body(c_ref, w_ref, b_ref, o_ref):
    c = c_ref[...]
    s = c * jax.nn.sigmoid(c)
    o_ref[0] = jnp.dot(s, w_ref[0], preferred_element_type=F32, precision=HIGHEST) + b_ref[0]


def _modulation(c_all, ada_w, ada_b):
    depth, d, n = ada_w.shape
    r = c_all.shape[0]
    tn = 1024
    out = pl.pallas_call(
        _mod_body,
        grid=(depth, n // tn),
        in_specs=[
            pl.BlockSpec((r, d), lambda i, j: (0, 0)),
            pl.BlockSpec((1, d, tn), lambda i, j: (i, 0, j)),
            pl.BlockSpec((1, 1, tn), lambda i, j: (i, 0, j)),
        ],
        out_specs=pl.BlockSpec((1, r, tn), lambda i, j: (i, 0, j)),
        out_shape=jax.ShapeDtypeStruct((depth, r, n), F32),
        compiler_params=_cparams("arbitrary", "arbitrary"),
        name="adaln_modulation",
    )(c_all, ada_w, ada_b.reshape(depth, 1, n))
    return out.reshape(depth, r, 6, d)


def _layer_norm_rows(t, g, b):
    mu = jnp.mean(t, axis=-1, keepdims=True)
    dev = t - mu
    var = jnp.mean(dev * dev, axis=-1, keepdims=True)
    return dev * lax.rsqrt(var + LN_EPS) * g + b


def _post_norm_and_route(h, y, mod, lng, lnb, wr, br):
    tm = h.shape[0]
    ng, ne = N_EXPERT_GROUPS, N_EXPERTS
    h1 = _layer_norm_rows(ALPHA * h + mod[2:3] * y, lng, lnb)
    v = h1 * (1.0 + mod[4:5]) + mod[3:4]
    v_hi = v.astype(BF16)
    v_lo = (v - v_hi.astype(F32)).astype(BF16)
    logits = (jnp.dot(v_hi, wr[0], preferred_element_type=F32) + jnp.dot(v_lo, wr[0], preferred_element_type=F32)
              + jnp.dot(v_hi, wr[1], preferred_element_type=F32) + br)

    lane = lax.broadcasted_iota(jnp.int32, (tm, ROUTE_LANES), 1)
    lane_f = lane.astype(F32)
    neg = -jnp.inf
    no_lane = float(ROUTE_LANES)
    is_group = lane < ng
    gl = jnp.where(is_group, logits, neg)
    gmax = jnp.max(gl, axis=-1, keepdims=True)
    gsum = jnp.sum(jnp.where(is_group, jnp.exp(logits - gmax), 0.0), axis=-1, keepdims=True)
    p_group = 1.0 / gsum
    g_idx = jnp.min(jnp.where(gl == gmax, lane_f, no_lane), axis=-1, keepdims=True)
    expert_group = ((lane - ng) // EXPERTS_PER_GROUP).astype(F32)
    in_group = (lane >= ng) & (lane < ng + ne) & (expert_group == g_idx)
    el = jnp.where(in_group, logits, neg)
    v1 = jnp.max(el, axis=-1, keepdims=True)
    i1 = jnp.min(jnp.where(el == v1, lane_f, no_lane), axis=-1, keepdims=True)
    el2 = jnp.where(lane_f == i1, neg, el)
    v2 = jnp.max(el2, axis=-1, keepdims=True)
    i2 = jnp.min(jnp.where(el2 == v2, lane_f, no_lane), axis=-1, keepdims=True)
    e21 = jnp.exp(v2 - v1)
    w1 = p_group / (1.0 + e21)
    w2 = p_group * e21 / (1.0 + e21)

    hit1 = lane_f == i1
    hit2 = lane_f == i2
    one1 = hit1.astype(BF16)
    one2 = hit2.astype(BF16)
    rr = lax.broadcasted_iota(jnp.int32, (tm, tm), 0)
    cc = lax.broadcasted_iota(jnp.int32, (tm, tm), 1)
    before = (cc < rr).astype(BF16)
    cum1 = jnp.dot(before, one1, preferred_element_type=F32)
    cum2 = jnp.dot(before, one2, preferred_element_type=F32)
    tot1 = jnp.sum(one1.astype(F32), axis=0, keepdims=True)
    tot2 = jnp.sum(one2.astype(F32), axis=0, keepdims=True)
    rank1 = jnp.sum(jnp.where(hit1, cum1, 0.0), axis=-1, keepdims=True)
    rank2 = jnp.sum(jnp.where(hit2, tot1 + cum2, 0.0), axis=-1, keepdims=True)

    col = lax.broadcasted_iota(jnp.int32, (tm, META_COLS), 1)
    meta = jnp.where(col == 0, i1 - ng,
           jnp.where(col == 1, i2 - ng,
           jnp.where(col == 2, rank1,
           jnp.where(col == 3, rank2,
           jnp.where(col == 4, w1,
           jnp.where(col == 5, w2, 0.0))))))
    return h1, v, meta, tot1 + tot2


def _gelu_tanh(x):
    return 0.5 * x * (1.0 + jnp.tanh(math.sqrt(2.0 / math.pi) * (x + 0.044715 * (x * x * x))))


class _Layout:
    def __init__(self, batch, seq, ctx_len):
        self.batch, self.seq, self.ctx_len = batch, seq, ctx_len
        self.n_lat = batch * seq
        self.n_ctx = batch * ctx_len
        self.n_tok = self.n_lat + self.n_ctx
        assert seq % TOK_TILE == 0 and ctx_len % TOK_TILE == 0
        self.lat_tiles = self.n_lat // TOK_TILE
        self.tiles = self.n_tok // TOK_TILE
        self.tiles_per_seq = seq // TOK_TILE
        self.tiles_per_ctx = ctx_len // TOK_TILE
        self.mod_rows = _round_up(batch + 1, SUBLANES)

    def mod_row(self, i):
        return jnp.minimum(i // self.tiles_per_seq, self.batch)

    def s5_tile(self, i):
        per_batch = self.tiles_per_seq + self.tiles_per_ctx
        lat = (i // self.tiles_per_seq) * per_batch + self.tiles_per_ctx + i % self.tiles_per_seq
        c = i - self.lat_tiles
        ctx = (c // self.tiles_per_ctx) * per_batch + c % self.tiles_per_ctx
        return jnp.where(i < self.lat_tiles, lat, ctx)

    def s5_row_spec(self):
        return pl.BlockSpec((TOK_TILE, D_MODEL), lambda i, *_: (self.s5_tile(i), 0))


def _mod_spec(lay, layer):
    return pl.BlockSpec((1, 1, 6, D_MODEL), lambda i, *_: (layer, lay.mod_row(i), 0, 0))


def _row_spec(tm=TOK_TILE, d=D_MODEL):
    return pl.BlockSpec((tm, d), lambda i, *_: (i, 0))


def _const_spec(shape):
    nd = len(shape)
    return pl.BlockSpec(shape, lambda i, *_: (0,) * nd)


def _modulate_body(lat_tiles, x_ref, ctx_ref, mod_ref, h_ref, u_ref):
    mod = mod_ref[0, 0]
    h = jnp.where(pl.program_id(0) < lat_tiles, x_ref[...], ctx_ref[...])
    h_ref[...] = h
    u_ref[...] = (h * (1.0 + mod[1:2]) + mod[0:1]).astype(u_ref.dtype)


def _modulate(lay, x, ctx, mods, layer, u_spec):
    ctx_tiles = lay.tiles - lay.lat_tiles
    return pl.pallas_call(
        functools.partial(_modulate_body, lay.lat_tiles),
        grid=(lay.tiles,),
        in_specs=[pl.BlockSpec((TOK_TILE, D_MODEL), lambda i: (jnp.minimum(i, lay.lat_tiles - 1), 0)),
                  pl.BlockSpec((TOK_TILE, D_MODEL), lambda i: (jnp.clip(i - lay.lat_tiles, 0, ctx_tiles - 1), 0)),
                  _mod_spec(lay, layer)],
        out_specs=[_row_spec(), u_spec],
        out_shape=[jax.ShapeDtypeStruct((lay.n_tok, D_MODEL), F32),
                   jax.ShapeDtypeStruct((lay.n_tok, D_MODEL), BF16)],
        compiler_params=_cparams("arbitrary"),
        name="input_modulate",
    )(x, ctx, mods)


PACKED = D_MODEL // 2


def _pack_rows(x):
    half = x.shape[1] // 2
    hi = lax.bitcast_convert_type(x[:, :half], jnp.uint32)
    lo = lax.bitcast_convert_type(x[:, half:], jnp.uint32)
    return hi | (lo >> 16)


def _unpack_rows(w):
    hi = lax.bitcast_convert_type(w & jnp.uint32(0xFFFF0000), F32)
    lo = lax.bitcast_convert_type(w << 16, F32)
    return hi.astype(BF16), lo.astype(BF16)


def _piece_spec(n_tiles, shift=0):
    return pl.BlockSpec((1, 1, MAX_PIECES), lambda i, *_: (jnp.minimum(i + shift, n_tiles - 1), 0, 0),
                        memory_space=pltpu.SMEM)


def _dispatch_body(np_ref, tail_ref, dst_ref, lp_ref, v_ref, xs_ref, loc, zeros, sem):
    i = pl.program_id(0)
    lp = lp_ref[0]
    slot_row = lax.broadcasted_iota(jnp.int32, (LOCAL_ROWS, v_ref.shape[0]), 0).astype(F32)
    pick = ((slot_row == lp[0:1]) | (slot_row == lp[1:2])).astype(BF16)
    loc[...] = _pack_rows(jnp.dot(pick, v_ref[...], preferred_element_type=F32))

    def piece(src, dst):
        return pltpu.make_async_copy(src, xs_ref.at[pl.ds(pl.multiple_of(dst, RUN_PAD), RUN_PAD)], sem)

    def issue(q, carry):
        piece(loc.at[pl.ds(pl.multiple_of(q * RUN_PAD, RUN_PAD), RUN_PAD)], dst_ref[0, 0, q]).start()
        return carry

    def drain(q, carry):
        piece(loc.at[pl.ds(0, RUN_PAD)], 0).wait()
        return carry

    n = np_ref[i]
    lax.fori_loop(0, n, issue, 0)
    lax.fori_loop(0, n, drain, 0)

    @pl.when(i == pl.num_programs(0) - 1)
    def _():
        zeros[...] = jnp.zeros_like(zeros)

        def issue_zero(q, carry):
            piece(zeros, tail_ref[0] + q * RUN_PAD).start()
            return carry

        def drain_zero(q, carry):
            piece(zeros, 0).wait()
            return carry

        lax.fori_loop(0, tail_ref[1], issue_zero, 0)
        lax.fori_loop(0, tail_ref[1], drain_zero, 0)


def _moe_dispatch(v, plan):
    n_tiles = plan.n_pieces.shape[0]
    grid_spec = pltpu.PrefetchScalarGridSpec(
        num_scalar_prefetch=2,
        grid=(n_tiles,),
        in_specs=[_piece_spec(n_tiles),
                  pl.BlockSpec((1, 2, TOK_TILE), lambda i, *_: (i, 0, 0)),
                  _row_spec()],
        out_specs=pl.BlockSpec(memory_space=pl.ANY),
        scratch_shapes=[pltpu.VMEM((LOCAL_ROWS, PACKED), jnp.uint32), pltpu.VMEM((RUN_PAD, PACKED), jnp.uint32),
                        pltpu.SemaphoreType.DMA(())],
    )
    return pl.pallas_call(
        _dispatch_body,
        grid_spec=grid_spec,
        out_shape=jax.ShapeDtypeStruct((plan.max_rows, PACKED), jnp.uint32),
        compiler_params=_cparams("arbitrary"),
        name="moe_dispatch",
    )(plan.n_pieces, plan.tail, plan.dst, plan.lp_rows, v)


def _expert_body(tile_ref, exp_ref, lo_ref, hi_ref, xs_ref, w1_ref, w2_ref, ys_ref, w1b, w2b):
    j = pl.program_id(0)
    jp = jnp.maximum(j - 1, 0)
    new_expert = (j == 0) | (exp_ref[j] != exp_ref[jp])
    first_visit = (j == 0) | (tile_ref[j] != tile_ref[jp])
    lo, hi = lo_ref[j], hi_ref[j]

    @pl.when(new_expert)
    def _():
        w1b[...] = w1_ref[0].astype(BF16)
        w2b[...] = w2_ref[0].astype(BF16)

    def expert_rows():
        x_a, x_b = _unpack_rows(xs_ref[...])
        gu = (jnp.dot(x_a, w1b[:PACKED], preferred_element_type=F32)
              + jnp.dot(x_b, w1b[PACKED:], preferred_element_type=F32))
        gate, up = gu[:, :D_EXPERT], gu[:, D_EXPERT:]
        a = (gate * jax.nn.sigmoid(gate) * up).astype(BF16)
        y = jnp.dot(a, w2b[...], preferred_element_type=F32)
        rows = lax.broadcasted_iota(jnp.int32, (xs_ref.shape[0], 1), 0)
        return _pack_rows(y.astype(BF16).astype(F32)), (rows >= lo) & (rows < hi)

    @pl.when(first_visit)
    def _():
        y, mine = expert_rows()
        ys_ref[...] = jnp.where(mine, y, jnp.uint32(0))

    @pl.when(jnp.logical_not(first_visit) & (hi > lo))
    def _():
        y, mine = expert_rows()
        ys_ref[...] = jnp.where(mine, y, ys_ref[...])


def _moe_experts(xs, w1, w2, items):
    tile_j, exp_j, lo_j, hi_j = items
    n_items = tile_j.shape[0]
    grid_spec = pltpu.PrefetchScalarGridSpec(
        num_scalar_prefetch=4,
        grid=(n_items,),
        in_specs=[
            pl.BlockSpec((EXPERT_TILE, PACKED), lambda j, t, e, lo, hi: (t[j], 0)),
            pl.BlockSpec((1, D_MODEL, 2 * D_EXPERT), lambda j, t, e, lo, hi: (e[j], 0, 0)),
            pl.BlockSpec((1, D_EXPERT, D_MODEL), lambda j, t, e, lo, hi: (e[j], 0, 0)),
        ],
        out_specs=pl.BlockSpec((EXPERT_TILE, PACKED), lambda j, t, e, lo, hi: (t[j], 0)),
        scratch_shapes=[pltpu.VMEM((D_MODEL, 2 * D_EXPERT), BF16), pltpu.VMEM((D_EXPERT, D_MODEL), BF16)],
    )
    return pl.pallas_call(
        _expert_body,
        grid_spec=grid_spec,
        out_shape=jax.ShapeDtypeStruct(xs.shape, xs.dtype),
        compiler_params=_cparams("arbitrary"),
        name="moe_experts",
    )(tile_j, exp_j, lo_j, hi_j, xs, w1, w2)


def _combine_body(has_next, np_ref, src_ref, srcn_ref, lpw_ref, h1_ref, mod_ref, modn_ref, lng_ref, lnb_ref,
                  ys_ref, *rest):
    if has_next:
        h2_ref, u_ref, buf, sem = rest
    else:
        h2_ref, buf, sem = rest
    tm = h1_ref.shape[0]
    i = pl.program_id(0)
    n = pl.num_programs(0)
    slot = i % 2

    def piece(src, s, q):
        return pltpu.make_async_copy(ys_ref.at[pl.ds(pl.multiple_of(src, RUN_PAD), RUN_PAD)],
                                     buf.at[s, pl.ds(pl.multiple_of(q * RUN_PAD, RUN_PAD), RUN_PAD)], sem.at[s])

    def gather(sref, s, count):
        def issue(q, carry):
            piece(sref[0, 0, q], s, q).start()
            return carry
        lax.fori_loop(0, count, issue, 0)

    @pl.when(i == 0)
    def _():
        buf[...] = jnp.zeros_like(buf)
        gather(src_ref, 0, np_ref[0])

    @pl.when(i + 1 < n)
    def _():
        gather(srcn_ref, 1 - slot, np_ref[jnp.minimum(i + 1, n - 1)])

    def drain(q, carry):
        piece(0, slot, 0).wait()
        return carry
    lax.fori_loop(0, np_ref[i], drain, 0)

    lpw = lpw_ref[...]
    lane = lax.broadcasted_iota(jnp.int32, (tm, LOCAL_ROWS), 1).astype(F32)
    wmat = jnp.where(lane == lpw[:, 0:1], lpw[:, 2:3], 0.0) + jnp.where(lane == lpw[:, 1:2], lpw[:, 3:4], 0.0)
    w_hi = wmat.astype(BF16)
    w_lo = (wmat - w_hi.astype(F32)).astype(BF16)
    f = jnp.concatenate(
        [jnp.dot(w_hi, yb, preferred_element_type=F32) + jnp.dot(w_lo, yb, preferred_element_type=F32)
         for yb in _unpack_rows(buf[slot])], axis=1)
    mod = mod_ref[0, 0]
    h2 = _layer_norm_rows(ALPHA * h1_ref[...] + mod[5:6] * f, lng_ref[...], lnb_ref[...])
    h2_ref[...] = h2
    if has_next:
        modn = modn_ref[0, 0]
        u_ref[...] = (h2 * (1.0 + modn[1:2]) + modn[0:1]).astype(u_ref.dtype)


def _moe_combine(lay, n_tiles, ys, plan, h1, mods, layer, lng, lnb, u_spec):
    has_next = u_spec is not None
    n_rows = n_tiles * TOK_TILE
    nxt = min(layer + 1, DEPTH - 1)
    out_shape = [jax.ShapeDtypeStruct((n_rows, D_MODEL), F32)]
    out_specs = [_row_spec()]
    if has_next:
        out_shape.append(jax.ShapeDtypeStruct((n_rows, D_MODEL), BF16))
        out_specs.append(u_spec)
    grid_spec = pltpu.PrefetchScalarGridSpec(
        num_scalar_prefetch=1,
        grid=(n_tiles,),
        in_specs=[
            _piece_spec(n_tiles), _piece_spec(n_tiles, 1),
            _row_spec(d=4), _row_spec(), _mod_spec(lay, layer), _mod_spec(lay, nxt),
            _const_spec((1, D_MODEL)), _const_spec((1, D_MODEL)),
            pl.BlockSpec(memory_space=pl.ANY),
        ],
        out_specs=out_specs,
        scratch_shapes=[pltpu.VMEM((2, LOCAL_ROWS, PACKED), jnp.uint32), pltpu.SemaphoreType.DMA((2,))],
    )
    outs = pl.pallas_call(
        functools.partial(_combine_body, has_next),
        grid_spec=grid_spec,
        out_shape=out_shape,
        compiler_params=_cparams("arbitrary"),
        name="moe_combine",
    )(plan.n_pieces, plan.dst, plan.dst, plan.lp_w, h1, mods, mods, lng, lnb, ys)
    return outs if has_next else (outs[0], None)


class _MoePlan:
    pass


def _exclusive_cumsum(a, axis):
    return jnp.cumsum(a, axis=axis) - a


def _moe_plan(meta, tile_counts, n_tiles):
    plan = _MoePlan()
    experts = jnp.arange(N_EXPERTS, dtype=jnp.int32)
    plan.max_rows = _round_up(n_tiles * (2 * TOK_TILE + N_EXPERTS * (RUN_PAD - 1)), EXPERT_TILE)
    cnt = tile_counts[:, 0, N_EXPERT_GROUPS:N_EXPERT_GROUPS + N_EXPERTS].astype(jnp.int32)
    run = (cnt + RUN_PAD - 1) // RUN_PAD * RUN_PAD
    local_start = _exclusive_cumsum(run, 1)
    counts = jnp.sum(run, axis=0)
    total = jnp.sum(counts)
    tail = (-total) % EXPERT_TILE
    counts = counts.at[N_EXPERTS - 1].add(tail)
    ends = jnp.cumsum(counts)
    starts = ends - counts
    run_start = starts[None, :] + _exclusive_cumsum(run, 0)
    plan.tail = jnp.stack([total, tail // RUN_PAD]).astype(jnp.int32)

    eid = meta[:, 0:2].astype(jnp.int32).reshape(n_tiles, TOK_TILE, 2)
    onehot = eid[..., None] == experts
    lp = meta[:, 2:4].reshape(n_tiles, TOK_TILE, 2) + jnp.sum(
        jnp.where(onehot, local_start[:, None, None, :], 0), axis=-1).astype(F32)
    plan.lp_rows = lp.transpose(0, 2, 1)
    plan.lp_w = jnp.concatenate([lp.reshape(-1, 2), meta[:, 4:6]], axis=1)

    pieces = run // RUN_PAD
    piece_end = jnp.cumsum(pieces, axis=1)
    plan.n_pieces = piece_end[:, -1].astype(jnp.int32)
    q = jnp.arange(MAX_PIECES, dtype=jnp.int32)
    owner = jnp.minimum(jnp.sum(q[None, :, None] >= piece_end[:, None, :], axis=-1), N_EXPERTS - 1)
    pick = owner[..., None] == experts
    first = jnp.sum(jnp.where(pick, (piece_end - pieces)[:, None, :], 0), axis=-1)
    base = jnp.sum(jnp.where(pick, run_start[:, None, :], 0), axis=-1)
    dst = jnp.where(q[None, :] < plan.n_pieces[:, None], base + (q[None, :] - first) * RUN_PAD, 0)
    plan.dst = dst.astype(jnp.int32).reshape(n_tiles, 1, MAX_PIECES)

    n_etiles = plan.max_rows // EXPERT_TILE
    first_tile = starts // EXPERT_TILE
    last_tile = jnp.maximum(ends - 1, 0) // EXPERT_TILE
    n_items_e = jnp.where(counts > 0, last_tile - first_tile + 1, 0)
    item_end = jnp.cumsum(n_items_e)
    item_start = item_end - n_items_e
    n_items = n_etiles + N_EXPERTS
    j = jnp.arange(n_items, dtype=jnp.int32)
    e_j = jnp.minimum(jnp.sum(j[:, None] >= item_end[None, :], axis=1), N_EXPERTS - 1).astype(jnp.int32)
    active = j < item_end[-1]
    tile_j = jnp.take(first_tile, e_j) + (j - jnp.take(item_start, e_j))
    lo = jnp.maximum(jnp.take(starts, e_j), tile_j * EXPERT_TILE) - tile_j * EXPERT_TILE
    hi = jnp.minimum(jnp.take(ends, e_j), (tile_j + 1) * EXPERT_TILE) - tile_j * EXPERT_TILE
    last = jnp.maximum(item_end[-1] - 1, 0)
    tile_j = jnp.where(active, tile_j, tile_j[last]).astype(jnp.int32)
    e_j = jnp.where(active, e_j, e_j[last]).astype(jnp.int32)
    lo = jnp.where(active, lo, 0).astype(jnp.int32)
    hi = jnp.where(active, hi, 0).astype(jnp.int32)
    plan.items = (tile_j, e_j, lo, hi)
    return plan


def _route_outputs(n_rows):
    n_tiles = n_rows // TOK_TILE
    shapes = [jax.ShapeDtypeStruct((n_rows, D_MODEL), F32),
              jax.ShapeDtypeStruct((n_rows, D_MODEL), BF16),
              jax.ShapeDtypeStruct((n_rows, META_COLS), F32),
              jax.ShapeDtypeStruct((n_tiles, 1, ROUTE_LANES), F32)]
    specs = [_row_spec(), _row_spec(), _row_spec(d=META_COLS),
             pl.BlockSpec((1, 1, ROUTE_LANES), lambda i: (i, 0, 0))]
    return shapes, specs


def _route_inputs(lay, layer):
    return [_row_spec(), _mod_spec(lay, layer), _const_spec((1, D_MODEL)), _const_spec((1, D_MODEL)),
            _const_spec((2, D_MODEL, ROUTE_LANES)), _const_spec((1, ROUTE_LANES))]


def _finish_route(y, h_ref, mod_ref, lng_ref, lnb_ref, wr_ref, br_ref, h1_ref, v_ref, meta_ref, cnt_ref):
    h1, v, meta, counts = _post_norm_and_route(h_ref[...], y, mod_ref[0, 0], lng_ref[...], lnb_ref[...],
                                               wr_ref[...], br_ref[...])
    h1_ref[...] = h1
    v_ref[...] = v.astype(v_ref.dtype)
    meta_ref[...] = meta
    cnt_ref[0] = counts


def _attn_out_body(o_ref, wo_ref, *rest):
    y = jnp.dot(o_ref[...], wo_ref[...], preferred_element_type=F32)
    _finish_route(y, *rest)


def _attn_out(lay, n_tiles, layer, o, w_o, h, mods, lng, lnb, wr, br):
    shapes, specs = _route_outputs(n_tiles * TOK_TILE)
    return pl.pallas_call(
        _attn_out_body,
        grid=(n_tiles,),
        in_specs=[_row_spec(), _const_spec((D_MODEL, D_MODEL))] + _route_inputs(lay, layer),
        out_specs=specs,
        out_shape=shapes,
        compiler_params=_cparams("arbitrary"),
        name="attn_out_route",
    )(o, w_o, h, mods, lng, lnb, wr, br)


def _rope_tables(seq):
    quarter = HEAD_DIM // 4
    inv_freq = ROPE_BASE ** (-jnp.arange(quarter, dtype=F32) / quarter)
    t = jnp.arange(seq)
    rows = (t // GRID_W).astype(F32)
    cols = (t % GRID_W).astype(F32)
    ang_r = rows[:, None] * inv_freq
    ang_c = cols[:, None] * inv_freq
    cos_h = jnp.concatenate([jnp.cos(ang_r), jnp.cos(ang_r), jnp.cos(ang_c), jnp.cos(ang_c)], axis=-1)
    sin_h = jnp.concatenate([-jnp.sin(ang_r), jnp.sin(ang_r), -jnp.sin(ang_c), jnp.sin(ang_c)], axis=-1)
    rep = LANES // HEAD_DIM
    return jnp.tile(cos_h, (1, rep)), jnp.tile(sin_h, (1, rep))


def _rope(x, cos, sin):
    quarter = HEAD_DIM // 4
    width = x.shape[1]
    n = width // LANES
    c = jnp.tile(cos, (1, n))
    s = jnp.tile(sin, (1, n))
    lane = lax.broadcasted_iota(jnp.int32, x.shape, 1)
    is_lo = (lane % (2 * quarter)) < quarter
    partner = jnp.where(is_lo, pltpu.roll(x, width - quarter, 1), pltpu.roll(x, quarter, 1))
    return x * c + partner * s


def _qkv_body(lat_tiles, u_ref, w_ref, cos_ref, sin_ref, q_ref, k_ref, v_ref):
    is_ctx = pl.program_id(0) >= lat_tiles
    qkv = jnp.dot(u_ref[...], w_ref[...], preferred_element_type=F32)
    nq, nkv = N_HEADS * HEAD_DIM, N_KV_HEADS * HEAD_DIM
    q, k, v = qkv[:, :nq], qkv[:, nq:nq + nkv], qkv[:, nq + nkv:]
    cos, sin = cos_ref[...], sin_ref[...]
    q = jnp.where(is_ctx, q, _rope(q, cos, sin))
    k = jnp.where(is_ctx, k, _rope(k, cos, sin))
    q_ref[...] = (q * HEAD_DIM ** -0.5).astype(q_ref.dtype)
    k_ref[...] = k.astype(k_ref.dtype)
    v_ref[...] = v.astype(v_ref.dtype)


def _qkv_rope(lay, u, w_qkv, cos_t, sin_t):
    nq, nkv = N_HEADS * HEAD_DIM, N_KV_HEADS * HEAD_DIM
    tab = pl.BlockSpec((TOK_TILE, LANES),
                       lambda i: (jnp.where(i < lay.lat_tiles, i % lay.tiles_per_seq, 0), 0))
    return pl.pallas_call(
        functools.partial(_qkv_body, lay.lat_tiles),
        grid=(lay.tiles,),
        in_specs=[_row_spec(), _const_spec((D_MODEL, nq + 2 * nkv)), tab, tab],
        out_specs=[_row_spec(), _row_spec(d=nkv), _row_spec(d=nkv)],
        out_shape=[jax.ShapeDtypeStruct((lay.n_tok, nq), BF16),
                   jax.ShapeDtypeStruct((lay.n_tok, nkv), BF16),
                   jax.ShapeDtypeStruct((lay.n_tok, nkv), BF16)],
        compiler_params=_cparams("arbitrary"),
        name="attn_qkv_rope",
    )(u, w_qkv, cos_t, sin_t)


def _attn_body(nqb, sink_ref, q_ref, kp_ref, kc_ref, kn_ref, kx_ref, vp_ref, vc_ref, vn_ref, vx_ref,
               o_ref, kcat, vcat):
    j = pl.program_id(1)
    blk = ATT_BLOCK
    n_win = 3 * blk
    for dst, parts in ((kcat, (kp_ref, kc_ref, kn_ref)), (vcat, (vp_ref, vc_ref, vn_ref))):
        for n, part in enumerate(parts):
            dst[n * blk:(n + 1) * blk] = part[...]
    kcat[n_win:] = kx_ref[...]
    vcat[n_win:] = vx_ref[...]
    rows = KV_REP * blk
    r = lax.broadcasted_iota(jnp.int32, (rows, n_win), 0) % blk
    c = lax.broadcasted_iota(jnp.int32, (rows, n_win), 1)
    valid = (jnp.abs(r + blk - c) <= WINDOW) & (j < nqb)
    valid = valid & ((c >= blk) | (j > 0)) & ((c < 2 * blk) | (j < nqb - 1))
    rep = lax.broadcasted_iota(jnp.int32, (rows, 1), 0) // blk
    for g in range(N_KV_HEADS):
        heads = [slice((g * KV_REP + n) * HEAD_DIM, (g * KV_REP + n + 1) * HEAD_DIM) for n in range(KV_REP)]
        kv_head = slice(g * HEAD_DIM, (g + 1) * HEAD_DIM)
        q4 = jnp.concatenate([q_ref[:, head] for head in heads], axis=0)
        s = lax.dot_general(q4, kcat[:, kv_head], (((1,), (1,)), ((), ())), preferred_element_type=F32)
        sink = jnp.zeros((rows, 1), F32)
        for n in range(KV_REP):
            sink = jnp.where(rep == n, sink_ref[0, g * KV_REP + n], sink)
        sw = jnp.where(valid, s[:, :n_win], -jnp.inf)
        sc = s[:, n_win:]
        m = jnp.maximum(jnp.max(sw, axis=-1, keepdims=True), jnp.max(sc, axis=-1, keepdims=True))
        m = jnp.maximum(m, sink)
        pw = jnp.exp(sw - m)
        pc = jnp.exp(sc - m)
        denom = (jnp.sum(pw, axis=-1, keepdims=True) + jnp.sum(pc, axis=-1, keepdims=True)
                 + jnp.exp(sink - m))
        acc = jnp.dot(pw.astype(BF16), vcat[:n_win, kv_head], preferred_element_type=F32)
        acc = acc + jnp.dot(pc.astype(BF16), vcat[n_win:, kv_head], preferred_element_type=F32)
        out = (acc / denom).astype(o_ref.dtype)
        for n, head in enumerate(heads):
            o_ref[:, head] = out[n * blk:(n + 1) * blk]


def _attention(lay, q, k, v, sink, with_ctx_out):
    nq, nkv = N_HEADS * HEAD_DIM, N_KV_HEADS * HEAD_DIM
    blk = ATT_BLOCK
    nqb = lay.seq // blk
    lc = lay.ctx_len
    assert lay.n_lat % lc == 0 and lay.seq % blk == 0 and lc % blk == 0
    ctx0 = lay.n_lat // lc
    ncb = lc // blk
    steps = nqb + (ncb if with_ctx_out else 0)
    q_blk = pl.BlockSpec(
        (blk, nq), lambda b, j: (jnp.where(j < nqb, b * nqb + j, (ctx0 + b) * ncb + j - nqb), 0))
    kv_blk = lambda shift: pl.BlockSpec(
        (blk, nkv), lambda b, j: (b * nqb + jnp.clip(j + shift, 0, nqb - 1), 0))
    kv_ctx = pl.BlockSpec((lc, nkv), lambda b, j: (ctx0 + b, 0))
    return pl.pallas_call(
        functools.partial(_attn_body, nqb),
        grid=(lay.batch, steps),
        in_specs=[pl.BlockSpec(memory_space=pltpu.SMEM), q_blk,
                  kv_blk(-1), kv_blk(0), kv_blk(1), kv_ctx,
                  kv_blk(-1), kv_blk(0), kv_blk(1), kv_ctx],
        out_specs=q_blk,
        out_shape=jax.ShapeDtypeStruct((lay.n_tok if with_ctx_out else lay.n_lat, nq), BF16),
        scratch_shapes=[pltpu.VMEM((3 * blk + lc, nkv), BF16), pltpu.VMEM((3 * blk + lc, nkv), BF16)],
        compiler_params=_cparams("arbitrary", "arbitrary"),
        name="attn_window",
    )(sink, q, k, k, k, k, v, v, v, v)


def _pw1_body(u_ref, wa_ref, wg_ref, ba_ref, bg_ref, z_ref):
    u = u_ref[...]
    a = jnp.dot(u, wa_ref[...], preferred_element_type=F32) + ba_ref[...]
    g = jnp.dot(u, wg_ref[...], preferred_element_type=F32) + bg_ref[...]
    z_ref[...] = (a * jax.nn.sigmoid(g)).astype(z_ref.dtype)


def _conv_pw1(lay, u, wa, wg, ba, bg):
    return pl.pallas_call(
        _pw1_body,
        grid=(lay.tiles,),
        in_specs=[_row_spec(), _const_spec((D_MODEL, D_MODEL)), _const_spec((D_MODEL, D_MODEL)),
                  _const_spec((1, D_MODEL)), _const_spec((1, D_MODEL))],
        out_specs=_row_spec(),
        out_shape=jax.ShapeDtypeStruct((lay.n_tok, D_MODEL), BF16),
        compiler_params=_cparams("arbitrary"),
        name="conv_pw1_glu",
    )(u, wa, wg, ba, bg)


def _conv_body(lat_tiles, tps, tpc, zp_ref, z_ref, zn_ref, wdw_ref, bdw_ref, cg_ref, cb_ref, w2_ref, b2_ref,
               h_ref, mod_ref, lng_ref, lnb_ref, wr_ref, br_ref, h1_ref, v_ref, meta_ref, cnt_ref, win, shifted):
    i = pl.program_id(0)
    tm = z_ref.shape[0]
    halo = CONV_HALO
    is_ctx = i >= lat_tiles
    pos = jnp.where(is_ctx, (i - lat_tiles) % tpc, i % tps)
    first = pos == 0
    last = pos == jnp.where(is_ctx, tpc, tps) - 1
    win[0:halo] = jnp.where(first, 0.0, zp_ref[...].astype(F32))
    win[halo:halo + tm] = z_ref[...].astype(F32)
    win[halo + tm:] = jnp.where(last, 0.0, zn_ref[...].astype(F32))
    off = halo - CONV_WIDTH // 2
    span = _round_up(off + CONV_WIDTH - 1, SUBLANES) - SUBLANES
    conv = jnp.zeros((tm, D_MODEL), F32)
    for res in range(SUBLANES):
        taps = [o - off for o in range(res, off + CONV_WIDTH, SUBLANES) if o >= off]
        if not taps:
            continue
        shifted[...] = win[res:res + tm + span, :]
        for tap in taps:
            base = off + tap - res
            conv = conv + wdw_ref[tap:tap + 1, :] * shifted[base:base + tm, :]
    conv = conv + bdw_ref[...]
    nrm = _layer_norm_rows(conv, cg_ref[...], cb_ref[...])
    act = (nrm * jax.nn.sigmoid(nrm)).astype(BF16)
    y = jnp.dot(act, w2_ref[...], preferred_element_type=F32) + b2_ref[...]
    _finish_route(y, h_ref, mod_ref, lng_ref, lnb_ref, wr_ref, br_ref, h1_ref, v_ref, meta_ref, cnt_ref)


def _conv_out(lay, n_tiles, layer, z, w_dw, b_dw, cg, cb, w2, b2, h, mods, lng, lnb, wr, br):
    per_tile = TOK_TILE // CONV_HALO
    n_halo = lay.n_tok // CONV_HALO
    shapes, specs = _route_outputs(n_tiles * TOK_TILE)
    halo_spec = lambda f: pl.BlockSpec((CONV_HALO, D_MODEL), lambda i: (jnp.clip(f(i), 0, n_halo - 1), 0))
    return pl.pallas_call(
        functools.partial(_conv_body, lay.lat_tiles, lay.tiles_per_seq, lay.tiles_per_ctx),
        grid=(n_tiles,),
        in_specs=[halo_spec(lambda i: i * per_tile - 1), _row_spec(), halo_spec(lambda i: (i + 1) * per_tile),
                  _const_spec((CONV_WIDTH, D_MODEL)), _const_spec((1, D_MODEL)), _const_spec((1, D_MODEL)),
                  _const_spec((1, D_MODEL)), _const_spec((D_MODEL, D_MODEL)), _const_spec((1, D_MODEL))]
                 + _route_inputs(lay, layer),
        out_specs=specs,
        out_shape=shapes,
        scratch_shapes=[pltpu.VMEM((TOK_TILE + 2 * CONV_HALO, D_MODEL), F32),
                        pltpu.VMEM((TOK_TILE + 2 * CONV_HALO - SUBLANES, D_MODEL), F32)],
        compiler_params=_cparams("arbitrary"),
        name="conv_dw_out_route",
    )(z, z, z, w_dw, b_dw, cg, cb, w2, b2, h, mods, lng, lnb, wr, br)


def _iota(shape, axis):
    return lax.broadcasted_iota(jnp.int32, shape, axis)


def _s5_prep_body(lrow_ref, lcol_ref, bt_ref, ct_ref, kk_ref, min_ref, mout_ref, arec_ref):
    tc, ch, ns = S5_CHUNK, SSM_CH_PER_GROUP, SSM_STATE
    kw = 2 * tc
    lrow = lrow_ref[0]
    lcol = lcol_ref[0]
    exact_dot = functools.partial(jnp.dot, preferred_element_type=F32, precision=HIGHEST)
    t_lane = _iota((1, tc), 1).astype(F32)
    s_row = _iota((tc, 1), 0).astype(F32)
    lag_lane = _iota((1, kw), 1) - (tc - 1)
    mout_ref[...] = jnp.zeros_like(mout_ref)
    kk = jnp.zeros((ch, ch * kw), F32)
    a_re, a_im = [], []
    for d in range(2):
        lr2 = jnp.minimum(lrow[3 * d:3 * d + 1], -1e-4)
        li2 = lrow[3 * d + 1:3 * d + 2]
        dt2 = jnp.exp(lrow[3 * d + 2:3 * d + 3])
        lr, li, dt = lr2[:, :ns], li2[:, :ns], dt2[:, :ns]
        lrc = jnp.minimum(lcol[:, 3 * d:3 * d + 1], -1e-4)
        lic = lcol[:, 3 * d + 1:3 * d + 2]
        dtc = jnp.exp(lcol[:, 3 * d + 2:3 * d + 3])
        er = jnp.exp(lr * dt)
        xr = er * jnp.cos(li * dt) - 1.0
        xi = er * jnp.sin(li * dt)
        den = lr * lr + li * li
        qr = (xr * lr + xi * li) / den
        qi = (xi * lr - xr * li) / den
        btr, bti = bt_ref[0, 2 * d], bt_ref[0, 2 * d + 1]
        bbr = qr * btr - qi * bti
        bbi = qr * bti + qi * btr
        n_in = (tc - 1.0 - s_row) if d == 0 else s_row
        mg = jnp.exp(n_in * (lr * dt))
        pr, pi = mg * jnp.cos(n_in * (li * dt)), mg * jnp.sin(n_in * (li * dt))
        re0, im0 = ns * d, 2 * ns + ns * d
        for c in range(ch):
            br = jnp.broadcast_to(bbr[c:c + 1], (tc, ns))
            bi = jnp.broadcast_to(bbi[c:c + 1], (tc, ns))
            min_ref[0, tc * c:tc * (c + 1), re0:re0 + ns] = (pr * br - pi * bi).astype(BF16)
            min_ref[0, tc * c:tc * (c + 1), im0:im0 + ns] = (pr * bi + pi * br).astype(BF16)

        ctr, cti = ct_ref[0, 2 * d], ct_ref[0, 2 * d + 1]

        def readout(n_of_lane, keep=None):
            width = n_of_lane.shape[1]
            mag = jnp.exp(n_of_lane * (lrc * dtc))
            w_re, w_im = mag * jnp.cos(n_of_lane * (lic * dtc)), mag * jnp.sin(n_of_lane * (lic * dtc))
            if keep is not None:
                w_re, w_im = jnp.where(keep, w_re, 0.0), jnp.where(keep, w_im, 0.0)
            res_re, res_im = [], []
            for c in range(ch):
                cr = jnp.broadcast_to(ctr[:, c:c + 1], (ns, width))
                ci = jnp.broadcast_to(cti[:, c:c + 1], (ns, width))
                res_re.append(cr * w_re - ci * w_im)
                res_im.append(-(cr * w_im + ci * w_re))
            return jnp.concatenate(res_re, axis=1), jnp.concatenate(res_im, axis=1)

        o_re, o_imneg = readout(t_lane + 1.0 if d == 0 else tc - t_lane)
        base = 4 * ns * d
        mout_ref[0, base + re0:base + re0 + ns, :] = o_re.astype(BF16)
        mout_ref[0, base + im0:base + im0 + ns, :] = o_imneg.astype(BF16)
        lag = lag_lane if d == 0 else -lag_lane
        k_re, k_imneg = readout(jnp.maximum(lag, 0).astype(F32), lag >= 0)
        kk = kk + exact_dot(bbr, k_re) + exact_dot(bbi, k_imneg)
        e2 = jnp.exp(tc * lr2 * dt2)
        a_re.append(e2 * jnp.cos(tc * li2 * dt2))
        a_im.append(e2 * jnp.sin(tc * li2 * dt2))
    kk_ref[0] = kk
    is_fwd = _iota((1, 2 * ns), 1) < ns
    arec_ref[0, 0:1] = jnp.where(is_fwd, a_re[0], a_re[1])
    arec_ref[0, 1:2] = jnp.where(is_fwd, a_im[0], a_im[1])


def _s5_prepare(lam_re, lam_im, log_dt, b_re, b_im, c_re, c_im):
    g, ns, ch, tc = SSM_GROUPS, SSM_STATE, SSM_CH_PER_GROUP, S5_CHUNK
    ldt = jnp.broadcast_to(log_dt[:, :, None], lam_re.shape)
    stack = jnp.stack([lam_re[0], lam_im[0], ldt[0], lam_re[1], lam_im[1], ldt[1]], axis=1)
    lrow = jnp.concatenate([stack, stack], axis=-1)
    lcol = jnp.swapaxes(stack, 1, 2)
    bt = jnp.stack([b_re[0], b_im[0], b_re[1], b_im[1]], axis=1).swapaxes(2, 3)
    ct = jnp.stack([c_re[0], c_im[0], c_re[1], c_im[1]], axis=1).swapaxes(2, 3)
    blk = lambda *s: pl.BlockSpec((1,) + s, lambda i: (i,) + (0,) * len(s))
    return pl.pallas_call(
        _s5_prep_body,
        grid=(g,),
        in_specs=[blk(6, 2 * ns), blk(ns, 6), blk(4, ch, ns), blk(4, ns, ch)],
        out_specs=[blk(ch, 2 * tc * ch), blk(tc * ch, 4 * ns), blk(8 * ns, tc * ch), blk(2, 2 * ns)],
        out_shape=[jax.ShapeDtypeStruct((g, ch, 2 * tc * ch), F32),
                   jax.ShapeDtypeStruct((g, tc * ch, 4 * ns), BF16),
                   jax.ShapeDtypeStruct((g, 8 * ns, tc * ch), BF16),
                   jax.ShapeDtypeStruct((g, 2, 2 * ns), F32)],
        compiler_params=_cparams("arbitrary"),
        name="s5_prepare",
    )(lrow, lcol, bt, ct)


def _s5_core_body(nb, nk, nkc, x_ref, kk_ref, min_ref, mout_ref, arec_ref, y_ref,
                  trow_a, trow_b, yacc, s_ref, h_ref):
    tc, ch, ns = S5_CHUNK, SSM_CH_PER_GROUP, SSM_STATE
    ns2, kw = 2 * ns, 2 * tc
    n_pairs = ch // 2

    def plane_pair(i):
        return jnp.concatenate([x_ref[2 * i], x_ref[2 * i + 1]], axis=1)

    def add_state(i, acc):
        rows = pl.ds(pl.multiple_of(i * kw, kw), kw)
        return acc + jnp.dot(plane_pair(i), min_ref[0, rows, :], preferred_element_type=F32)
    contrib = lax.fori_loop(0, n_pairs, add_state, jnp.zeros((nb * nk, 2 * ns2), F32))
    s_ref[0] = contrib[:, :ns2]
    s_ref[1] = contrib[:, ns2:]

    arec = arec_ref[0]
    a_re, a_im = arec[0:1], arec[1:2]
    is_fwd = _iota((nb, ns2), 1) < ns

    def step(i, carry):
        h_re, h_im = carry
        kr = jnp.where(i < nkc, nkc - 1 - i, nk - 1 - (i - nkc))
        fwd_rows = pl.ds(i, nb, stride=nk)
        rev_rows = pl.ds(kr, nb, stride=nk)
        h_ref[0, 0, fwd_rows, :] = h_re
        h_ref[0, 1, fwd_rows, :] = h_im
        h_ref[1, 0, rev_rows, :] = h_re
        h_ref[1, 1, rev_rows, :] = h_im
        s_re = jnp.where(is_fwd, s_ref[0, fwd_rows, :], s_ref[0, rev_rows, :])
        s_im = jnp.where(is_fwd, s_ref[1, fwd_rows, :], s_ref[1, rev_rows, :])
        return a_re * h_re - a_im * h_im + s_re, a_re * h_im + a_im * h_re + s_im

    zero = jnp.zeros((nb, ns2), F32)
    lax.fori_loop(0, nk, step, (zero, zero))

    y0 = jnp.zeros((nb * nk, ch * tc), F32)
    for d in range(2):
        hd = jnp.concatenate([h_ref[d, 0], h_ref[d, 1]], axis=1).astype(BF16)
        y0 = y0 + jnp.dot(hd, mout_ref[0, 2 * ns2 * d:2 * ns2 * (d + 1), :], preferred_element_type=F32)
    yacc[...] = y0

    def make_blocks(i, dst):
        for cc in range(2):
            lag_row = kk_ref[0, pl.ds(2 * i + cc, 1), :]
            for c in range(ch):
                lag = jnp.broadcast_to(lag_row[:, kw * c:kw * (c + 1)], (tc, kw))
                block = pltpu.roll(lag, tc + 1, 1, stride=1, stride_axis=0)[:, :tc]
                dst[tc * cc:tc * (cc + 1), tc * c:tc * (c + 1)] = block.astype(BF16)

    make_blocks(0, trow_a)

    def add_two_pairs(i2, carry):
        i = 2 * i2
        make_blocks(i + 1, trow_b)
        yacc[...] += jnp.dot(plane_pair(i), trow_a[...], preferred_element_type=F32)
        make_blocks(jnp.minimum(i + 2, n_pairs - 1), trow_a)
        yacc[...] += jnp.dot(plane_pair(i + 1), trow_b[...], preferred_element_type=F32)
        return carry
    lax.fori_loop(0, n_pairs // 2, add_two_pairs, 0)
    for c in range(ch):
        y_ref[c] = yacc[:, tc * c:tc * (c + 1)].astype(y_ref.dtype)


def _s5_core(lay, u, mats):
    kk, m_in, m_out, arec = mats
    g, ch, tc, ns = SSM_GROUPS, SSM_CH_PER_GROUP, S5_CHUNK, SSM_STATE
    b = lay.batch
    assert lay.seq % tc == 0 and lay.ctx_len % tc == 0
    nkl, nkc = lay.seq // tc, lay.ctx_len // tc
    nk = nkl + nkc
    rows = b * nk
    x = u.reshape(rows, tc, D_MODEL).transpose(2, 0, 1)
    blk = lambda *s: pl.BlockSpec((1,) + s, lambda i: (i,) + (0,) * len(s))
    planes = pl.BlockSpec((ch, rows, tc), lambda i: (i, 0, 0))
    y = pl.pallas_call(
        functools.partial(_s5_core_body, b, nk, nkc),
        grid=(g,),
        in_specs=[planes, blk(ch, 2 * tc * ch), blk(tc * ch, 4 * ns), blk(8 * ns, tc * ch), blk(2, 2 * ns)],
        out_specs=planes,
        out_shape=jax.ShapeDtypeStruct((D_MODEL, rows, tc), BF16),
        scratch_shapes=[pltpu.VMEM((2 * tc, ch * tc), BF16), pltpu.VMEM((2 * tc, ch * tc), BF16),
                        pltpu.VMEM((rows, ch * tc), F32),
                        pltpu.VMEM((2, rows, 2 * ns), F32), pltpu.VMEM((2, 2, rows, 2 * ns), F32)],
        compiler_params=_cparams("arbitrary"),
        name="s5_scan",
    )(x, kk, m_in, m_out, arec)
    return y.transpose(1, 2, 0).reshape(rows * tc, D_MODEL)


def _s5_out_body(y_ref, u_ref, d_ref, wv_ref, wg_ref, *rest):
    u = u_ref[...].astype(F32)
    act = _gelu_tanh(y_ref[...].astype(F32) + d_ref[...] * u).astype(BF16)
    val = jnp.dot(act, wv_ref[...], preferred_element_type=F32)
    gate = jnp.dot(act, wg_ref[...], preferred_element_type=F32)
    _finish_route(val * jax.nn.sigmoid(gate), *rest)


def _s5_out(lay, n_tiles, layer, y, u, d, wv, wg, h, mods, lng, lnb, wr, br):
    shapes, specs = _route_outputs(n_tiles * TOK_TILE)
    return pl.pallas_call(
        _s5_out_body,
        grid=(n_tiles,),
        in_specs=[lay.s5_row_spec(), lay.s5_row_spec(), _const_spec((1, D_MODEL)),
                  _const_spec((D_MODEL, D_MODEL)), _const_spec((D_MODEL, D_MODEL))] + _route_inputs(lay, layer),
        out_specs=specs,
        out_shape=shapes,
        compiler_params=_cparams("arbitrary"),
        name="s5_out_route",
    )(y, u, d, wv, wg, h, mods, lng, lnb, wr, br)


def kernel(x, c, ctx, c_ctx, ada_w, ada_b, ln_g, ln_b, s5_lam_re, s5_lam_im, s5_log_dt, s5_b_re, s5_b_im, s5_c_re, s5_c_im, s5_d, s5_w_glu, cv_w_pw1, cv_b_pw1, cv_w_dw, cv_b_dw, cv_ln_g, cv_ln_b, cv_w_pw2, cv_b_pw2, at_w_qkv, at_w_o, at_sink, moe_wg, moe_bg, moe_we, moe_be, moe_w1, moe_w2):
    batch, seq, d = x.shape
    lay = _Layout(batch, seq, ctx.shape[1])
    depth = ada_w.shape[0]
    row = lambda a: a.reshape(1, -1)

    c_all =jnp.concatenate([c, c_ctx[None], jnp.zeros((lay.mod_rows - batch - 1, d), F32)], axis=0)
    mods = _modulation(c_all, ada_w, ada_b)
    cos_t, sin_t = _rope_tables(seq)
    ng, ne = N_EXPERT_GROUPS, N_EXPERTS
    pad = jnp.zeros((d, ROUTE_LANES - ng - ne), F32)

    u_spec = lambda layer: lay.s5_row_spec() if layer % 3 == 0 else _row_spec()
    h, u = _modulate(lay, x.reshape(lay.n_lat, d), ctx.reshape(lay.n_ctx, d), mods, 0, u_spec(0))
    for i in range(depth):
        last = i == depth - 1
        kind, j = i % 3, i // 3
        n_tiles = lay.lat_tiles if last else lay.tiles
        wr = _split_bf16(jnp.concatenate([moe_wg[i], moe_we[i], pad], axis=1))
        br = jnp.concatenate([moe_bg[i], moe_be[i], pad[0]], axis=0)[None]
        route_args = (h, mods, row(ln_g[i, 0]), row(ln_b[i, 0]), wr, br)
        if kind == 0:
            mats = _s5_prepare(s5_lam_re[j], s5_lam_im[j], s5_log_dt[j], s5_b_re[j], s5_b_im[j],
                               s5_c_re[j], s5_c_im[j])
            y = _s5_core(lay, u, mats)
            wglu = s5_w_glu[j].astype(BF16)
            h1, v, meta, cnt = _s5_out(lay, n_tiles, i, y, u, row(s5_d[j]), wglu[:, :d], wglu[:, d:], *route_args)
        elif kind == 1:
            w1 = cv_w_pw1[j].astype(BF16)
            z = _conv_pw1(lay, u, w1[:, :d], w1[:, d:], row(cv_b_pw1[j, :d]), row(cv_b_pw1[j, d:]))
            h1, v, meta, cnt = _conv_out(lay, n_tiles, i, z, cv_w_dw[j], row(cv_b_dw[j]), row(cv_ln_g[j]),
                                         row(cv_ln_b[j]), cv_w_pw2[j].astype(BF16), row(cv_b_pw2[j]),
                                         *route_args)
        else:
            q, k, vv = _qkv_rope(lay, u, at_w_qkv[j].astype(BF16), cos_t, sin_t)
            o = _attention(lay, q, k, vv, row(at_sink[j]), not last)
            h1, v, meta, cnt = _attn_out(lay, n_tiles, i, o, at_w_o[j].astype(BF16), *route_args)
        plan = _moe_plan(meta, cnt, n_tiles)
        xs = _moe_dispatch(v, plan)
        ys = _moe_experts(xs, moe_w1[i], moe_w2[i], plan.items)
        h, u = _moe_combine(lay, n_tiles, ys, plan, h1, mods, i, row(ln_g[i, 1]), row(ln_b[i, 1]),
                            None if last else u_spec(i + 1))
    return h.reshape(batch, seq, d)
```

```python
import functools
import math

import jax
import jax.numpy as jnp
from jax import lax
from jax.experimental import pallas as pl
from jax.experimental.pallas import tpu as pltpu

F32 = jnp.float32
BF16 = jnp.bfloat16
HIGHEST = lax.Precision.HIGHEST

D_MODEL = 1024
DEPTH = 4
GRID_W = 64
SSM_CH_PER_GROUP = 16
SSM_GROUPS = D_MODEL // SSM_CH_PER_GROUP
SSM_STATE = 64
CONV_WIDTH = 31
HEAD_DIM = 64
N_HEADS = D_MODEL // HEAD_DIM
N_KV_HEADS = N_HEADS // 4
KV_REP = N_HEADS // N_KV_HEADS
WINDOW = 128
ATT_BLOCK = 128
ROPE_BASE = 10000.0
N_EXPERT_GROUPS = 4
EXPERTS_PER_GROUP = 8
N_EXPERTS = N_EXPERT_GROUPS * EXPERTS_PER_GROUP
D_EXPERT = D_MODEL // 2
ALPHA = (2 * DEPTH) ** 0.25
LN_EPS = 1e-5

SUBLANES = 8
LANES = 128

TOK_TILE = 256
EXPERT_TILE = 512
S5_CHUNK = LANES
CONV_HALO = 16
ROUTE_LANES = LANES
META_COLS = 8
RUN_PAD = SUBLANES
LOCAL_ROWS = -(-(2 * TOK_TILE + N_EXPERTS * (RUN_PAD - 1)) // LANES) * LANES
VMEM_LIMIT = 56 * 1024 * 1024


def _cparams(*sem):
    return pltpu.CompilerParams(dimension_semantics=sem, vmem_limit_bytes=VMEM_LIMIT)


def _round_up(n, m):
    return (n + m - 1) // m * m


def _split_bf16(w):
    hi = w.astype(BF16)
    return jnp.stack([hi, (w - hi.astype(F32)).astype(BF16)])


def _mod_body(c_ref, w_ref, b_ref, o_ref):
    c = c_ref[...]
    s = c * jax.nn.sigmoid(c)
    o_ref[0] = jnp.dot(s, w_ref[0], preferred_element_type=F32, precision=HIGHEST) + b_ref[0]


def _modulation(c_all, ada_w, ada_b):
    depth, d, n = ada_w.shape
    r = c_all.shape[0]
    tn = 1024
    out = pl.pallas_call(
        _mod_body,
        grid=(depth, n // tn),
        in_specs=[
            pl.BlockSpec((r, d), lambda i, j: (0, 0)),
            pl.BlockSpec((1, d, tn), lambda i, j: (i, 0, j)),
            pl.BlockSpec((1, 1, tn), lambda i, j: (i, 0, j)),
        ],
        out_specs=pl.BlockSpec((1, r, tn), lambda i, j: (i, 0, j)),
        out_shape=jax.ShapeDtypeStruct((depth, r, n), F32),
        compiler_params=_cparams("arbitrary", "arbitrary"),
        name="adaln_modulation",
    )(c_all, ada_w, ada_b.reshape(depth, 1, n))
    return out.reshape(depth, r, 6, d)


def _layer_norm_rows(t, g, b):
    mu = jnp.mean(t, axis=-1, keepdims=True)
    dev = t - mu
    var = jnp.mean(dev * dev, axis=-1, keepdims=True)
    return dev * lax.rsqrt(var + LN_EPS) * g + b


def _post_norm_and_route(h, y, mod, lng, lnb, wr, br):
    tm = h.shape[0]
    ng, ne = N_EXPERT_GROUPS, N_EXPERTS
    h1 = _layer_norm_rows(ALPHA * h + mod[2:3] * y, lng, lnb)
    v = h1 * (1.0 + mod[4:5]) + mod[3:4]
    v_hi = v.astype(BF16)
    v_lo = (v - v_hi.astype(F32)).astype(BF16)
    logits = (jnp.dot(v_hi, wr[0], preferred_element_type=F32) + jnp.dot(v_lo, wr[0], preferred_element_type=F32)
              + jnp.dot(v_hi, wr[1], preferred_element_type=F32) + br)

    lane = lax.broadcasted_iota(jnp.int32, (tm, ROUTE_LANES), 1)
    lane_f = lane.astype(F32)
    neg = -jnp.inf
    no_lane = float(ROUTE_LANES)
    is_group = lane < ng
    gl = jnp.where(is_group, logits, neg)
    gmax = jnp.max(gl, axis=-1, keepdims=True)
    gsum = jnp.sum(jnp.where(is_group, jnp.exp(logits - gmax), 0.0), axis=-1, keepdims=True)
    p_group = 1.0 / gsum
    g_idx = jnp.min(jnp.where(gl == gmax, lane_f, no_lane), axis=-1, keepdims=True)
    expert_group = ((lane - ng) // EXPERTS_PER_GROUP).astype(F32)
    in_group = (lane >= ng) & (lane < ng + ne) & (expert_group == g_idx)
    el = jnp.where(in_group, logits, neg)
    v1 = jnp.max(el, axis=-1, keepdims=True)
    i1 = jnp.min(jnp.where(el == v1, lane_f, no_lane), axis=-1, keepdims=True)
    el2 = jnp.where(lane_f == i1, neg, el)
    v2 = jnp.max(el2, axis=-1, keepdims=True)
    i2 = jnp.min(jnp.where(el2 == v2, lane_f, no_lane), axis=-1, keepdims=True)
    e21 = jnp.exp(v2 - v1)
    w1 = p_group / (1.0 + e21)
    w2 = p_group * e21 / (1.0 + e21)

    hit1 = lane_f == i1
    hit2 = lane_f == i2
    one1 = hit1.astype(BF16)
    one2 = hit2.astype(BF16)
    rr = lax.broadcasted_iota(jnp.int32, (tm, tm), 0)
    cc = lax.broadcasted_iota(jnp.int32, (tm, tm), 1)
    before = (cc < rr).astype(BF16)
    cum1 = jnp.dot(before, one1, preferred_element_type=F32)
    cum2 = jnp.dot(before, one2, preferred_element_type=F32)
    tot1 = jnp.sum(one1.astype(F32), axis=0, keepdims=True)
    tot2 = jnp.sum(one2.astype(F32), axis=0, keepdims=True)
    counts = tot1 + tot2
    pieces = jnp.floor((counts + (RUN_PAD - 1.0)) * (1.0 / RUN_PAD))
    earlier = (lax.broadcasted_iota(jnp.int32, (ROUTE_LANES, ROUTE_LANES), 0)
               < lax.broadcasted_iota(jnp.int32, (ROUTE_LANES, ROUTE_LANES), 1)).astype(BF16)
    run_start = RUN_PAD * jnp.dot(jnp.broadcast_to(pieces, (SUBLANES, ROUTE_LANES)).astype(BF16), earlier,
                                  preferred_element_type=F32)[0:1]
    rank1 = jnp.sum(jnp.where(hit1, run_start + cum1, 0.0), axis=-1, keepdims=True)
    rank2 = jnp.sum(jnp.where(hit2, run_start + tot1 + cum2, 0.0), axis=-1, keepdims=True)

    col = lax.broadcasted_iota(jnp.int32, (tm, META_COLS), 1)
    meta = jnp.where(col == 0, i1 - ng,
           jnp.where(col == 1, i2 - ng,
           jnp.where(col == 2, rank1,
           jnp.where(col == 3, rank2,
           jnp.where(col == 4, w1,
           jnp.where(col == 5, w2, 0.0))))))
    return h1, v, meta, tot1 + tot2


def _gelu_tanh(x):
    return 0.5 * x * (1.0 + jnp.tanh(math.sqrt(2.0 / math.pi) * (x + 0.044715 * (x * x * x))))


class _Layout:
    def __init__(self, batch, seq, ctx_len):
        self.batch, self.seq, self.ctx_len = batch, seq, ctx_len
        self.n_lat = batch * seq
        self.n_ctx = batch * ctx_len
        self.n_tok = self.n_lat + self.n_ctx
        assert seq % TOK_TILE == 0 and ctx_len % TOK_TILE == 0
        self.lat_tiles = self.n_lat // TOK_TILE
        self.tiles = self.n_tok // TOK_TILE
        self.tiles_per_seq = seq // TOK_TILE
        self.tiles_per_ctx = ctx_len // TOK_TILE
        self.mod_rows = _round_up(batch + 1, SUBLANES)

    def mod_row(self, i):
        return jnp.minimum(i // self.tiles_per_seq, self.batch)

    def s5_tile(self, i):
        per_batch = self.tiles_per_seq + self.tiles_per_ctx
        lat = (i // self.tiles_per_seq) * per_batch + self.tiles_per_ctx + i % self.tiles_per_seq
        c = i - self.lat_tiles
        ctx = (c // self.tiles_per_ctx) * per_batch + c % self.tiles_per_ctx
        return jnp.where(i < self.lat_tiles, lat, ctx)

    def s5_row_spec(self):
        return pl.BlockSpec((TOK_TILE, D_MODEL), lambda i, *_: (self.s5_tile(i), 0))


def _mod_spec(lay, layer):
    return pl.BlockSpec((1, 1, 6, D_MODEL), lambda i, *_: (layer, lay.mod_row(i), 0, 0))


def _row_spec(tm=TOK_TILE, d=D_MODEL):
    return pl.BlockSpec((tm, d), lambda i, *_: (i, 0))


def _const_spec(shape):
    nd = len(shape)
    return pl.BlockSpec(shape, lambda i, *_: (0,) * nd)


def _modulate_body(lat_tiles, x_ref, ctx_ref, mod_ref, h_ref, u_ref):
    mod = mod_ref[0, 0]
    h = jnp.where(pl.program_id(0) < lat_tiles, x_ref[...], ctx_ref[...])
    h_ref[...] = h
    u_ref[...] = (h * (1.0 + mod[1:2]) + mod[0:1]).astype(u_ref.dtype)


def _modulate(lay, x, ctx, mods, layer, u_spec):
    ctx_tiles = lay.tiles - lay.lat_tiles
    return pl.pallas_call(
        functools.partial(_modulate_body, lay.lat_tiles),
        grid=(lay.tiles,),
        in_specs=[pl.BlockSpec((TOK_TILE, D_MODEL), lambda i: (jnp.minimum(i, lay.lat_tiles - 1), 0)),
                  pl.BlockSpec((TOK_TILE, D_MODEL), lambda i: (jnp.clip(i - lay.lat_tiles, 0, ctx_tiles - 1), 0)),
                  _mod_spec(lay, layer)],
        out_specs=[_row_spec(), u_spec],
        out_shape=[jax.ShapeDtypeStruct((lay.n_tok, D_MODEL), F32),
                   jax.ShapeDtypeStruct((lay.n_tok, D_MODEL), BF16)],
        compiler_params=_cparams("arbitrary"),
        name="input_modulate",
    )(x, ctx, mods)


PACKED = D_MODEL // 2


def _pack_rows(x):
    half = x.shape[1] // 2
    hi = lax.bitcast_convert_type(x[:, :half], jnp.uint32)
    lo = lax.bitcast_convert_type(x[:, half:], jnp.uint32)
    return hi | (lo >> 16)


def _unpack_rows(w):
    hi = lax.bitcast_convert_type(w & jnp.uint32(0xFFFF0000), F32)
    lo = lax.bitcast_convert_type(w << 16, F32)
    return hi.astype(BF16), lo.astype(BF16)


def _piece_spec(n_tiles, shift=0):
    return pl.BlockSpec((1, 1, 2 * N_EXPERTS), lambda i, *_: (jnp.minimum(i + shift, n_tiles - 1), 0, 0),
                        memory_space=pltpu.SMEM)


def _for_each_piece(runs_ref, fn):
    def per_expert(e, q0):
        n = runs_ref[0, 0, e]
        row0 = runs_ref[0, 0, N_EXPERTS + e]

        def per_piece(p, carry):
            fn(q0 + p, row0 + p * RUN_PAD)
            return carry

        lax.fori_loop(0, n, per_piece, 0)
        return q0 + n
    lax.fori_loop(0, N_EXPERTS, per_expert, 0)


def _dispatch_body(np_ref, tail_ref, runs_ref, lp_ref, v_ref, xs_ref, loc, zeros, sem):
    i = pl.program_id(0)
    lp = lp_ref[0]
    slot_row = lax.broadcasted_iota(jnp.int32, (LOCAL_ROWS, v_ref.shape[0]), 0).astype(F32)
    pick = ((slot_row == lp[0:1]) | (slot_row == lp[1:2])).astype(BF16)
    loc[...] = _pack_rows(jnp.dot(pick, v_ref[...], preferred_element_type=F32))

    def piece(src, dst):
        return pltpu.make_async_copy(src, xs_ref.at[pl.ds(pl.multiple_of(dst, RUN_PAD), RUN_PAD)], sem)

    def issue(q, row):
        piece(loc.at[pl.ds(pl.multiple_of(q * RUN_PAD, RUN_PAD), RUN_PAD)], row).start()

    def drain(q, carry):
        piece(loc.at[pl.ds(0, RUN_PAD)], 0).wait()
        return carry

    _for_each_piece(runs_ref, issue)
    lax.fori_loop(0, np_ref[i], drain, 0)

    @pl.when(i == pl.num_programs(0) - 1)
    def _():
        zeros[...] = jnp.zeros_like(zeros)

        def issue_zero(q, carry):
            piece(zeros, tail_ref[0] + q * RUN_PAD).start()
            return carry

        def drain_zero(q, carry):
            piece(zeros, 0).wait()
            return carry

        lax.fori_loop(0, tail_ref[1], issue_zero, 0)
        lax.fori_loop(0, tail_ref[1], drain_zero, 0)


def _moe_dispatch(v, plan):
    n_tiles = plan.n_pieces.shape[0]
    grid_spec = pltpu.PrefetchScalarGridSpec(
        num_scalar_prefetch=2,
        grid=(n_tiles,),
        in_specs=[_piece_spec(n_tiles),
                  pl.BlockSpec((1, 2, TOK_TILE), lambda i, *_: (i, 0, 0)),
                  _row_spec()],
        out_specs=pl.BlockSpec(memory_space=pl.ANY),
        scratch_shapes=[pltpu.VMEM((LOCAL_ROWS, PACKED), jnp.uint32), pltpu.VMEM((RUN_PAD, PACKED), jnp.uint32),
                        pltpu.SemaphoreType.DMA(())],
    )
    return pl.pallas_call(
        _dispatch_body,
        grid_spec=grid_spec,
        out_shape=jax.ShapeDtypeStruct((plan.max_rows, PACKED), jnp.uint32),
        compiler_params=_cparams("arbitrary"),
        name="moe_dispatch",
    )(plan.n_pieces, plan.tail, plan.runs, plan.lp_rows, v)


def _expert_body(tile_ref, exp_ref, lo_ref, hi_ref, xs_ref, w1_ref, w2_ref, ys_ref, w1b, w2b):
    j = pl.program_id(0)
    jp = jnp.maximum(j - 1, 0)
    new_expert = (j == 0) | (exp_ref[j] != exp_ref[jp])
    first_visit = (j == 0) | (tile_ref[j] != tile_ref[jp])
    lo, hi = lo_ref[j], hi_ref[j]

    @pl.when(new_expert)
    def _():
        w1b[...] = w1_ref[0, 0].astype(BF16)
        w2b[...] = w2_ref[0, 0].astype(BF16)

    def expert_rows():
        x_a, x_b = _unpack_rows(xs_ref[...])
        gu = (jnp.dot(x_a, w1b[:PACKED], preferred_element_type=F32)
              + jnp.dot(x_b, w1b[PACKED:], preferred_element_type=F32))
        gate, up = gu[:, :D_EXPERT], gu[:, D_EXPERT:]
        a = (gate * jax.nn.sigmoid(gate) * up).astype(BF16)
        y = jnp.dot(a, w2b[...], preferred_element_type=F32)
        rows = lax.broadcasted_iota(jnp.int32, (xs_ref.shape[0], 1), 0)
        return _pack_rows(y.astype(BF16).astype(F32)), (rows >= lo) & (rows < hi)

    @pl.when(first_visit)
    def _():
        y, mine = expert_rows()
        ys_ref[...] = jnp.where(mine, y, jnp.uint32(0))

    @pl.when(jnp.logical_not(first_visit) & (hi > lo))
    def _():
        y, mine = expert_rows()
        ys_ref[...] = jnp.where(mine, y, ys_ref[...])


def _moe_experts(xs, w1, w2, layer, items):
    tile_j, exp_j, lo_j, hi_j = items
    n_items = tile_j.shape[0]
    grid_spec = pltpu.PrefetchScalarGridSpec(
        num_scalar_prefetch=4,
        grid=(n_items,),
        in_specs=[
            pl.BlockSpec((EXPERT_TILE, PACKED), lambda j, t, e, lo, hi: (t[j], 0)),
            pl.BlockSpec((1, 1, D_MODEL, 2 * D_EXPERT), lambda j, t, e, lo, hi: (layer, e[j], 0, 0)),
            pl.BlockSpec((1, 1, D_EXPERT, D_MODEL), lambda j, t, e, lo, hi: (layer, e[j], 0, 0)),
        ],
        out_specs=pl.BlockSpec((EXPERT_TILE, PACKED), lambda j, t, e, lo, hi: (t[j], 0)),
        scratch_shapes=[pltpu.VMEM((D_MODEL, 2 * D_EXPERT), BF16), pltpu.VMEM((D_EXPERT, D_MODEL), BF16)],
    )
    return pl.pallas_call(
        _expert_body,
        grid_spec=grid_spec,
        out_shape=jax.ShapeDtypeStruct(xs.shape, xs.dtype),
        compiler_params=_cparams("arbitrary"),
        name="moe_experts",
    )(tile_j, exp_j, lo_j, hi_j, xs, w1, w2)


def _combine_body(has_next, np_ref, src_ref, srcn_ref, lpw_ref, h1_ref, mod_ref, modn_ref, lng_ref, lnb_ref,
                  ys_ref, *rest):
    if has_next:
        h2_ref, u_ref, buf, sem = rest
    else:
        h2_ref, buf, sem = rest
    tm = h1_ref.shape[0]
    i = pl.program_id(0)
    n = pl.num_programs(0)
    slot = i % 2

    def piece(src, s, q):
        return pltpu.make_async_copy(ys_ref.at[pl.ds(pl.multiple_of(src, RUN_PAD), RUN_PAD)],
                                     buf.at[s, pl.ds(pl.multiple_of(q * RUN_PAD, RUN_PAD), RUN_PAD)], sem.at[s])

    def gather(runs_ref, s):
        _for_each_piece(runs_ref, lambda q, row: piece(row, s, q).start())

    @pl.when(i == 0)
    def _():
        buf[...] = jnp.zeros_like(buf)
        gather(src_ref, 0)

    @pl.when(i + 1 < n)
    def _():
        gather(srcn_ref, 1 - slot)

    def drain(q, carry):
        piece(0, slot, 0).wait()
        return carry
    lax.fori_loop(0, np_ref[i], drain, 0)

    lpw = lpw_ref[...]
    lane = lax.broadcasted_iota(jnp.int32, (tm, LOCAL_ROWS), 1).astype(F32)
    wmat = jnp.where(lane == lpw[:, 0:1], lpw[:, 2:3], 0.0) + jnp.where(lane == lpw[:, 1:2], lpw[:, 3:4], 0.0)
    w_hi = wmat.astype(BF16)
    w_lo = (wmat - w_hi.astype(F32)).astype(BF16)
    f = jnp.concatenate(
        [jnp.dot(w_hi, yb, preferred_element_type=F32) + jnp.dot(w_lo, yb, preferred_element_type=F32)
         for yb in _unpack_rows(buf[slot])], axis=1)
    mod = mod_ref[0, 0]
    h2 = _layer_norm_rows(ALPHA * h1_ref[...] + mod[5:6] * f, lng_ref[...], lnb_ref[...])
    h2_ref[...] = h2
    if has_next:
        modn = modn_ref[0, 0]
        u_ref[...] = (h2 * (1.0 + modn[1:2]) + modn[0:1]).astype(u_ref.dtype)


def _moe_combine(lay, n_tiles, ys, plan, h1, mods, layer, lng, lnb, u_spec):
    has_next = u_spec is not None
    n_rows = n_tiles * TOK_TILE
    nxt = min(layer + 1, DEPTH - 1)
    out_shape = [jax.ShapeDtypeStruct((n_rows, D_MODEL), F32)]
    out_specs = [_row_spec()]
    if has_next:
        out_shape.append(jax.ShapeDtypeStruct((n_rows, D_MODEL), BF16))
        out_specs.append(u_spec)
    grid_spec = pltpu.PrefetchScalarGridSpec(
        num_scalar_prefetch=1,
        grid=(n_tiles,),
        in_specs=[
            _piece_spec(n_tiles), _piece_spec(n_tiles, 1),
            _row_spec(d=4), _row_spec(), _mod_spec(lay, layer), _mod_spec(lay, nxt),
            _const_spec((1, D_MODEL)), _const_spec((1, D_MODEL)),
            pl.BlockSpec(memory_space=pl.ANY),
        ],
        out_specs=out_specs,
        scratch_shapes=[pltpu.VMEM((2, LOCAL_ROWS, PACKED), jnp.uint32), pltpu.SemaphoreType.DMA((2,))],
    )
    outs = pl.pallas_call(
        functools.partial(_combine_body, has_next),
        grid_spec=grid_spec,
        out_shape=out_shape,
        compiler_params=_cparams("arbitrary"),
        name="moe_combine",
    )(plan.n_pieces, plan.runs, plan.runs, plan.lp_w, h1, mods, mods, lng, lnb, ys)
    return outs if has_next else (outs[0], None)


class _MoePlan:
    pass


def _exclusive_cumsum(a, axis):
    return jnp.cumsum(a, axis=axis) - a


def _moe_plan(meta, tile_counts, n_tiles):
    plan = _MoePlan()
    plan.max_rows = _round_up(n_tiles * (2 * TOK_TILE + N_EXPERTS * (RUN_PAD - 1)), EXPERT_TILE)
    cnt = tile_counts[:, 0, N_EXPERT_GROUPS:N_EXPERT_GROUPS + N_EXPERTS].astype(jnp.int32)
    run = (cnt + RUN_PAD - 1) // RUN_PAD * RUN_PAD
    counts = jnp.sum(run, axis=0)
    total = jnp.sum(counts)
    tail = (-total) % EXPERT_TILE
    counts = counts.at[N_EXPERTS - 1].add(tail)
    ends = jnp.cumsum(counts)
    starts = ends - counts
    run_start = starts[None, :] + _exclusive_cumsum(run, 0)
    plan.tail = jnp.stack([total, tail // RUN_PAD]).astype(jnp.int32)

    plan.lp_rows = meta[:, 2:4].reshape(n_tiles, TOK_TILE, 2).transpose(0, 2, 1)
    plan.lp_w = meta[:, 2:6]

    pieces = run // RUN_PAD
    plan.n_pieces = jnp.sum(pieces, axis=1).astype(jnp.int32)
    plan.runs = jnp.concatenate([pieces, run_start], axis=1).astype(jnp.int32).reshape(n_tiles, 1, 2 * N_EXPERTS)

    n_etiles = plan.max_rows // EXPERT_TILE
    first_tile = starts // EXPERT_TILE
    last_tile = jnp.maximum(ends - 1, 0) // EXPERT_TILE
    n_items_e = jnp.where(counts > 0, last_tile - first_tile + 1, 0)
    item_end = jnp.cumsum(n_items_e)
    item_start = item_end - n_items_e
    n_items = n_etiles + N_EXPERTS
    j = jnp.arange(n_items, dtype=jnp.int32)
    e_j = jnp.minimum(jnp.sum(j[:, None] >= item_end[None, :], axis=1), N_EXPERTS - 1).astype(jnp.int32)
    active = j < item_end[-1]
    tile_j = jnp.take(first_tile, e_j) + (j - jnp.take(item_start, e_j))
    lo = jnp.maximum(jnp.take(starts, e_j), tile_j * EXPERT_TILE) - tile_j * EXPERT_TILE
    hi = jnp.minimum(jnp.take(ends, e_j), (tile_j + 1) * EXPERT_TILE) - tile_j * EXPERT_TILE
    last = jnp.maximum(item_end[-1] - 1, 0)
    tile_j = jnp.where(active, tile_j, tile_j[last]).astype(jnp.int32)
    e_j = jnp.where(active, e_j, e_j[last]).astype(jnp.int32)
    lo = jnp.where(active, lo, 0).astype(jnp.int32)
    hi = jnp.where(active, hi, 0).astype(jnp.int32)
    plan.items = (tile_j, e_j, lo, hi)
    return plan


def _route_outputs(n_rows):
    n_tiles = n_rows // TOK_TILE
    shapes = [jax.ShapeDtypeStruct((n_rows, D_MODEL), F32),
              jax.ShapeDtypeStruct((n_rows, D_MODEL), BF16),
              jax.ShapeDtypeStruct((n_rows, META_COLS), F32),
              jax.ShapeDtypeStruct((n_tiles, 1, ROUTE_LANES), F32)]
    specs = [_row_spec(), _row_spec(), _row_spec(d=META_COLS),
             pl.BlockSpec((1, 1, ROUTE_LANES), lambda i: (i, 0, 0))]
    return shapes, specs


def _route_inputs(lay, layer):
    return [_row_spec(), _mod_spec(lay, layer), _const_spec((1, D_MODEL)), _const_spec((1, D_MODEL)),
            _const_spec((2, D_MODEL, ROUTE_LANES)), _const_spec((1, ROUTE_LANES))]


def _finish_route(y, h_ref, mod_ref, lng_ref, lnb_ref, wr_ref, br_ref, h1_ref, v_ref, meta_ref, cnt_ref):
    h1, v, meta, counts = _post_norm_and_route(h_ref[...], y, mod_ref[0, 0], lng_ref[...], lnb_ref[...],
                                               wr_ref[...], br_ref[...])
    h1_ref[...] = h1
    v_ref[...] = v.astype(v_ref.dtype)
    meta_ref[...] = meta
    cnt_ref[0] = counts


def _attn_out_body(o_ref, wo_ref, *rest):
    y = jnp.dot(o_ref[...], wo_ref[...], preferred_element_type=F32)
    _finish_route(y, *rest)


def _attn_out(lay, n_tiles, layer, o, w_o, h, mods, lng, lnb, wr, br):
    shapes, specs = _route_outputs(n_tiles * TOK_TILE)
    return pl.pallas_call(
        _attn_out_body,
        grid=(n_tiles,),
        in_specs=[_row_spec(), _const_spec((D_MODEL, D_MODEL))] + _route_inputs(lay, layer),
        out_specs=specs,
        out_shape=shapes,
        compiler_params=_cparams("arbitrary"),
        name="attn_out_route",
    )(o, w_o, h, mods, lng, lnb, wr, br)


def _rope_tables(seq):
    quarter = HEAD_DIM // 4
    inv_freq = ROPE_BASE ** (-jnp.arange(quarter, dtype=F32) / quarter)
    t = jnp.arange(seq)
    rows = (t // GRID_W).astype(F32)
    cols = (t % GRID_W).astype(F32)
    ang_r = rows[:, None] * inv_freq
    ang_c = cols[:, None] * inv_freq
    cos_h = jnp.concatenate([jnp.cos(ang_r), jnp.cos(ang_r), jnp.cos(ang_c), jnp.cos(ang_c)], axis=-1)
    sin_h = jnp.concatenate([-jnp.sin(ang_r), jnp.sin(ang_r), -jnp.sin(ang_c), jnp.sin(ang_c)], axis=-1)
    rep = LANES // HEAD_DIM
    return jnp.tile(cos_h, (1, rep)), jnp.tile(sin_h, (1, rep))


def _rope(x, cos, sin):
    quarter = HEAD_DIM // 4
    width = x.shape[1]
    n = width // LANES
    c = jnp.tile(cos, (1, n))
    s = jnp.tile(sin, (1, n))
    lane = lax.broadcasted_iota(jnp.int32, x.shape, 1)
    is_lo = (lane % (2 * quarter)) < quarter
    partner = jnp.where(is_lo, pltpu.roll(x, width - quarter, 1), pltpu.roll(x, quarter, 1))
    return x * c + partner * s


def _qkv_body(lat_tiles, u_ref, w_ref, cos_ref, sin_ref, q_ref, k_ref, v_ref):
    is_ctx = pl.program_id(0) >= lat_tiles
    qkv = jnp.dot(u_ref[...], w_ref[...], preferred_element_type=F32)
    nq, nkv = N_HEADS * HEAD_DIM, N_KV_HEADS * HEAD_DIM
    q, k, v = qkv[:, :nq], qkv[:, nq:nq + nkv], qkv[:, nq + nkv:]
    cos, sin = cos_ref[...], sin_ref[...]
    q = jnp.where(is_ctx, q, _rope(q, cos, sin))
    k = jnp.where(is_ctx, k, _rope(k, cos, sin))
    q_ref[...] = (q * HEAD_DIM ** -0.5).astype(q_ref.dtype)
    k_ref[...] = k.astype(k_ref.dtype)
    v_ref[...] = v.astype(v_ref.dtype)


def _qkv_rope(lay, u, w_qkv, cos_t, sin_t):
    nq, nkv = N_HEADS * HEAD_DIM, N_KV_HEADS * HEAD_DIM
    tab = pl.BlockSpec((TOK_TILE, LANES),
                       lambda i: (jnp.where(i < lay.lat_tiles, i % lay.tiles_per_seq, 0), 0))
    return pl.pallas_call(
        functools.partial(_qkv_body, lay.lat_tiles),
        grid=(lay.tiles,),
        in_specs=[_row_spec(), _const_spec((D_MODEL, nq + 2 * nkv)), tab, tab],
        out_specs=[_row_spec(), _row_spec(d=nkv), _row_spec(d=nkv)],
        out_shape=[jax.ShapeDtypeStruct((lay.n_tok, nq), BF16),
                   jax.ShapeDtypeStruct((lay.n_tok, nkv), BF16),
                   jax.ShapeDtypeStruct((lay.n_tok, nkv), BF16)],
        compiler_params=_cparams("arbitrary"),
        name="attn_qkv_rope",
    )(u, w_qkv, cos_t, sin_t)


def _attn_body(nqb, sink_ref, q_ref, kp_ref, kc_ref, kn_ref, kx_ref, vp_ref, vc_ref, vn_ref, vx_ref,
               o_ref, kcat, vcat):
    j = pl.program_id(1)
    blk = ATT_BLOCK
    n_win = 3 * blk
    for dst, parts in ((kcat, (kp_ref, kc_ref, kn_ref)), (vcat, (vp_ref, vc_ref, vn_ref))):
        for n, part in enumerate(parts):
            dst[n * blk:(n + 1) * blk] = part[...]
    kcat[n_win:] = kx_ref[...]
    vcat[n_win:] = vx_ref[...]
    rows = KV_REP * blk
    r = lax.broadcasted_iota(jnp.int32, (rows, n_win), 0) % blk
    c = lax.broadcasted_iota(jnp.int32, (rows, n_win), 1)
    valid = (jnp.abs(r + blk - c) <= WINDOW) & (j < nqb)
    valid = valid & ((c >= blk) | (j > 0)) & ((c < 2 * blk) | (j < nqb - 1))
    rep = lax.broadcasted_iota(jnp.int32, (rows, 1), 0) // blk
    for g in range(N_KV_HEADS):
        heads = [slice((g * KV_REP + n) * HEAD_DIM, (g * KV_REP + n + 1) * HEAD_DIM) for n in range(KV_REP)]
        kv_head = slice(g * HEAD_DIM, (g + 1) * HEAD_DIM)
        q4 = jnp.concatenate([q_ref[:, head] for head in heads], axis=0)
        s = lax.dot_general(q4, kcat[:, kv_head], (((1,), (1,)), ((), ())), preferred_element_type=F32)
        sink = jnp.zeros((rows, 1), F32)
        for n in range(KV_REP):
            sink = jnp.where(rep == n, sink_ref[0, g * KV_REP + n], sink)
        sw = jnp.where(valid, s[:, :n_win], -jnp.inf)
        sc = s[:, n_win:]
        m = jnp.maximum(jnp.max(sw, axis=-1, keepdims=True), jnp.max(sc, axis=-1, keepdims=True))
        m = jnp.maximum(m, sink)
        pw = jnp.exp(sw - m)
        pc = jnp.exp(sc - m)
        denom = (jnp.sum(pw, axis=-1, keepdims=True) + jnp.sum(pc, axis=-1, keepdims=True)
                 + jnp.exp(sink - m))
        acc = jnp.dot(pw.astype(BF16), vcat[:n_win, kv_head], preferred_element_type=F32)
        acc = acc + jnp.dot(pc.astype(BF16), vcat[n_win:, kv_head], preferred_element_type=F32)
        out = (acc / denom).astype(o_ref.dtype)
        for n, head in enumerate(heads):
            o_ref[:, head] = out[n * blk:(n + 1) * blk]


def _attention(lay, q, k, v, sink, with_ctx_out):
    nq, nkv = N_HEADS * HEAD_DIM, N_KV_HEADS * HEAD_DIM
    blk = ATT_BLOCK
    nqb = lay.seq // blk
    lc = lay.ctx_len
    assert lay.n_lat % lc == 0 and lay.seq % blk == 0 and lc % blk == 0
    ctx0 = lay.n_lat // lc
    ncb = lc // blk
    steps = nqb + (ncb if with_ctx_out else 0)
    q_blk = pl.BlockSpec(
        (blk, nq), lambda b, j: (jnp.where(j < nqb, b * nqb + j, (ctx0 + b) * ncb + j - nqb), 0))
    kv_blk = lambda shift: pl.BlockSpec(
        (blk, nkv), lambda b, j: (b * nqb + jnp.clip(j + shift, 0, nqb - 1), 0))
    kv_ctx = pl.BlockSpec((lc, nkv), lambda b, j: (ctx0 + b, 0))
    return pl.pallas_call(
        functools.partial(_attn_body, nqb),
        grid=(lay.batch, steps),
        in_specs=[pl.BlockSpec(memory_space=pltpu.SMEM), q_blk,
                  kv_blk(-1), kv_blk(0), kv_blk(1), kv_ctx,
                  kv_blk(-1), kv_blk(0), kv_blk(1), kv_ctx],
        out_specs=q_blk,
        out_shape=jax.ShapeDtypeStruct((lay.n_tok if with_ctx_out else lay.n_lat, nq), BF16),
        scratch_shapes=[pltpu.VMEM((3 * blk + lc, nkv), BF16), pltpu.VMEM((3 * blk + lc, nkv), BF16)],
        compiler_params=_cparams("arbitrary", "arbitrary"),
        name="attn_window",
    )(sink, q, k, k, k, k, v, v, v, v)


def _pw1_body(u_ref, wa_ref, wg_ref, ba_ref, bg_ref, z_ref):
    u = u_ref[...]
    a = jnp.dot(u, wa_ref[...], preferred_element_type=F32) + ba_ref[...]
    g = jnp.dot(u, wg_ref[...], preferred_element_type=F32) + bg_ref[...]
    z_ref[...] = (a * jax.nn.sigmoid(g)).astype(z_ref.dtype)


def _conv_pw1(lay, u, wa, wg, ba, bg):
    return pl.pallas_call(
        _pw1_body,
        grid=(lay.tiles,),
        in_specs=[_row_spec(), _const_spec((D_MODEL, D_MODEL)), _const_spec((D_MODEL, D_MODEL)),
                  _const_spec((1, D_MODEL)), _const_spec((1, D_MODEL))],
        out_specs=_row_spec(),
        out_shape=jax.ShapeDtypeStruct((lay.n_tok, D_MODEL), BF16),
        compiler_params=_cparams("arbitrary"),
        name="conv_pw1_glu",
    )(u, wa, wg, ba, bg)


def _conv_body(lat_tiles, tps, tpc, zp_ref, z_ref, zn_ref, wdw_ref, bdw_ref, cg_ref, cb_ref, w2_ref, b2_ref,
               h_ref, mod_ref, lng_ref, lnb_ref, wr_ref, br_ref, h1_ref, v_ref, meta_ref, cnt_ref, win, shifted):
    i = pl.program_id(0)
    tm = z_ref.shape[0]
    halo = CONV_HALO
    is_ctx = i >= lat_tiles
    pos = jnp.where(is_ctx, (i - lat_tiles) % tpc, i % tps)
    first = pos == 0
    last = pos == jnp.where(is_ctx, tpc, tps) - 1
    win[0:halo] = jnp.where(first, 0.0, zp_ref[...].astype(F32))
    win[halo:halo + tm] = z_ref[...].astype(F32)
    win[halo + tm:] = jnp.where(last, 0.0, zn_ref[...].astype(F32))
    off = halo - CONV_WIDTH // 2
    span = _round_up(off + CONV_WIDTH - 1, SUBLANES) - SUBLANES
    conv = jnp.zeros((tm, D_MODEL), F32)
    for res in range(SUBLANES):
        taps = [o - off for o in range(res, off + CONV_WIDTH, SUBLANES) if o >= off]
        if not taps:
            continue
        shifted[...] = win[res:res + tm + span, :]
        for tap in taps:
            base = off + tap - res
            conv = conv + wdw_ref[tap:tap + 1, :] * shifted[base:base + tm, :]
    conv = conv + bdw_ref[...]
    nrm = _layer_norm_rows(conv, cg_ref[...], cb_ref[...])
    act = (nrm * jax.nn.sigmoid(nrm)).astype(BF16)
    y = jnp.dot(act, w2_ref[...], preferred_element_type=F32) + b2_ref[...]
    _finish_route(y, h_ref, mod_ref, lng_ref, lnb_ref, wr_ref, br_ref, h1_ref, v_ref, meta_ref, cnt_ref)


def _conv_out(lay, n_tiles, layer, z, w_dw, b_dw, cg, cb, w2, b2, h, mods, lng, lnb, wr, br):
    per_tile = TOK_TILE // CONV_HALO
    n_halo = lay.n_tok // CONV_HALO
    shapes, specs = _route_outputs(n_tiles * TOK_TILE)
    halo_spec = lambda f: pl.BlockSpec((CONV_HALO, D_MODEL), lambda i: (jnp.clip(f(i), 0, n_halo - 1), 0))
    return pl.pallas_call(
        functools.partial(_conv_body, lay.lat_tiles, lay.tiles_per_seq, lay.tiles_per_ctx),
        grid=(n_tiles,),
        in_specs=[halo_spec(lambda i: i * per_tile - 1), _row_spec(), halo_spec(lambda i: (i + 1) * per_tile),
                  _const_spec((CONV_WIDTH, D_MODEL)), _const_spec((1, D_MODEL)), _const_spec((1, D_MODEL)),
                  _const_spec((1, D_MODEL)), _const_spec((D_MODEL, D_MODEL)), _const_spec((1, D_MODEL))]
                 + _route_inputs(lay, layer),
        out_specs=specs,
        out_shape=shapes,
        scratch_shapes=[pltpu.VMEM((TOK_TILE + 2 * CONV_HALO, D_MODEL), F32),
                        pltpu.VMEM((TOK_TILE + 2 * CONV_HALO - SUBLANES, D_MODEL), F32)],
        compiler_params=_cparams("arbitrary"),
        name="conv_dw_out_route",
    )(z, z, z, w_dw, b_dw, cg, cb, w2, b2, h, mods, lng, lnb, wr, br)


def _iota(shape, axis):
    return lax.broadcasted_iota(jnp.int32, shape, axis)


def _s5_prep_body(lrow_ref, lcol_ref, bt_ref, ct_ref, kk_ref, min_ref, mout_ref, arec_ref):
    tc, ch, ns = S5_CHUNK, SSM_CH_PER_GROUP, SSM_STATE
    kw = 2 * tc
    lrow = lrow_ref[0]
    lcol = lcol_ref[0]
    exact_dot = functools.partial(jnp.dot, preferred_element_type=F32, precision=HIGHEST)
    t_lane = _iota((1, tc), 1).astype(F32)
    s_row = _iota((tc, 1), 0).astype(F32)
    lag_lane = _iota((1, kw), 1) - (tc - 1)
    mout_ref[...] = jnp.zeros_like(mout_ref)
    kk = jnp.zeros((ch, ch * kw), F32)
    a_re, a_im = [], []
    for d in range(2):
        lr2 = jnp.minimum(lrow[3 * d:3 * d + 1], -1e-4)
        li2 = lrow[3 * d + 1:3 * d + 2]
        dt2 = jnp.exp(lrow[3 * d + 2:3 * d + 3])
        lr, li, dt = lr2[:, :ns], li2[:, :ns], dt2[:, :ns]
        lrc = jnp.minimum(lcol[:, 3 * d:3 * d + 1], -1e-4)
        lic = lcol[:, 3 * d + 1:3 * d + 2]
        dtc = jnp.exp(lcol[:, 3 * d + 2:3 * d + 3])
        er = jnp.exp(lr * dt)
        xr = er * jnp.cos(li * dt) - 1.0
        xi = er * jnp.sin(li * dt)
        den = lr * lr + li * li
        qr = (xr * lr + xi * li) / den
        qi = (xi * lr - xr * li) / den
        btr, bti = bt_ref[0, 2 * d], bt_ref[0, 2 * d + 1]
        bbr = qr * btr - qi * bti
        bbi = qr * bti + qi * btr
        n_in = (tc - 1.0 - s_row) if d == 0 else s_row
        mg = jnp.exp(n_in * (lr * dt))
        pr, pi = mg * jnp.cos(n_in * (li * dt)), mg * jnp.sin(n_in * (li * dt))
        re0, im0 = ns * d, 2 * ns + ns * d
        for c in range(ch):
            br = jnp.broadcast_to(bbr[c:c + 1], (tc, ns))
            bi = jnp.broadcast_to(bbi[c:c + 1], (tc, ns))
            min_ref[0, tc * c:tc * (c + 1), re0:re0 + ns] = (pr * br - pi * bi).astype(BF16)
            min_ref[0, tc * c:tc * (c + 1), im0:im0 + ns] = (pr * bi + pi * br).astype(BF16)

        ctr, cti = ct_ref[0, 2 * d], ct_ref[0, 2 * d + 1]

        def readout(n_of_lane, keep=None):
            width = n_of_lane.shape[1]
            mag = jnp.exp(n_of_lane * (lrc * dtc))
            w_re, w_im = mag * jnp.cos(n_of_lane * (lic * dtc)), mag * jnp.sin(n_of_lane * (lic * dtc))
            if keep is not None:
                w_re, w_im = jnp.where(keep, w_re, 0.0), jnp.where(keep, w_im, 0.0)
            res_re, res_im = [], []
            for c in range(ch):
                cr = jnp.broadcast_to(ctr[:, c:c + 1], (ns, width))
                ci = jnp.broadcast_to(cti[:, c:c + 1], (ns, width))
                res_re.append(cr * w_re - ci * w_im)
                res_im.append(-(cr * w_im + ci * w_re))
            return jnp.concatenate(res_re, axis=1), jnp.concatenate(res_im, axis=1)

        o_re, o_imneg = readout(t_lane + 1.0 if d == 0 else tc - t_lane)
        base = 4 * ns * d
        mout_ref[0, base + re0:base + re0 + ns, :] = o_re.astype(BF16)
        mout_ref[0, base + im0:base + im0 + ns, :] = o_imneg.astype(BF16)
        lag = lag_lane if d == 0 else -lag_lane
        k_re, k_imneg = readout(jnp.maximum(lag, 0).astype(F32), lag >= 0)
        kk = kk + exact_dot(bbr, k_re) + exact_dot(bbi, k_imneg)
        e2 = jnp.exp(tc * lr2 * dt2)
        a_re.append(e2 * jnp.cos(tc * li2 * dt2))
        a_im.append(e2 * jnp.sin(tc * li2 * dt2))
    kk_ref[0] = kk
    is_fwd = _iota((1, 2 * ns), 1) < ns
    arec_ref[0, 0:1] = jnp.where(is_fwd, a_re[0], a_re[1])
    arec_ref[0, 1:2] = jnp.where(is_fwd, a_im[0], a_im[1])


def _s5_prepare(lam_re, lam_im, log_dt, b_re, b_im, c_re, c_im):
    g, ns, ch, tc = SSM_GROUPS, SSM_STATE, SSM_CH_PER_GROUP, S5_CHUNK
    ldt = jnp.broadcast_to(log_dt[:, :, None], lam_re.shape)
    stack = jnp.stack([lam_re[0], lam_im[0], ldt[0], lam_re[1], lam_im[1], ldt[1]], axis=1)
    lrow = jnp.concatenate([stack, stack], axis=-1)
    lcol = jnp.swapaxes(stack, 1, 2)
    bt = jnp.stack([b_re[0], b_im[0], b_re[1], b_im[1]], axis=1).swapaxes(2, 3)
    ct = jnp.stack([c_re[0], c_im[0], c_re[1], c_im[1]], axis=1).swapaxes(2, 3)
    blk = lambda *s: pl.BlockSpec((1,) + s, lambda i: (i,) + (0,) * len(s))
    return pl.pallas_call(
        _s5_prep_body,
        grid=(g,),
        in_specs=[blk(6, 2 * ns), blk(ns, 6), blk(4, ch, ns), blk(4, ns, ch)],
        out_specs=[blk(ch, 2 * tc * ch), blk(tc * ch, 4 * ns), blk(8 * ns, tc * ch), blk(2, 2 * ns)],
        out_shape=[jax.ShapeDtypeStruct((g, ch, 2 * tc * ch), F32),
                   jax.ShapeDtypeStruct((g, tc * ch, 4 * ns), BF16),
                   jax.ShapeDtypeStruct((g, 8 * ns, tc * ch), BF16),
                   jax.ShapeDtypeStruct((g, 2, 2 * ns), F32)],
        compiler_params=_cparams("arbitrary"),
        name="s5_prepare",
    )(lrow, lcol, bt, ct)


def _s5_core_body(nb, nk, nkc, x_ref, kk_ref, min_ref, mout_ref, arec_ref, y_ref,
                  trow_a, trow_b, yacc, s_ref, h_ref):
    tc, ch, ns = S5_CHUNK, SSM_CH_PER_GROUP, SSM_STATE
    ns2, kw = 2 * ns, 2 * tc
    n_pairs = ch // 2

    def plane_pair(i):
        return jnp.concatenate([x_ref[2 * i], x_ref[2 * i + 1]], axis=1)

    def add_state(i, acc):
        rows = pl.ds(pl.multiple_of(i * kw, kw), kw)
        return acc + jnp.dot(plane_pair(i), min_ref[0, rows, :], preferred_element_type=F32)
    contrib = lax.fori_loop(0, n_pairs, add_state, jnp.zeros((nb * nk, 2 * ns2), F32))
    s_ref[0] = contrib[:, :ns2]
    s_ref[1] = contrib[:, ns2:]

    arec = arec_ref[0]
    a_re, a_im = arec[0:1], arec[1:2]
    is_fwd = _iota((nb, ns2), 1) < ns

    def step(i, carry):
        h_re, h_im = carry
        kr = jnp.where(i < nkc, nkc - 1 - i, nk - 1 - (i - nkc))
        fwd_rows = pl.ds(i, nb, stride=nk)
        rev_rows = pl.ds(kr, nb, stride=nk)
        h_ref[0, 0, fwd_rows, :] = h_re
        h_ref[0, 1, fwd_rows, :] = h_im
        h_ref[1, 0, rev_rows, :] = h_re
        h_ref[1, 1, rev_rows, :] = h_im
        s_re = jnp.where(is_fwd, s_ref[0, fwd_rows, :], s_ref[0, rev_rows, :])
        s_im = jnp.where(is_fwd, s_ref[1, fwd_rows, :], s_ref[1, rev_rows, :])
        return a_re * h_re - a_im * h_im + s_re, a_re * h_im + a_im * h_re + s_im

    zero = jnp.zeros((nb, ns2), F32)
    lax.fori_loop(0, nk, step, (zero, zero))

    y0 = jnp.zeros((nb * nk, ch * tc), F32)
    for d in range(2):
        hd = jnp.concatenate([h_ref[d, 0], h_ref[d, 1]], axis=1).astype(BF16)
        y0 = y0 + jnp.dot(hd, mout_ref[0, 2 * ns2 * d:2 * ns2 * (d + 1), :], preferred_element_type=F32)
    yacc[...] = y0

    def make_blocks(i, dst):
        for cc in range(2):
            lag_row = kk_ref[0, pl.ds(2 * i + cc, 1), :]
            for c in range(ch):
                lag = jnp.broadcast_to(lag_row[:, kw * c:kw * (c + 1)], (tc, kw))
                block = pltpu.roll(lag, tc + 1, 1, stride=1, stride_axis=0)[:, :tc]
                dst[tc * cc:tc * (cc + 1), tc * c:tc * (c + 1)] = block.astype(BF16)

    make_blocks(0, trow_a)

    def add_two_pairs(i2, carry):
        i = 2 * i2
        make_blocks(i + 1, trow_b)
        yacc[...] += jnp.dot(plane_pair(i), trow_a[...], preferred_element_type=F32)
        make_blocks(jnp.minimum(i + 2, n_pairs - 1), trow_a)
        yacc[...] += jnp.dot(plane_pair(i + 1), trow_b[...], preferred_element_type=F32)
        return carry
    lax.fori_loop(0, n_pairs // 2, add_two_pairs, 0)
    for c in range(ch):
        y_ref[c] = yacc[:, tc * c:tc * (c + 1)].astype(y_ref.dtype)


def _s5_core(lay, u, mats):
    kk, m_in, m_out, arec = mats
    g, ch, tc, ns = SSM_GROUPS, SSM_CH_PER_GROUP, S5_CHUNK, SSM_STATE
    b = lay.batch
    assert lay.seq % tc == 0 and lay.ctx_len % tc == 0
    nkl, nkc = lay.seq // tc, lay.ctx_len // tc
    nk = nkl + nkc
    rows = b * nk
    x = u.reshape(rows, tc, D_MODEL).transpose(2, 0, 1)
    blk = lambda *s: pl.BlockSpec((1,) + s, lambda i: (i,) + (0,) * len(s))
    planes = pl.BlockSpec((ch, rows, tc), lambda i: (i, 0, 0))
    y = pl.pallas_call(
        functools.partial(_s5_core_body, b, nk, nkc),
        grid=(g,),
        in_specs=[planes, blk(ch, 2 * tc * ch), blk(tc * ch, 4 * ns), blk(8 * ns, tc * ch), blk(2, 2 * ns)],
        out_specs=planes,
        out_shape=jax.ShapeDtypeStruct((D_MODEL, rows, tc), BF16),
        scratch_shapes=[pltpu.VMEM((2 * tc, ch * tc), BF16), pltpu.VMEM((2 * tc, ch * tc), BF16),
                        pltpu.VMEM((rows, ch * tc), F32),
                        pltpu.VMEM((2, rows, 2 * ns), F32), pltpu.VMEM((2, 2, rows, 2 * ns), F32)],
        compiler_params=_cparams("arbitrary"),
        name="s5_scan",
    )(x, kk, m_in, m_out, arec)
    return y.transpose(1, 2, 0).reshape(rows * tc, D_MODEL)


def _s5_out_body(y_ref, u_ref, d_ref, wv_ref, wg_ref, *rest):
    u = u_ref[...].astype(F32)
    act = _gelu_tanh(y_ref[...].astype(F32) + d_ref[...] * u).astype(BF16)
    val = jnp.dot(act, wv_ref[...], preferred_element_type=F32)
    gate = jnp.dot(act, wg_ref[...], preferred_element_type=F32)
    _finish_route(val * jax.nn.sigmoid(gate), *rest)


def _s5_out(lay, n_tiles, layer, y, u, d, wv, wg, h, mods, lng, lnb, wr, br):
    shapes, specs = _route_outputs(n_tiles * TOK_TILE)
    return pl.pallas_call(
        _s5_out_body,
        grid=(n_tiles,),
        in_specs=[lay.s5_row_spec(), lay.s5_row_spec(), _const_spec((1, D_MODEL)),
                  _const_spec((D_MODEL, D_MODEL)), _const_spec((D_MODEL, D_MODEL))] + _route_inputs(lay, layer),
        out_specs=specs,
        out_shape=shapes,
        compiler_params=_cparams("arbitrary"),
        name="s5_out_route",
    )(y, u, d, wv, wg, h, mods, lng, lnb, wr, br)


def kernel(x, c, ctx, c_ctx, ada_w, ada_b, ln_g, ln_b, s5_lam_re, s5_lam_im, s5_log_dt, s5_b_re, s5_b_im, s5_c_re, s5_c_im, s5_d, s5_w_glu, cv_w_pw1, cv_b_pw1, cv_w_dw, cv_b_dw, cv_ln_g, cv_ln_b, cv_w_pw2, cv_b_pw2, at_w_qkv, at_w_o, at_sink, moe_wg, moe_bg, moe_we, moe_be, moe_w1, moe_w2):
    batch, seq, d = x.shape
    lay = _Layout(batch, seq, ctx.shape[1])
    depth = ada_w.shape[0]
    row = lambda a: a.reshape(1, -1)

    c_all =jnp.concatenate([c, c_ctx[None], jnp.zeros((lay.mod_rows - batch - 1, d), F32)], axis=0)
    mods = _modulation(c_all, ada_w, ada_b)
    cos_t, sin_t = _rope_tables(seq)
    ng, ne = N_EXPERT_GROUPS, N_EXPERTS
    pad = jnp.zeros((d, ROUTE_LANES - ng - ne), F32)

    u_spec = lambda layer: lay.s5_row_spec() if layer % 3 == 0 else _row_spec()
    h, u = _modulate(lay, x.reshape(lay.n_lat, d), ctx.reshape(lay.n_ctx, d), mods, 0, u_spec(0))
    for i in range(depth):
        last = i == depth - 1
        kind, j = i % 3, i // 3
        n_tiles = lay.lat_tiles if last else lay.tiles
        wr = _split_bf16(jnp.concatenate([moe_wg[i], moe_we[i], pad], axis=1))
        br = jnp.concatenate([moe_bg[i], moe_be[i], pad[0]], axis=0)[None]
        route_args = (h, mods, row(ln_g[i, 0]), row(ln_b[i, 0]), wr, br)
        if kind == 0:
            mats = _s5_prepare(s5_lam_re[j], s5_lam_im[j], s5_log_dt[j], s5_b_re[j], s5_b_im[j],
                               s5_c_re[j], s5_c_im[j])
            y = _s5_core(lay, u, mats)
            wglu = s5_w_glu[j].astype(BF16)
            h1, v, meta, cnt = _s5_out(lay, n_tiles, i, y, u, row(s5_d[j]), wglu[:, :d], wglu[:, d:], *route_args)
        elif kind == 1:
            w1 = cv_w_pw1[j].astype(BF16)
            z = _conv_pw1(lay, u, w1[:, :d], w1[:, d:], row(cv_b_pw1[j, :d]), row(cv_b_pw1[j, d:]))
            h1, v, meta, cnt = _conv_out(lay, n_tiles, i, z, cv_w_dw[j], row(cv_b_dw[j]), row(cv_ln_g[j]),
                                         row(cv_ln_b[j]), cv_w_pw2[j].astype(BF16), row(cv_b_pw2[j]),
                                         *route_args)
        else:
            q, k, vv = _qkv_rope(lay, u, at_w_qkv[j].astype(BF16), cos_t, sin_t)
            o = _attention(lay, q, k, vv, row(at_sink[j]), not last)
            h1, v, meta, cnt = _attn_out(lay, n_tiles, i, o, at_w_o[j].astype(BF16), *route_args)
        plan = _moe_plan(meta, cnt, n_tiles)
        xs = _moe_dispatch(v, plan)
        ys = _moe_experts(xs, moe_w1, moe_w2, i, plan.items)
        h, u = _moe_combine(lay, n_tiles, ys, plan, h1, mods, i, row(ln_g[i, 1]), row(ln_b[i, 1]),
                            None if last else u_spec(i + 1))
    return h.reshape(batch, seq, d)
```

```python
import functools
import math

import jax
import jax.numpy as jnp
from jax import lax
from jax.experimental import pallas as pl
from jax.experimental.pallas import tpu as pltpu

F32 = jnp.float32
BF16 = jnp.bfloat16
HIGHEST = lax.Precision.HIGHEST

D_MODEL = 1024
DEPTH = 4
GRID_W = 64
SSM_CH_PER_GROUP = 16
SSM_GROUPS = D_MODEL // SSM_CH_PER_GROUP
SSM_STATE = 64
CONV_WIDTH = 31
HEAD_DIM = 64
N_HEADS = D_MODEL // HEAD_DIM
N_KV_HEADS = N_HEADS // 4
KV_REP = N_HEADS // N_KV_HEADS
WINDOW = 128
ATT_BLOCK = 128
ROPE_BASE = 10000.0
N_EXPERT_GROUPS = 4
EXPERTS_PER_GROUP = 8
N_EXPERTS = N_EXPERT_GROUPS * EXPERTS_PER_GROUP
D_EXPERT = D_MODEL // 2
ALPHA = (2 * DEPTH) ** 0.25
LN_EPS = 1e-5

SUBLANES = 8
LANES = 128

TOK_TILE = 256
EXPERT_TILE = 512
S5_CHUNK = LANES
CONV_HALO = 16
ROUTE_LANES = LANES
META_COLS = 8
RUN_PAD = SUBLANES
LOCAL_ROWS = -(-(2 * TOK_TILE + N_EXPERTS * (RUN_PAD - 1)) // LANES) * LANES
MAX_PIECES = LOCAL_ROWS // RUN_PAD
VMEM_LIMIT = 56 * 1024 * 1024


def _cparams(*sem):
    return pltpu.CompilerParams(dimension_semantics=sem, vmem_limit_bytes=VMEM_LIMIT)


def _round_up(n, m):
    return (n + m - 1) // m * m


def _split_bf16(w):
    hi = w.astype(BF16)
    return jnp.stack([hi, (w - hi.astype(F32)).astype(BF16)])


def _mod_body(c_ref, w_ref, b_ref, o_ref):
    c = c_ref[...]
    s = c * jax.nn.sigmoid(c)
    o_ref[0] = jnp.dot(s, w_ref[0], preferred_element_type=F32, precision=HIGHEST) + b_ref[0]


def _modulation(c_all, ada_w, ada_b):
    depth, d, n = ada_w.shape
    r = c_all.shape[0]
    tn = 1024
    out = pl.pallas_call(
        _mod_body,
        grid=(depth, n // tn),
        in_specs=[
            pl.BlockSpec((r, d), lambda i, j: (0, 0)),
            pl.BlockSpec((1, d, tn), lambda i, j: (i, 0, j)),
            pl.BlockSpec((1, 1, tn), lambda i, j: (i, 0, j)),
        ],
        out_specs=pl.BlockSpec((1, r, tn), lambda i, j: (i, 0, j)),
        out_shape=jax.ShapeDtypeStruct((depth, r, n), F32),
        compiler_params=_cparams("arbitrary", "arbitrary"),
        name="adaln_modulation",
    )(c_all, ada_w, ada_b.reshape(depth, 1, n))
    return out.reshape(depth, r, 6, d)


def _layer_norm_rows(t, g, b):
    mu = jnp.mean(t, axis=-1, keepdims=True)
    dev = t - mu
    var = jnp.mean(dev * dev, axis=-1, keepdims=True)
    return dev * lax.rsqrt(var + LN_EPS) * g + b


def _post_norm_and_route(h, y, mod, lng, lnb, wr, br):
    tm = h.shape[0]
    ng, ne = N_EXPERT_GROUPS, N_EXPERTS
    h1 = _layer_norm_rows(ALPHA * h + mod[2:3] * y, lng, lnb)
    v = h1 * (1.0 + mod[4:5]) + mod[3:4]
    v_hi = v.astype(BF16)
    v_lo = (v - v_hi.astype(F32)).astype(BF16)
    logits = (jnp.dot(v_hi, wr[0], preferred_element_type=F32) + jnp.dot(v_lo, wr[0], preferred_element_type=F32)
              + jnp.dot(v_hi, wr[1], preferred_element_type=F32) + br)

    lane = lax.broadcasted_iota(jnp.int32, (tm, ROUTE_LANES), 1)
    lane_f = lane.astype(F32)
    neg = -jnp.inf
    no_lane = float(ROUTE_LANES)
    is_group = lane < ng
    gl = jnp.where(is_group, logits, neg)
    gmax = jnp.max(gl, axis=-1, keepdims=True)
    gsum = jnp.sum(jnp.where(is_group, jnp.exp(logits - gmax), 0.0), axis=-1, keepdims=True)
    p_group = 1.0 / gsum
    g_idx = jnp.min(jnp.where(gl == gmax, lane_f, no_lane), axis=-1, keepdims=True)
    expert_group = ((lane - ng) // EXPERTS_PER_GROUP).astype(F32)
    in_group = (lane >= ng) & (lane < ng + ne) & (expert_group == g_idx)
    el = jnp.where(in_group, logits, neg)
    v1 = jnp.max(el, axis=-1, keepdims=True)
    i1 = jnp.min(jnp.where(el == v1, lane_f, no_lane), axis=-1, keepdims=True)
    el2 = jnp.where(lane_f == i1, neg, el)
    v2 = jnp.max(el2, axis=-1, keepdims=True)
    i2 = jnp.min(jnp.where(el2 == v2, lane_f, no_lane), axis=-1, keepdims=True)
    e21 = jnp.exp(v2 - v1)
    w1 = p_group / (1.0 + e21)
    w2 = p_group * e21 / (1.0 + e21)

    hit1 = lane_f == i1
    hit2 = lane_f == i2
    one1 = hit1.astype(BF16)
    one2 = hit2.astype(BF16)
    rr = lax.broadcasted_iota(jnp.int32, (tm, tm), 0)
    cc = lax.broadcasted_iota(jnp.int32, (tm, tm), 1)
    before = (cc < rr).astype(BF16)
    cum1 = jnp.dot(before, one1, preferred_element_type=F32)
    cum2 = jnp.dot(before, one2, preferred_element_type=F32)
    tot1 = jnp.sum(one1.astype(F32), axis=0, keepdims=True)
    tot2 = jnp.sum(one2.astype(F32), axis=0, keepdims=True)
    counts = tot1 + tot2
    pieces = jnp.floor((counts + (RUN_PAD - 1.0)) * (1.0 / RUN_PAD))
    earlier = (lax.broadcasted_iota(jnp.int32, (ROUTE_LANES, ROUTE_LANES), 0)
               < lax.broadcasted_iota(jnp.int32, (ROUTE_LANES, ROUTE_LANES), 1)).astype(BF16)
    run_start = RUN_PAD * jnp.dot(jnp.broadcast_to(pieces, (SUBLANES, ROUTE_LANES)).astype(BF16), earlier,
                                  preferred_element_type=F32)[0:1]
    rank1 = jnp.sum(jnp.where(hit1, run_start + cum1, 0.0), axis=-1, keepdims=True)
    rank2 = jnp.sum(jnp.where(hit2, run_start + tot1 + cum2, 0.0), axis=-1, keepdims=True)

    col = lax.broadcasted_iota(jnp.int32, (tm, META_COLS), 1)
    meta = jnp.where(col == 0, i1 - ng,
           jnp.where(col == 1, i2 - ng,
           jnp.where(col == 2, rank1,
           jnp.where(col == 3, rank2,
           jnp.where(col == 4, w1,
           jnp.where(col == 5, w2, 0.0))))))
    return h1, v, meta, tot1 + tot2


def _gelu_tanh(x):
    return 0.5 * x * (1.0 + jnp.tanh(math.sqrt(2.0 / math.pi) * (x + 0.044715 * (x * x * x))))


class _Layout:
    def __init__(self, batch, seq, ctx_len):
        self.batch, self.seq, self.ctx_len = batch, seq, ctx_len
        self.n_lat = batch * seq
        self.n_ctx = batch * ctx_len
        self.n_tok = self.n_lat + self.n_ctx
        assert seq % TOK_TILE == 0 and ctx_len % TOK_TILE == 0
        self.lat_tiles = self.n_lat // TOK_TILE
        self.tiles = self.n_tok // TOK_TILE
        self.tiles_per_seq = seq // TOK_TILE
        self.tiles_per_ctx = ctx_len // TOK_TILE
        self.mod_rows = _round_up(batch + 1, SUBLANES)

    def mod_row(self, i):
        return jnp.minimum(i // self.tiles_per_seq, self.batch)

    def s5_tile(self, i):
        per_batch = self.tiles_per_seq + self.tiles_per_ctx
        lat = (i // self.tiles_per_seq) * per_batch + self.tiles_per_ctx + i % self.tiles_per_seq
        c = i - self.lat_tiles
        ctx = (c // self.tiles_per_ctx) * per_batch + c % self.tiles_per_ctx
        return jnp.where(i < self.lat_tiles, lat, ctx)

    def s5_row_spec(self):
        return pl.BlockSpec((TOK_TILE, D_MODEL), lambda i, *_: (self.s5_tile(i), 0))


def _mod_spec(lay, layer):
    return pl.BlockSpec((1, 1, 6, D_MODEL), lambda i, *_: (layer, lay.mod_row(i), 0, 0))


def _row_spec(tm=TOK_TILE, d=D_MODEL):
    return pl.BlockSpec((tm, d), lambda i, *_: (i, 0))


def _const_spec(shape):
    nd = len(shape)
    return pl.BlockSpec(shape, lambda i, *_: (0,) * nd)


def _modulate_body(lat_tiles, x_ref, ctx_ref, mod_ref, h_ref, u_ref):
    mod = mod_ref[0, 0]
    h = jnp.where(pl.program_id(0) < lat_tiles, x_ref[...], ctx_ref[...])
    h_ref[...] = h
    u_ref[...] = (h * (1.0 + mod[1:2]) + mod[0:1]).astype(u_ref.dtype)


def _modulate(lay, x, ctx, mods, layer, u_spec):
    ctx_tiles = lay.tiles - lay.lat_tiles
    return pl.pallas_call(
        functools.partial(_modulate_body, lay.lat_tiles),
        grid=(lay.tiles,),
        in_specs=[pl.BlockSpec((TOK_TILE, D_MODEL), lambda i: (jnp.minimum(i, lay.lat_tiles - 1), 0)),
                  pl.BlockSpec((TOK_TILE, D_MODEL), lambda i: (jnp.clip(i - lay.lat_tiles, 0, ctx_tiles - 1), 0)),
                  _mod_spec(lay, layer)],
        out_specs=[_row_spec(), u_spec],
        out_shape=[jax.ShapeDtypeStruct((lay.n_tok, D_MODEL), F32),
                   jax.ShapeDtypeStruct((lay.n_tok, D_MODEL), BF16)],
        compiler_params=_cparams("arbitrary"),
        name="input_modulate",
    )(x, ctx, mods)


PACKED = D_MODEL // 2


def _pack_rows(x):
    half = x.shape[1] // 2
    hi = lax.bitcast_convert_type(x[:, :half], jnp.uint32)
    lo = lax.bitcast_convert_type(x[:, half:], jnp.uint32)
    return hi | (lo >> 16)


def _unpack_rows(w):
    hi = lax.bitcast_convert_type(w & jnp.uint32(0xFFFF0000), F32)
    lo = lax.bitcast_convert_type(w << 16, F32)
    return hi.astype(BF16), lo.astype(BF16)


def _piece_spec(n_tiles, shift=0):
    return pl.BlockSpec((1, 1, MAX_PIECES), lambda i, *_: (jnp.minimum(i + shift, n_tiles - 1), 0, 0),
                        memory_space=pltpu.SMEM)


def _for_each_piece(rows_ref, count, fn):
    def per_piece(q, carry):
        fn(q, rows_ref[0, 0, q])
        return carry
    lax.fori_loop(0, count, per_piece, 0)


def _dispatch_body(np_ref, tail_ref, runs_ref, lp_ref, v_ref, xs_ref, loc, zeros, sem):
    i = pl.program_id(0)
    lp = lp_ref[0]
    slot_row = lax.broadcasted_iota(jnp.int32, (LOCAL_ROWS, v_ref.shape[0]), 0).astype(F32)
    pick = ((slot_row == lp[0:1]) | (slot_row == lp[1:2])).astype(BF16)
    loc[...] = _pack_rows(jnp.dot(pick, v_ref[...], preferred_element_type=F32))

    def piece(src, dst):
        return pltpu.make_async_copy(src, xs_ref.at[pl.ds(pl.multiple_of(dst, RUN_PAD), RUN_PAD)], sem)

    def issue(q, row):
        piece(loc.at[pl.ds(pl.multiple_of(q * RUN_PAD, RUN_PAD), RUN_PAD)], row).start()

    def drain(q, carry):
        piece(loc.at[pl.ds(0, RUN_PAD)], 0).wait()
        return carry

    _for_each_piece(runs_ref, np_ref[i], issue)
    lax.fori_loop(0, np_ref[i], drain, 0)

    @pl.when(i == pl.num_programs(0) - 1)
    def _():
        zeros[...] = jnp.zeros_like(zeros)

        def issue_zero(q, carry):
            piece(zeros, tail_ref[0] + q * RUN_PAD).start()
            return carry

        def drain_zero(q, carry):
            piece(zeros, 0).wait()
            return carry

        lax.fori_loop(0, tail_ref[1], issue_zero, 0)
        lax.fori_loop(0, tail_ref[1], drain_zero, 0)


def _moe_dispatch(v, plan):
    n_tiles = plan.n_pieces.shape[0]
    grid_spec = pltpu.PrefetchScalarGridSpec(
        num_scalar_prefetch=2,
        grid=(n_tiles,),
        in_specs=[_piece_spec(n_tiles),
                  pl.BlockSpec((1, 2, TOK_TILE), lambda i, *_: (i, 0, 0)),
                  _row_spec()],
        out_specs=pl.BlockSpec(memory_space=pl.ANY),
        scratch_shapes=[pltpu.VMEM((LOCAL_ROWS, PACKED), jnp.uint32), pltpu.VMEM((RUN_PAD, PACKED), jnp.uint32),
                        pltpu.SemaphoreType.DMA(())],
    )
    return pl.pallas_call(
        _dispatch_body,
        grid_spec=grid_spec,
        out_shape=jax.ShapeDtypeStruct((plan.max_rows, PACKED), jnp.uint32),
        compiler_params=_cparams("arbitrary"),
        name="moe_dispatch",
    )(plan.n_pieces, plan.tail, plan.piece_rows, plan.lp_rows, v)


def _expert_body(tile_ref, exp_ref, lo_ref, hi_ref, xs_ref, w1_ref, w2_ref, ys_ref, w1b, w2b):
    j = pl.program_id(0)
    jp = jnp.maximum(j - 1, 0)
    new_expert = (j == 0) | (exp_ref[j] != exp_ref[jp])
    first_visit = (j == 0) | (tile_ref[j] != tile_ref[jp])
    lo, hi = lo_ref[j], hi_ref[j]

    @pl.when(new_expert)
    def _():
        w1b[...] = w1_ref[0, 0].astype(BF16)
        w2b[...] = w2_ref[0, 0].astype(BF16)

    def expert_rows():
        x_a, x_b = _unpack_rows(xs_ref[...])
        gu = (jnp.dot(x_a, w1b[:PACKED], preferred_element_type=F32)
              + jnp.dot(x_b, w1b[PACKED:], preferred_element_type=F32))
        gate, up = gu[:, :D_EXPERT], gu[:, D_EXPERT:]
        a = (gate * jax.nn.sigmoid(gate) * up).astype(BF16)
        y = jnp.dot(a, w2b[...], preferred_element_type=F32)
        rows = lax.broadcasted_iota(jnp.int32, (xs_ref.shape[0], 1), 0)
        return _pack_rows(y.astype(BF16).astype(F32)), (rows >= lo) & (rows < hi)

    @pl.when(first_visit)
    def _():
        y, mine = expert_rows()
        ys_ref[...] = jnp.where(mine, y, jnp.uint32(0))

    @pl.when(jnp.logical_not(first_visit) & (hi > lo))
    def _():
        y, mine = expert_rows()
        ys_ref[...] = jnp.where(mine, y, ys_ref[...])


def _moe_experts(xs, w1, w2, layer, items):
    tile_j, exp_j, lo_j, hi_j = items
    n_items = tile_j.shape[0]
    grid_spec = pltpu.PrefetchScalarGridSpec(
        num_scalar_prefetch=4,
        grid=(n_items,),
        in_specs=[
            pl.BlockSpec((EXPERT_TILE, PACKED), lambda j, t, e, lo, hi: (t[j], 0)),
            pl.BlockSpec((1, 1, D_MODEL, 2 * D_EXPERT), lambda j, t, e, lo, hi: (layer, e[j], 0, 0)),
            pl.BlockSpec((1, 1, D_EXPERT, D_MODEL), lambda j, t, e, lo, hi: (layer, e[j], 0, 0)),
        ],
        out_specs=pl.BlockSpec((EXPERT_TILE, PACKED), lambda j, t, e, lo, hi: (t[j], 0)),
        scratch_shapes=[pltpu.VMEM((D_MODEL, 2 * D_EXPERT), BF16), pltpu.VMEM((D_EXPERT, D_MODEL), BF16)],
    )
    return pl.pallas_call(
        _expert_body,
        grid_spec=grid_spec,
        out_shape=jax.ShapeDtypeStruct(xs.shape, xs.dtype),
        compiler_params=_cparams("arbitrary"),
        name="moe_experts",
    )(tile_j, exp_j, lo_j, hi_j, xs, w1, w2)


def _combine_body(has_next, np_ref, src_ref, srcn_ref, lpw_ref, h1_ref, mod_ref, modn_ref, lng_ref, lnb_ref,
                  ys_ref, *rest):
    if has_next:
        h2_ref, u_ref, buf, sem = rest
    else:
        h2_ref, buf, sem = rest
    tm = h1_ref.shape[0]
    i = pl.program_id(0)
    n = pl.num_programs(0)
    slot = i % 2

    def piece(src, s, q):
        return pltpu.make_async_copy(ys_ref.at[pl.ds(pl.multiple_of(src, RUN_PAD), RUN_PAD)],
                                     buf.at[s, pl.ds(pl.multiple_of(q * RUN_PAD, RUN_PAD), RUN_PAD)], sem.at[s])

    def gather(rows_ref, s, count):
        _for_each_piece(rows_ref, count, lambda q, row: piece(row, s, q).start())

    @pl.when(i == 0)
    def _():
        buf[...] = jnp.zeros_like(buf)
        gather(src_ref, 0, np_ref[0])

    @pl.when(i + 1 < n)
    def _():
        gather(srcn_ref, 1 - slot, np_ref[jnp.minimum(i + 1, n - 1)])

    def drain(q, carry):
        piece(0, slot, 0).wait()
        return carry
    lax.fori_loop(0, np_ref[i], drain, 0)

    lpw = lpw_ref[...]
    lane = lax.broadcasted_iota(jnp.int32, (tm, LOCAL_ROWS), 1).astype(F32)
    wmat = jnp.where(lane == lpw[:, 0:1], lpw[:, 2:3], 0.0) + jnp.where(lane == lpw[:, 1:2], lpw[:, 3:4], 0.0)
    w_hi = wmat.astype(BF16)
    w_lo = (wmat - w_hi.astype(F32)).astype(BF16)
    f = jnp.concatenate(
        [jnp.dot(w_hi, yb, preferred_element_type=F32) + jnp.dot(w_lo, yb, preferred_element_type=F32)
         for yb in _unpack_rows(buf[slot])], axis=1)
    mod = mod_ref[0, 0]
    h2 = _layer_norm_rows(ALPHA * h1_ref[...] + mod[5:6] * f, lng_ref[...], lnb_ref[...])
    h2_ref[...] = h2
    if has_next:
        modn = modn_ref[0, 0]
        u_ref[...] = (h2 * (1.0 + modn[1:2]) + modn[0:1]).astype(u_ref.dtype)


def _moe_combine(lay, n_tiles, ys, plan, h1, mods, layer, lng, lnb, u_spec):
    has_next = u_spec is not None
    n_rows = n_tiles * TOK_TILE
    nxt = min(layer + 1, DEPTH - 1)
    out_shape = [jax.ShapeDtypeStruct((n_rows, D_MODEL), F32)]
    out_specs = [_row_spec()]
    if has_next:
        out_shape.append(jax.ShapeDtypeStruct((n_rows, D_MODEL), BF16))
        out_specs.append(u_spec)
    grid_spec = pltpu.PrefetchScalarGridSpec(
        num_scalar_prefetch=1,
        grid=(n_tiles,),
        in_specs=[
            _piece_spec(n_tiles), _piece_spec(n_tiles, 1),
            _row_spec(d=4), _row_spec(), _mod_spec(lay, layer), _mod_spec(lay, nxt),
            _const_spec((1, D_MODEL)), _const_spec((1, D_MODEL)),
            pl.BlockSpec(memory_space=pl.ANY),
        ],
        out_specs=out_specs,
        scratch_shapes=[pltpu.VMEM((2, LOCAL_ROWS, PACKED), jnp.uint32), pltpu.SemaphoreType.DMA((2,))],
    )
    outs = pl.pallas_call(
        functools.partial(_combine_body, has_next),
        grid_spec=grid_spec,
        out_shape=out_shape,
        compiler_params=_cparams("arbitrary"),
        name="moe_combine",
    )(plan.n_pieces, plan.piece_rows, plan.piece_rows, plan.lp_w, h1, mods, mods, lng, lnb, ys)
    return outs if has_next else (outs[0], None)


class _MoePlan:
    pass


def _exclusive_cumsum(a, axis):
    return jnp.cumsum(a, axis=axis) - a


def _moe_plan(meta, tile_counts, n_tiles):
    plan = _MoePlan()
    plan.max_rows = _round_up(n_tiles * (2 * TOK_TILE + N_EXPERTS * (RUN_PAD - 1)), EXPERT_TILE)
    cnt = tile_counts[:, 0, N_EXPERT_GROUPS:N_EXPERT_GROUPS + N_EXPERTS].astype(jnp.int32)
    run = (cnt + RUN_PAD - 1) // RUN_PAD * RUN_PAD
    counts = jnp.sum(run, axis=0)
    total = jnp.sum(counts)
    tail = (-total) % EXPERT_TILE
    counts = counts.at[N_EXPERTS - 1].add(tail)
    ends = jnp.cumsum(counts)
    starts = ends - counts
    run_start = starts[None, :] + _exclusive_cumsum(run, 0)
    plan.tail = jnp.stack([total, tail // RUN_PAD]).astype(jnp.int32)

    plan.lp_rows = meta[:, 2:4].reshape(n_tiles, TOK_TILE, 2).transpose(0, 2, 1)
    plan.lp_w = meta[:, 2:6]

    experts = jnp.arange(N_EXPERTS, dtype=jnp.int32)
    pieces = run // RUN_PAD
    piece_end = jnp.cumsum(pieces, axis=1)
    plan.n_pieces = piece_end[:, -1].astype(jnp.int32)
    q = jnp.arange(MAX_PIECES, dtype=jnp.int32)
    owner = jnp.minimum(jnp.sum(q[None, :, None] >= piece_end[:, None, :], axis=-1), N_EXPERTS - 1)
    pick = owner[..., None] == experts
    first = jnp.sum(jnp.where(pick, (piece_end - pieces)[:, None, :], 0), axis=-1)
    base = jnp.sum(jnp.where(pick, run_start[:, None, :], 0), axis=-1)
    rows = jnp.where(q[None, :] < plan.n_pieces[:, None], base + (q[None, :] - first) * RUN_PAD, 0)
    plan.piece_rows = rows.astype(jnp.int32).reshape(n_tiles, 1, MAX_PIECES)

    n_etiles = plan.max_rows // EXPERT_TILE
    first_tile = starts // EXPERT_TILE
    last_tile = jnp.maximum(ends - 1, 0) // EXPERT_TILE
    n_items_e = jnp.where(counts > 0, last_tile - first_tile + 1, 0)
    item_end = jnp.cumsum(n_items_e)
    item_start = item_end - n_items_e
    n_items = n_etiles + N_EXPERTS
    j = jnp.arange(n_items, dtype=jnp.int32)
    e_j = jnp.minimum(jnp.sum(j[:, None] >= item_end[None, :], axis=1), N_EXPERTS - 1).astype(jnp.int32)
    active = j < item_end[-1]
    tile_j = jnp.take(first_tile, e_j) + (j - jnp.take(item_start, e_j))
    lo = jnp.maximum(jnp.take(starts, e_j), tile_j * EXPERT_TILE) - tile_j * EXPERT_TILE
    hi = jnp.minimum(jnp.take(ends, e_j), (tile_j + 1) * EXPERT_TILE) - tile_j * EXPERT_TILE
    last = jnp.maximum(item_end[-1] - 1, 0)
    tile_j = jnp.where(active, tile_j, tile_j[last]).astype(jnp.int32)
    e_j = jnp.where(active, e_j, e_j[last]).astype(jnp.int32)
    lo = jnp.where(active, lo, 0).astype(jnp.int32)
    hi = jnp.where(active, hi, 0).astype(jnp.int32)
    plan.items = (tile_j, e_j, lo, hi)
    return plan


def _route_outputs(n_rows):
    n_tiles = n_rows // TOK_TILE
    shapes = [jax.ShapeDtypeStruct((n_rows, D_MODEL), F32),
              jax.ShapeDtypeStruct((n_rows, D_MODEL), BF16),
              jax.ShapeDtypeStruct((n_rows, META_COLS), F32),
              jax.ShapeDtypeStruct((n_tiles, 1, ROUTE_LANES), F32)]
    specs = [_row_spec(), _row_spec(), _row_spec(d=META_COLS),
             pl.BlockSpec((1, 1, ROUTE_LANES), lambda i: (i, 0, 0))]
    return shapes, specs


def _route_inputs(lay, layer):
    return [_row_spec(), _mod_spec(lay, layer), _const_spec((1, D_MODEL)), _const_spec((1, D_MODEL)),
            _const_spec((2, D_MODEL, ROUTE_LANES)), _const_spec((1, ROUTE_LANES))]


def _finish_route(y, h_ref, mod_ref, lng_ref, lnb_ref, wr_ref, br_ref, h1_ref, v_ref, meta_ref, cnt_ref):
    h1, v, meta, counts = _post_norm_and_route(h_ref[...], y, mod_ref[0, 0], lng_ref[...], lnb_ref[...],
                                               wr_ref[...], br_ref[...])
    h1_ref[...] = h1
    v_ref[...] = v.astype(v_ref.dtype)
    meta_ref[...] = meta
    cnt_ref[0] = counts


def _attn_out_body(o_ref, wo_ref, *rest):
    y = jnp.dot(o_ref[...], wo_ref[...], preferred_element_type=F32)
    _finish_route(y, *rest)


def _attn_out(lay, n_tiles, layer, o, w_o, h, mods, lng, lnb, wr, br):
    shapes, specs = _route_outputs(n_tiles * TOK_TILE)
    return pl.pallas_call(
        _attn_out_body,
        grid=(n_tiles,),
        in_specs=[_row_spec(), _const_spec((D_MODEL, D_MODEL))] + _route_inputs(lay, layer),
        out_specs=specs,
        out_shape=shapes,
        compiler_params=_cparams("arbitrary"),
        name="attn_out_route",
    )(o, w_o, h, mods, lng, lnb, wr, br)


def _rope_tables(seq):
    quarter = HEAD_DIM // 4
    inv_freq = ROPE_BASE ** (-jnp.arange(quarter, dtype=F32) / quarter)
    t = jnp.arange(seq)
    rows = (t // GRID_W).astype(F32)
    cols = (t % GRID_W).astype(F32)
    ang_r = rows[:, None] * inv_freq
    ang_c = cols[:, None] * inv_freq
    cos_h = jnp.concatenate([jnp.cos(ang_r), jnp.cos(ang_r), jnp.cos(ang_c), jnp.cos(ang_c)], axis=-1)
    sin_h = jnp.concatenate([-jnp.sin(ang_r), jnp.sin(ang_r), -jnp.sin(ang_c), jnp.sin(ang_c)], axis=-1)
    rep = LANES // HEAD_DIM
    return jnp.tile(cos_h, (1, rep)), jnp.tile(sin_h, (1, rep))


def _rope(x, cos, sin):
    quarter = HEAD_DIM // 4
    width = x.shape[1]
    n = width // LANES
    c = jnp.tile(cos, (1, n))
    s = jnp.tile(sin, (1, n))
    lane = lax.broadcasted_iota(jnp.int32, x.shape, 1)
    is_lo = (lane % (2 * quarter)) < quarter
    partner = jnp.where(is_lo, pltpu.roll(x, width - quarter, 1), pltpu.roll(x, quarter, 1))
    return x * c + partner * s


def _qkv_body(lat_tiles, u_ref, w_ref, cos_ref, sin_ref, q_ref, k_ref, v_ref):
    is_ctx = pl.program_id(0) >= lat_tiles
    qkv = jnp.dot(u_ref[...], w_ref[...], preferred_element_type=F32)
    nq, nkv = N_HEADS * HEAD_DIM, N_KV_HEADS * HEAD_DIM
    q, k, v = qkv[:, :nq], qkv[:, nq:nq + nkv], qkv[:, nq + nkv:]
    cos, sin = cos_ref[...], sin_ref[...]
    q = jnp.where(is_ctx, q, _rope(q, cos, sin))
    k = jnp.where(is_ctx, k, _rope(k, cos, sin))
    q_ref[...] = (q * HEAD_DIM ** -0.5).astype(q_ref.dtype)
    k_ref[...] = k.astype(k_ref.dtype)
    v_ref[...] = v.astype(v_ref.dtype)


def _qkv_rope(lay, u, w_qkv, cos_t, sin_t):
    nq, nkv = N_HEADS * HEAD_DIM, N_KV_HEADS * HEAD_DIM
    tab = pl.BlockSpec((TOK_TILE, LANES),
                       lambda i: (jnp.where(i < lay.lat_tiles, i % lay.tiles_per_seq, 0), 0))
    return pl.pallas_call(
        functools.partial(_qkv_body, lay.lat_tiles),
        grid=(lay.tiles,),
        in_specs=[_row_spec(), _const_spec((D_MODEL, nq + 2 * nkv)), tab, tab],
        out_specs=[_row_spec(), _row_spec(d=nkv), _row_spec(d=nkv)],
        out_shape=[jax.ShapeDtypeStruct((lay.n_tok, nq), BF16),
                   jax.ShapeDtypeStruct((lay.n_tok, nkv), BF16),
                   jax.ShapeDtypeStruct((lay.n_tok, nkv), BF16)],
        compiler_params=_cparams("arbitrary"),
        name="attn_qkv_rope",
    )(u, w_qkv, cos_t, sin_t)


def _attn_body(nqb, sink_ref, q_ref, kp_ref, kc_ref, kn_ref, kx_ref, vp_ref, vc_ref, vn_ref, vx_ref,
               o_ref, kcat, vcat):
    j = pl.program_id(1)
    blk = ATT_BLOCK
    n_win = 3 * blk
    for dst, parts in ((kcat, (kp_ref, kc_ref, kn_ref)), (vcat, (vp_ref, vc_ref, vn_ref))):
        for n, part in enumerate(parts):
            dst[n * blk:(n + 1) * blk] = part[...]
    kcat[n_win:] = kx_ref[...]
    vcat[n_win:] = vx_ref[...]
    rows = KV_REP * blk
    r = lax.broadcasted_iota(jnp.int32, (rows, n_win), 0) % blk
    c = lax.broadcasted_iota(jnp.int32, (rows, n_win), 1)
    valid = (jnp.abs(r + blk - c) <= WINDOW) & (j < nqb)
    valid = valid & ((c >= blk) | (j > 0)) & ((c < 2 * blk) | (j < nqb - 1))
    rep = lax.broadcasted_iota(jnp.int32, (rows, 1), 0) // blk
    for g in range(N_KV_HEADS):
        heads = [slice((g * KV_REP + n) * HEAD_DIM, (g * KV_REP + n + 1) * HEAD_DIM) for n in range(KV_REP)]
        kv_head = slice(g * HEAD_DIM, (g + 1) * HEAD_DIM)
        q4 = jnp.concatenate([q_ref[:, head] for head in heads], axis=0)
        s = lax.dot_general(q4, kcat[:, kv_head], (((1,), (1,)), ((), ())), preferred_element_type=F32)
        sink = jnp.zeros((rows, 1), F32)
        for n in range(KV_REP):
            sink = jnp.where(rep == n, sink_ref[0, g * KV_REP + n], sink)
        sw = jnp.where(valid, s[:, :n_win], -jnp.inf)
        sc = s[:, n_win:]
        m = jnp.maximum(jnp.max(sw, axis=-1, keepdims=True), jnp.max(sc, axis=-1, keepdims=True))
        m = jnp.maximum(m, sink)
        pw = jnp.exp(sw - m)
        pc = jnp.exp(sc - m)
        denom = (jnp.sum(pw, axis=-1, keepdims=True) + jnp.sum(pc, axis=-1, keepdims=True)
                 + jnp.exp(sink - m))
        acc = jnp.dot(pw.astype(BF16), vcat[:n_win, kv_head], preferred_element_type=F32)
        acc = acc + jnp.dot(pc.astype(BF16), vcat[n_win:, kv_head], preferred_element_type=F32)
        out = (acc / denom).astype(o_ref.dtype)
        for n, head in enumerate(heads):
            o_ref[:, head] = out[n * blk:(n + 1) * blk]


def _attention(lay, q, k, v, sink, with_ctx_out):
    nq, nkv = N_HEADS * HEAD_DIM, N_KV_HEADS * HEAD_DIM
    blk = ATT_BLOCK
    nqb = lay.seq // blk
    lc = lay.ctx_len
    assert lay.n_lat % lc == 0 and lay.seq % blk == 0 and lc % blk == 0
    ctx0 = lay.n_lat // lc
    ncb = lc // blk
    steps = nqb + (ncb if with_ctx_out else 0)
    q_blk = pl.BlockSpec(
        (blk, nq), lambda b, j: (jnp.where(j < nqb, b * nqb + j, (ctx0 + b) * ncb + j - nqb), 0))
    kv_blk = lambda shift: pl.BlockSpec(
        (blk, nkv), lambda b, j: (b * nqb + jnp.clip(j + shift, 0, nqb - 1), 0))
    kv_ctx = pl.BlockSpec((lc, nkv), lambda b, j: (ctx0 + b, 0))
    return pl.pallas_call(
        functools.partial(_attn_body, nqb),
        grid=(lay.batch, steps),
        in_specs=[pl.BlockSpec(memory_space=pltpu.SMEM), q_blk,
                  kv_blk(-1), kv_blk(0), kv_blk(1), kv_ctx,
                  kv_blk(-1), kv_blk(0), kv_blk(1), kv_ctx],
        out_specs=q_blk,
        out_shape=jax.ShapeDtypeStruct((lay.n_tok if with_ctx_out else lay.n_lat, nq), BF16),
        scratch_shapes=[pltpu.VMEM((3 * blk + lc, nkv), BF16), pltpu.VMEM((3 * blk + lc, nkv), BF16)],
        compiler_params=_cparams("arbitrary", "arbitrary"),
        name="attn_window",
    )(sink, q, k, k, k, k, v, v, v, v)


def _pw1_body(u_ref, wa_ref, wg_ref, ba_ref, bg_ref, z_ref):
    u = u_ref[...]
    a = jnp.dot(u, wa_ref[...], preferred_element_type=F32) + ba_ref[...]
    g = jnp.dot(u, wg_ref[...], preferred_element_type=F32) + bg_ref[...]
    z_ref[...] = (a * jax.nn.sigmoid(g)).astype(z_ref.dtype)


def _conv_pw1(lay, u, wa, wg, ba, bg):
    return pl.pallas_call(
        _pw1_body,
        grid=(lay.tiles,),
        in_specs=[_row_spec(), _const_spec((D_MODEL, D_MODEL)), _const_spec((D_MODEL, D_MODEL)),
                  _const_spec((1, D_MODEL)), _const_spec((1, D_MODEL))],
        out_specs=_row_spec(),
        out_shape=jax.ShapeDtypeStruct((lay.n_tok, D_MODEL), BF16),
        compiler_params=_cparams("arbitrary"),
        name="conv_pw1_glu",
    )(u, wa, wg, ba, bg)


def _conv_body(lat_tiles, tps, tpc, zp_ref, z_ref, zn_ref, wdw_ref, bdw_ref, cg_ref, cb_ref, w2_ref, b2_ref,
               h_ref, mod_ref, lng_ref, lnb_ref, wr_ref, br_ref, h1_ref, v_ref, meta_ref, cnt_ref, win, shifted):
    i = pl.program_id(0)
    tm = z_ref.shape[0]
    halo = CONV_HALO
    is_ctx = i >= lat_tiles
    pos = jnp.where(is_ctx, (i - lat_tiles) % tpc, i % tps)
    first = pos == 0
    last = pos == jnp.where(is_ctx, tpc, tps) - 1
    win[0:halo] = jnp.where(first, 0.0, zp_ref[...].astype(F32))
    win[halo:halo + tm] = z_ref[...].astype(F32)
    win[halo + tm:] = jnp.where(last, 0.0, zn_ref[...].astype(F32))
    off = halo - CONV_WIDTH // 2
    span = _round_up(off + CONV_WIDTH - 1, SUBLANES) - SUBLANES
    conv = jnp.zeros((tm, D_MODEL), F32)
    for res in range(SUBLANES):
        taps = [o - off for o in range(res, off + CONV_WIDTH, SUBLANES) if o >= off]
        if not taps:
            continue
        shifted[...] = win[res:res + tm + span, :]
        for tap in taps:
            base = off + tap - res
            conv = conv + wdw_ref[tap:tap + 1, :] * shifted[base:base + tm, :]
    conv = conv + bdw_ref[...]
    nrm = _layer_norm_rows(conv, cg_ref[...], cb_ref[...])
    act = (nrm * jax.nn.sigmoid(nrm)).astype(BF16)
    y = jnp.dot(act, w2_ref[...], preferred_element_type=F32) + b2_ref[...]
    _finish_route(y, h_ref, mod_ref, lng_ref, lnb_ref, wr_ref, br_ref, h1_ref, v_ref, meta_ref, cnt_ref)


def _conv_out(lay, n_tiles, layer, z, w_dw, b_dw, cg, cb, w2, b2, h, mods, lng, lnb, wr, br):
    per_tile = TOK_TILE // CONV_HALO
    n_halo = lay.n_tok // CONV_HALO
    shapes, specs = _route_outputs(n_tiles * TOK_TILE)
    halo_spec = lambda f: pl.BlockSpec((CONV_HALO, D_MODEL), lambda i: (jnp.clip(f(i), 0, n_halo - 1), 0))
    return pl.pallas_call(
        functools.partial(_conv_body, lay.lat_tiles, lay.tiles_per_seq, lay.tiles_per_ctx),
        grid=(n_tiles,),
        in_specs=[halo_spec(lambda i: i * per_tile - 1), _row_spec(), halo_spec(lambda i: (i + 1) * per_tile),
                  _const_spec((CONV_WIDTH, D_MODEL)), _const_spec((1, D_MODEL)), _const_spec((1, D_MODEL)),
                  _const_spec((1, D_MODEL)), _const_spec((D_MODEL, D_MODEL)), _const_spec((1, D_MODEL))]
                 + _route_inputs(lay, layer),
        out_specs=specs,
        out_shape=shapes,
        scratch_shapes=[pltpu.VMEM((TOK_TILE + 2 * CONV_HALO, D_MODEL), F32),
                        pltpu.VMEM((TOK_TILE + 2 * CONV_HALO - SUBLANES, D_MODEL), F32)],
        compiler_params=_cparams("arbitrary"),
        name="conv_dw_out_route",
    )(z, z, z, w_dw, b_dw, cg, cb, w2, b2, h, mods, lng, lnb, wr, br)


def _iota(shape, axis):
    return lax.broadcasted_iota(jnp.int32, shape, axis)


def _s5_prep_body(lrow_ref, lcol_ref, bt_ref, ct_ref, kk_ref, min_ref, mout_ref, arec_ref):
    tc, ch, ns = S5_CHUNK, SSM_CH_PER_GROUP, SSM_STATE
    kw = 2 * tc
    lrow = lrow_ref[0]
    lcol = lcol_ref[0]
    exact_dot = functools.partial(jnp.dot, preferred_element_type=F32, precision=HIGHEST)
    t_lane = _iota((1, tc), 1).astype(F32)
    s_row = _iota((tc, 1), 0).astype(F32)
    lag_lane = _iota((1, kw), 1) - (tc - 1)
    mout_ref[...] = jnp.zeros_like(mout_ref)
    kk = jnp.zeros((ch, ch * kw), F32)
    a_re, a_im = [], []
    for d in range(2):
        lr2 = jnp.minimum(lrow[3 * d:3 * d + 1], -1e-4)
        li2 = lrow[3 * d + 1:3 * d + 2]
        dt2 = jnp.exp(lrow[3 * d + 2:3 * d + 3])
        lr, li, dt = lr2[:, :ns], li2[:, :ns], dt2[:, :ns]
        lrc = jnp.minimum(lcol[:, 3 * d:3 * d + 1], -1e-4)
        lic = lcol[:, 3 * d + 1:3 * d + 2]
        dtc = jnp.exp(lcol[:, 3 * d + 2:3 * d + 3])
        er = jnp.exp(lr * dt)
        xr = er * jnp.cos(li * dt) - 1.0
        xi = er * jnp.sin(li * dt)
        den = lr * lr + li * li
        qr = (xr * lr + xi * li) / den
        qi = (xi * lr - xr * li) / den
        btr, bti = bt_ref[0, 2 * d], bt_ref[0, 2 * d + 1]
        bbr = qr * btr - qi * bti
        bbi = qr * bti + qi * btr
        n_in = (tc - 1.0 - s_row) if d == 0 else s_row
        mg = jnp.exp(n_in * (lr * dt))
        pr, pi = mg * jnp.cos(n_in * (li * dt)), mg * jnp.sin(n_in * (li * dt))
        re0, im0 = ns * d, 2 * ns + ns * d
        for c in range(ch):
            br = jnp.broadcast_to(bbr[c:c + 1], (tc, ns))
            bi = jnp.broadcast_to(bbi[c:c + 1], (tc, ns))
            min_ref[0, tc * c:tc * (c + 1), re0:re0 + ns] = (pr * br - pi * bi).astype(BF16)
            min_ref[0, tc * c:tc * (c + 1), im0:im0 + ns] = (pr * bi + pi * br).astype(BF16)

        ctr, cti = ct_ref[0, 2 * d], ct_ref[0, 2 * d + 1]

        def readout(n_of_lane, keep=None):
            width = n_of_lane.shape[1]
            mag = jnp.exp(n_of_lane * (lrc * dtc))
            w_re, w_im = mag * jnp.cos(n_of_lane * (lic * dtc)), mag * jnp.sin(n_of_lane * (lic * dtc))
            if keep is not None:
                w_re, w_im = jnp.where(keep, w_re, 0.0), jnp.where(keep, w_im, 0.0)
            res_re, res_im = [], []
            for c in range(ch):
                cr = jnp.broadcast_to(ctr[:, c:c + 1], (ns, width))
                ci = jnp.broadcast_to(cti[:, c:c + 1], (ns, width))
                res_re.append(cr * w_re - ci * w_im)
                res_im.append(-(cr * w_im + ci * w_re))
            return jnp.concatenate(res_re, axis=1), jnp.concatenate(res_im, axis=1)

        o_re, o_imneg = readout(t_lane + 1.0 if d == 0 else tc - t_lane)
        base = 4 * ns * d
        mout_ref[0, base + re0:base + re0 + ns, :] = o_re.astype(BF16)
        mout_ref[0, base + im0:base + im0 + ns, :] = o_imneg.astype(BF16)
        lag = lag_lane if d == 0 else -lag_lane
        k_re, k_imneg = readout(jnp.maximum(lag, 0).astype(F32), lag >= 0)
        kk = kk + exact_dot(bbr, k_re) + exact_dot(bbi, k_imneg)
        e2 = jnp.exp(tc * lr2 * dt2)
        a_re.append(e2 * jnp.cos(tc * li2 * dt2))
        a_im.append(e2 * jnp.sin(tc * li2 * dt2))
    kk_ref[0] = kk
    is_fwd = _iota((1, 2 * ns), 1) < ns
    arec_ref[0, 0:1] = jnp.where(is_fwd, a_re[0], a_re[1])
    arec_ref[0, 1:2] = jnp.where(is_fwd, a_im[0], a_im[1])


def _s5_prepare(lam_re, lam_im, log_dt, b_re, b_im, c_re, c_im):
    g, ns, ch, tc = SSM_GROUPS, SSM_STATE, SSM_CH_PER_GROUP, S5_CHUNK
    ldt = jnp.broadcast_to(log_dt[:, :, None], lam_re.shape)
    stack = jnp.stack([lam_re[0], lam_im[0], ldt[0], lam_re[1], lam_im[1], ldt[1]], axis=1)
    lrow = jnp.concatenate([stack, stack], axis=-1)
    lcol = jnp.swapaxes(stack, 1, 2)
    bt = jnp.stack([b_re[0], b_im[0], b_re[1], b_im[1]], axis=1).swapaxes(2, 3)
    ct = jnp.stack([c_re[0], c_im[0], c_re[1], c_im[1]], axis=1).swapaxes(2, 3)
    blk = lambda *s: pl.BlockSpec((1,) + s, lambda i: (i,) + (0,) * len(s))
    return pl.pallas_call(
        _s5_prep_body,
        grid=(g,),
        in_specs=[blk(6, 2 * ns), blk(ns, 6), blk(4, ch, ns), blk(4, ns, ch)],
        out_specs=[blk(ch, 2 * tc * ch), blk(tc * ch, 4 * ns), blk(8 * ns, tc * ch), blk(2, 2 * ns)],
        out_shape=[jax.ShapeDtypeStruct((g, ch, 2 * tc * ch), F32),
                   jax.ShapeDtypeStruct((g, tc * ch, 4 * ns), BF16),
                   jax.ShapeDtypeStruct((g, 8 * ns, tc * ch), BF16),
                   jax.ShapeDtypeStruct((g, 2, 2 * ns), F32)],
        compiler_params=_cparams("arbitrary"),
        name="s5_prepare",
    )(lrow, lcol, bt, ct)


def _s5_core_body(nb, nk, nkc, x_ref, kk_ref, min_ref, mout_ref, arec_ref, y_ref,
                  trow_a, trow_b, yacc, s_ref, h_ref):
    tc, ch, ns = S5_CHUNK, SSM_CH_PER_GROUP, SSM_STATE
    ns2, kw = 2 * ns, 2 * tc
    n_pairs = ch // 2

    def plane_pair(i):
        return jnp.concatenate([x_ref[2 * i], x_ref[2 * i + 1]], axis=1)

    def add_state(i, acc):
        rows = pl.ds(pl.multiple_of(i * kw, kw), kw)
        return acc + jnp.dot(plane_pair(i), min_ref[0, rows, :], preferred_element_type=F32)
    contrib = lax.fori_loop(0, n_pairs, add_state, jnp.zeros((nb * nk, 2 * ns2), F32))
    s_ref[0] = contrib[:, :ns2]
    s_ref[1] = contrib[:, ns2:]

    arec = arec_ref[0]
    a_re, a_im = arec[0:1], arec[1:2]
    is_fwd = _iota((nb, ns2), 1) < ns

    def step(i, carry):
        h_re, h_im = carry
        kr = jnp.where(i < nkc, nkc - 1 - i, nk - 1 - (i - nkc))
        fwd_rows = pl.ds(i, nb, stride=nk)
        rev_rows = pl.ds(kr, nb, stride=nk)
        h_ref[0, 0, fwd_rows, :] = h_re
        h_ref[0, 1, fwd_rows, :] = h_im
        h_ref[1, 0, rev_rows, :] = h_re
        h_ref[1, 1, rev_rows, :] = h_im
        s_re = jnp.where(is_fwd, s_ref[0, fwd_rows, :], s_ref[0, rev_rows, :])
        s_im = jnp.where(is_fwd, s_ref[1, fwd_rows, :], s_ref[1, rev_rows, :])
        return a_re * h_re - a_im * h_im + s_re, a_re * h_im + a_im * h_re + s_im

    zero = jnp.zeros((nb, ns2), F32)
    lax.fori_loop(0, nk, step, (zero, zero))

    y0 = jnp.zeros((nb * nk, ch * tc), F32)
    for d in range(2):
        hd = jnp.concatenate([h_ref[d, 0], h_ref[d, 1]], axis=1).astype(BF16)
        y0 = y0 + jnp.dot(hd, mout_ref[0, 2 * ns2 * d:2 * ns2 * (d + 1), :], preferred_element_type=F32)
    yacc[...] = y0

    def make_blocks(i, dst):
        for cc in range(2):
            lag_row = kk_ref[0, pl.ds(2 * i + cc, 1), :]
            for c in range(ch):
                lag = jnp.broadcast_to(lag_row[:, kw * c:kw * (c + 1)], (tc, kw))
                block = pltpu.roll(lag, tc + 1, 1, stride=1, stride_axis=0)[:, :tc]
                dst[tc * cc:tc * (cc + 1), tc * c:tc * (c + 1)] = block.astype(BF16)

    make_blocks(0, trow_a)

    def add_two_pairs(i2, carry):
        i = 2 * i2
        make_blocks(i + 1, trow_b)
        yacc[...] += jnp.dot(plane_pair(i), trow_a[...], preferred_element_type=F32)
        make_blocks(jnp.minimum(i + 2, n_pairs - 1), trow_a)
        yacc[...] += jnp.dot(plane_pair(i + 1), trow_b[...], preferred_element_type=F32)
        return carry
    lax.fori_loop(0, n_pairs // 2, add_two_pairs, 0)
    for c in range(ch):
        y_ref[c] = yacc[:, tc * c:tc * (c + 1)].astype(y_ref.dtype)


def _s5_core(lay, u, mats):
    kk, m_in, m_out, arec = mats
    g, ch, tc, ns = SSM_GROUPS, SSM_CH_PER_GROUP, S5_CHUNK, SSM_STATE
    b = lay.batch
    assert lay.seq % tc == 0 and lay.ctx_len % tc == 0
    nkl, nkc = lay.seq // tc, lay.ctx_len // tc
    nk = nkl + nkc
    rows = b * nk
    x = u.reshape(rows, tc, D_MODEL).transpose(2, 0, 1)
    blk = lambda *s: pl.BlockSpec((1,) + s, lambda i: (i,) + (0,) * len(s))
    planes = pl.BlockSpec((ch, rows, tc), lambda i: (i, 0, 0))
    y = pl.pallas_call(
        functools.partial(_s5_core_body, b, nk, nkc),
        grid=(g,),
        in_specs=[planes, blk(ch, 2 * tc * ch), blk(tc * ch, 4 * ns), blk(8 * ns, tc * ch), blk(2, 2 * ns)],
        out_specs=planes,
        out_shape=jax.ShapeDtypeStruct((D_MODEL, rows, tc), BF16),
        scratch_shapes=[pltpu.VMEM((2 * tc, ch * tc), BF16), pltpu.VMEM((2 * tc, ch * tc), BF16),
                        pltpu.VMEM((rows, ch * tc), F32),
                        pltpu.VMEM((2, rows, 2 * ns), F32), pltpu.VMEM((2, 2, rows, 2 * ns), F32)],
        compiler_params=_cparams("arbitrary"),
        name="s5_scan",
    )(x, kk, m_in, m_out, arec)
    return y.transpose(1, 2, 0).reshape(rows * tc, D_MODEL)


def _s5_out_body(y_ref, u_ref, d_ref, wv_ref, wg_ref, *rest):
    u = u_ref[...].astype(F32)
    act = _gelu_tanh(y_ref[...].astype(F32) + d_ref[...] * u).astype(BF16)
    val = jnp.dot(act, wv_ref[...], preferred_element_type=F32)
    gate = jnp.dot(act, wg_ref[...], preferred_element_type=F32)
    _finish_route(val * jax.nn.sigmoid(gate), *rest)


def _s5_out(lay, n_tiles, layer, y, u, d, wv, wg, h, mods, lng, lnb, wr, br):
    shapes, specs = _route_outputs(n_tiles * TOK_TILE)
    return pl.pallas_call(
        _s5_out_body,
        grid=(n_tiles,),
        in_specs=[lay.s5_row_spec(), lay.s5_row_spec(), _const_spec((1, D_MODEL)),
                  _const_spec((D_MODEL, D_MODEL)), _const_spec((D_MODEL, D_MODEL))] + _route_inputs(lay, layer),
        out_specs=specs,
        out_shape=shapes,
        compiler_params=_cparams("arbitrary"),
        name="s5_out_route",
    )(y, u, d, wv, wg, h, mods, lng, lnb, wr, br)


def kernel(x, c, ctx, c_ctx, ada_w, ada_b, ln_g, ln_b, s5_lam_re, s5_lam_im, s5_log_dt, s5_b_re, s5_b_im, s5_c_re, s5_c_im, s5_d, s5_w_glu, cv_w_pw1, cv_b_pw1, cv_w_dw, cv_b_dw, cv_ln_g, cv_ln_b, cv_w_pw2, cv_b_pw2, at_w_qkv, at_w_o, at_sink, moe_wg, moe_bg, moe_we, moe_be, moe_w1, moe_w2):
    batch, seq, d = x.shape
    lay = _Layout(batch, seq, ctx.shape[1])
    depth = ada_w.shape[0]
    row = lambda a: a.reshape(1, -1)

    c_all =jnp.concatenate([c, c_ctx[None], jnp.zeros((lay.mod_rows - batch - 1, d), F32)], axis=0)
    mods = _modulation(c_all, ada_w, ada_b)
    cos_t, sin_t = _rope_tables(seq)
    ng, ne = N_EXPERT_GROUPS, N_EXPERTS
    pad = jnp.zeros((d, ROUTE_LANES - ng - ne), F32)

    u_spec = lambda layer: lay.s5_row_spec() if layer % 3 == 0 else _row_spec()
    h, u = _modulate(lay, x.reshape(lay.n_lat, d), ctx.reshape(lay.n_ctx, d), mods, 0, u_spec(0))
    for i in range(depth):
        last = i == depth - 1
        kind, j = i % 3, i // 3
        n_tiles = lay.lat_tiles if last else lay.tiles
        wr = _split_bf16(jnp.concatenate([moe_wg[i], moe_we[i], pad], axis=1))
        br = jnp.concatenate([moe_bg[i], moe_be[i], pad[0]], axis=0)[None]
        route_args = (h, mods, row(ln_g[i, 0]), row(ln_b[i, 0]), wr, br)
        if kind == 0:
            mats = _s5_prepare(s5_lam_re[j], s5_lam_im[j], s5_log_dt[j], s5_b_re[j], s5_b_im[j],
                               s5_c_re[j], s5_c_im[j])
            y = _s5_core(lay, u, mats)
            wglu = s5_w_glu[j].astype(BF16)
            h1, v, meta, cnt = _s5_out(lay, n_tiles, i, y, u, row(s5_d[j]), wglu[:, :d], wglu[:, d:], *route_args)
        elif kind == 1:
            w1 = cv_w_pw1[j].astype(BF16)
            z = _conv_pw1(lay, u, w1[:, :d], w1[:, d:], row(cv_b_pw1[j, :d]), row(cv_b_pw1[j, d:]))
            h1, v, meta, cnt = _conv_out(lay, n_tiles, i, z, cv_w_dw[j], row(cv_b_dw[j]), row(cv_ln_g[j]),
                                         row(cv_ln_b[j]), cv_w_pw2[j].astype(BF16), row(cv_b_pw2[j]),
                                         *route_args)
        else:
            q, k, vv = _qkv_rope(lay, u, at_w_qkv[j].astype(BF16), cos_t, sin_t)
            o = _attention(lay, q, k, vv, row(at_sink[j]), not last)
            h1, v, meta, cnt = _attn_out(lay, n_tiles, i, o, at_w_o[j].astype(BF16), *route_args)
        plan = _moe_plan(meta, cnt, n_tiles)
        xs = _moe_dispatch(v, plan)
        ys = _moe_experts(xs, moe_w1, moe_w2, i, plan.items)
        h, u = _moe_combine(lay, n_tiles, ys, plan, h1, mods, i, row(ln_g[i, 1]), row(ln_b[i, 1]),
                            None if last else u_spec(i + 1))
    return h.reshape(batch, seq, d)
```

```python
import functools
import math

import jax
import jax.numpy as jnp
from jax import lax
from jax.experimental import pallas as pl
from jax.experimental.pallas import tpu as pltpu

F32 = jnp.float32
BF16 = jnp.bfloat16
HIGHEST = lax.Precision.HIGHEST

D_MODEL = 1024
DEPTH = 4
GRID_W = 64
SSM_CH_PER_GROUP = 16
SSM_GROUPS = D_MODEL // SSM_CH_PER_GROUP
SSM_STATE = 64
CONV_WIDTH = 31
HEAD_DIM = 64
N_HEADS = D_MODEL // HEAD_DIM
N_KV_HEADS = N_HEADS // 4
KV_REP = N_HEADS // N_KV_HEADS
WINDOW = 128
ATT_BLOCK = 128
ROPE_BASE = 10000.0
N_EXPERT_GROUPS = 4
EXPERTS_PER_GROUP = 8
N_EXPERTS = N_EXPERT_GROUPS * EXPERTS_PER_GROUP
D_EXPERT = D_MODEL // 2
ALPHA = (2 * DEPTH) ** 0.25
LN_EPS = 1e-5

SUBLANES = 8
LANES = 128

TOK_TILE = 256
EXPERT_TILE = 512
S5_CHUNK = LANES
CONV_HALO = 16
ROUTE_LANES = LANES
META_COLS = 8
RUN_PAD = SUBLANES
LOCAL_ROWS = -(-(2 * TOK_TILE + N_EXPERTS * (RUN_PAD - 1)) // LANES) * LANES
MAX_PIECES = LOCAL_ROWS // RUN_PAD
VMEM_LIMIT = 56 * 1024 * 1024


def _cparams(*sem):
    return pltpu.CompilerParams(dimension_semantics=sem, vmem_limit_bytes=VMEM_LIMIT)


def _round_up(n, m):
    return (n + m - 1) // m * m


def _split_bf16(w):
    hi = w.astype(BF16)
    return jnp.stack([hi, (w - hi.astype(F32)).astype(BF16)])


def _mod_body(c_ref, w_ref, b_ref, o_ref):
    c = c_ref[...]
    s = c * jax.nn.sigmoid(c)
    o_ref[0] = jnp.dot(s, w_ref[0], preferred_element_type=F32, precision=HIGHEST) + b_ref[0]


def _modulation(c_all, ada_w, ada_b):
    depth, d, n = ada_w.shape
    r = c_all.shape[0]
    tn = 1024
    out = pl.pallas_call(
        _mod_body,
        grid=(depth, n // tn),
        in_specs=[
            pl.BlockSpec((r, d), lambda i, j: (0, 0)),
            pl.BlockSpec((1, d, tn), lambda i, j: (i, 0, j)),
            pl.BlockSpec((1, 1, tn), lambda i, j: (i, 0, j)),
        ],
        out_specs=pl.BlockSpec((1, r, tn), lambda i, j: (i, 0, j)),
        out_shape=jax.ShapeDtypeStruct((depth, r, n), F32),
        compiler_params=_cparams("arbitrary", "arbitrary"),
        name="adaln_modulation",
    )(c_all, ada_w, ada_b.reshape(depth, 1, n))
    return out.reshape(depth, r, 6, d)


def _layer_norm_rows(t, g, b):
    mu = jnp.mean(t, axis=-1, keepdims=True)
    dev = t - mu
    var = jnp.mean(dev * dev, axis=-1, keepdims=True)
    return dev * lax.rsqrt(var + LN_EPS) * g + b


def _post_norm_and_route(h, y, mod, lng, lnb, wr, br):
    tm = h.shape[0]
    ng, ne = N_EXPERT_GROUPS, N_EXPERTS
    h1 = _layer_norm_rows(ALPHA * h + mod[2:3] * y, lng, lnb)
    v = h1 * (1.0 + mod[4:5]) + mod[3:4]
    v_hi = v.astype(BF16)
    v_lo = (v - v_hi.astype(F32)).astype(BF16)
    logits = (jnp.dot(v_hi, wr[0], preferred_element_type=F32) + jnp.dot(v_lo, wr[0], preferred_element_type=F32)
              + jnp.dot(v_hi, wr[1], preferred_element_type=F32) + br)

    lane = lax.broadcasted_iota(jnp.int32, (tm, ROUTE_LANES), 1)
    lane_f = lane.astype(F32)
    neg = -jnp.inf
    no_lane = float(ROUTE_LANES)
    is_group = lane < ng
    gl = jnp.where(is_group, logits, neg)
    gmax = jnp.max(gl, axis=-1, keepdims=True)
    gsum = jnp.sum(jnp.where(is_group, jnp.exp(logits - gmax), 0.0), axis=-1, keepdims=True)
    p_group = 1.0 / gsum
    g_idx = jnp.min(jnp.where(gl == gmax, lane_f, no_lane), axis=-1, keepdims=True)
    expert_group = ((lane - ng) // EXPERTS_PER_GROUP).astype(F32)
    in_group = (lane >= ng) & (lane < ng + ne) & (expert_group == g_idx)
    el = jnp.where(in_group, logits, neg)
    v1 = jnp.max(el, axis=-1, keepdims=True)
    i1 = jnp.min(jnp.where(el == v1, lane_f, no_lane), axis=-1, keepdims=True)
    el2 = jnp.where(lane_f == i1, neg, el)
    v2 = jnp.max(el2, axis=-1, keepdims=True)
    i2 = jnp.min(jnp.where(el2 == v2, lane_f, no_lane), axis=-1, keepdims=True)
    e21 = jnp.exp(v2 - v1)
    w1 = p_group / (1.0 + e21)
    w2 = p_group * e21 / (1.0 + e21)

    hit1 = lane_f == i1
    hit2 = lane_f == i2
    one1 = hit1.astype(BF16)
    one2 = hit2.astype(BF16)
    rr = lax.broadcasted_iota(jnp.int32, (tm, tm), 0)
    cc = lax.broadcasted_iota(jnp.int32, (tm, tm), 1)
    before = (cc < rr).astype(BF16)
    cum1 = jnp.dot(before, one1, preferred_element_type=F32)
    cum2 = jnp.dot(before, one2, preferred_element_type=F32)
    tot1 = jnp.sum(one1.astype(F32), axis=0, keepdims=True)
    tot2 = jnp.sum(one2.astype(F32), axis=0, keepdims=True)
    counts = tot1 + tot2
    pieces = jnp.floor((counts + (RUN_PAD - 1.0)) * (1.0 / RUN_PAD))
    earlier = (lax.broadcasted_iota(jnp.int32, (ROUTE_LANES, ROUTE_LANES), 0)
               < lax.broadcasted_iota(jnp.int32, (ROUTE_LANES, ROUTE_LANES), 1)).astype(BF16)
    run_start = RUN_PAD * jnp.dot(jnp.broadcast_to(pieces, (SUBLANES, ROUTE_LANES)).astype(BF16), earlier,
                                  preferred_element_type=F32)[0:1]
    rank1 = jnp.sum(jnp.where(hit1, run_start + cum1, 0.0), axis=-1, keepdims=True)
    rank2 = jnp.sum(jnp.where(hit2, run_start + tot1 + cum2, 0.0), axis=-1, keepdims=True)

    col = lax.broadcasted_iota(jnp.int32, (tm, META_COLS), 1)
    meta = jnp.where(col == 0, i1 - ng,
           jnp.where(col == 1, i2 - ng,
           jnp.where(col == 2, rank1,
           jnp.where(col == 3, rank2,
           jnp.where(col == 4, w1,
           jnp.where(col == 5, w2, 0.0))))))
    return h1, v, meta, tot1 + tot2


def _gelu_tanh(x):
    return 0.5 * x * (1.0 + jnp.tanh(math.sqrt(2.0 / math.pi) * (x + 0.044715 * (x * x * x))))


class _Layout:
    def __init__(self, batch, seq, ctx_len):
        self.batch, self.seq, self.ctx_len = batch, seq, ctx_len
        self.n_lat = batch * seq
        self.n_ctx = batch * ctx_len
        self.n_tok = self.n_lat + self.n_ctx
        assert seq % TOK_TILE == 0 and ctx_len % TOK_TILE == 0
        self.lat_tiles = self.n_lat // TOK_TILE
        self.tiles = self.n_tok // TOK_TILE
        self.tiles_per_seq = seq // TOK_TILE
        self.tiles_per_ctx = ctx_len // TOK_TILE
        self.mod_rows = _round_up(batch + 1, SUBLANES)

    def mod_row(self, i):
        return jnp.minimum(i // self.tiles_per_seq, self.batch)

    def s5_tile(self, i):
        per_batch = self.tiles_per_seq + self.tiles_per_ctx
        lat = (i // self.tiles_per_seq) * per_batch + self.tiles_per_ctx + i % self.tiles_per_seq
        c = i - self.lat_tiles
        ctx = (c // self.tiles_per_ctx) * per_batch + c % self.tiles_per_ctx
        return jnp.where(i < self.lat_tiles, lat, ctx)

    def s5_row_spec(self):
        return pl.BlockSpec((TOK_TILE, D_MODEL), lambda i, *_: (self.s5_tile(i), 0))


def _mod_spec(lay, layer):
    return pl.BlockSpec((1, 1, 6, D_MODEL), lambda i, *_: (layer, lay.mod_row(i), 0, 0))


def _row_spec(tm=TOK_TILE, d=D_MODEL):
    return pl.BlockSpec((tm, d), lambda i, *_: (i, 0))


def _const_spec(shape):
    nd = len(shape)
    return pl.BlockSpec(shape, lambda i, *_: (0,) * nd)


def _modulate_body(lat_tiles, x_ref, ctx_ref, mod_ref, h_ref, u_ref):
    mod = mod_ref[0, 0]
    h = jnp.where(pl.program_id(0) < lat_tiles, x_ref[...], ctx_ref[...])
    h_ref[...] = h
    u_ref[...] = (h * (1.0 + mod[1:2]) + mod[0:1]).astype(u_ref.dtype)


def _modulate(lay, x, ctx, mods, layer, u_spec):
    ctx_tiles = lay.tiles - lay.lat_tiles
    return pl.pallas_call(
        functools.partial(_modulate_body, lay.lat_tiles),
        grid=(lay.tiles,),
        in_specs=[pl.BlockSpec((TOK_TILE, D_MODEL), lambda i: (jnp.minimum(i, lay.lat_tiles - 1), 0)),
                  pl.BlockSpec((TOK_TILE, D_MODEL), lambda i: (jnp.clip(i - lay.lat_tiles, 0, ctx_tiles - 1), 0)),
                  _mod_spec(lay, layer)],
        out_specs=[_row_spec(), u_spec],
        out_shape=[jax.ShapeDtypeStruct((lay.n_tok, D_MODEL), F32),
                   jax.ShapeDtypeStruct((lay.n_tok, D_MODEL), BF16)],
        compiler_params=_cparams("arbitrary"),
        name="input_modulate",
    )(x, ctx, mods)


PACKED = D_MODEL // 2


def _pack_rows(x):
    half = x.shape[1] // 2
    hi = lax.bitcast_convert_type(x[:, :half], jnp.uint32)
    lo = lax.bitcast_convert_type(x[:, half:], jnp.uint32)
    return hi | (lo >> 16)


def _unpack_rows(w):
    hi = lax.bitcast_convert_type(w & jnp.uint32(0xFFFF0000), F32)
    lo = lax.bitcast_convert_type(w << 16, F32)
    return hi.astype(BF16), lo.astype(BF16)


def _piece_spec(n_tiles, shift=0):
    return pl.BlockSpec((1, 1, MAX_PIECES), lambda i, *_: (jnp.minimum(i + shift, n_tiles - 1), 0, 0),
                        memory_space=pltpu.SMEM)


def _for_each_piece(rows_ref, count, fn):
    def per_piece(q, carry):
        fn(q, rows_ref[0, 0, q])
        return carry
    lax.fori_loop(0, count, per_piece, 0)


def _dispatch_body(np_ref, tail_ref, runs_ref, lp_ref, v_ref, xs_ref, loc, zeros, sem):
    i = pl.program_id(0)
    lp = lp_ref[0]
    slot_row = lax.broadcasted_iota(jnp.int32, (LOCAL_ROWS, v_ref.shape[0]), 0).astype(F32)
    pick = ((slot_row == lp[0:1]) | (slot_row == lp[1:2])).astype(BF16)
    loc[...] = _pack_rows(jnp.dot(pick, v_ref[...], preferred_element_type=F32))

    def piece(src, dst):
        return pltpu.make_async_copy(src, xs_ref.at[pl.ds(pl.multiple_of(dst, RUN_PAD), RUN_PAD)], sem)

    def issue(q, row):
        piece(loc.at[pl.ds(pl.multiple_of(q * RUN_PAD, RUN_PAD), RUN_PAD)], row).start()

    def drain(q, carry):
        piece(loc.at[pl.ds(0, RUN_PAD)], 0).wait()
        return carry

    _for_each_piece(runs_ref, np_ref[i], issue)
    lax.fori_loop(0, np_ref[i], drain, 0)

    @pl.when(i == pl.num_programs(0) - 1)
    def _():
        zeros[...] = jnp.zeros_like(zeros)

        def issue_zero(q, carry):
            piece(zeros, tail_ref[0] + q * RUN_PAD).start()
            return carry

        def drain_zero(q, carry):
            piece(zeros, 0).wait()
            return carry

        lax.fori_loop(0, tail_ref[1], issue_zero, 0)
        lax.fori_loop(0, tail_ref[1], drain_zero, 0)


def _moe_dispatch(v, plan):
    n_tiles = plan.n_pieces.shape[0]
    grid_spec = pltpu.PrefetchScalarGridSpec(
        num_scalar_prefetch=2,
        grid=(n_tiles,),
        in_specs=[_piece_spec(n_tiles),
                  pl.BlockSpec((1, 2, TOK_TILE), lambda i, *_: (i, 0, 0)),
                  _row_spec()],
        out_specs=pl.BlockSpec(memory_space=pl.ANY),
        scratch_shapes=[pltpu.VMEM((LOCAL_ROWS, PACKED), jnp.uint32), pltpu.VMEM((RUN_PAD, PACKED), jnp.uint32),
                        pltpu.SemaphoreType.DMA(())],
    )
    return pl.pallas_call(
        _dispatch_body,
        grid_spec=grid_spec,
        out_shape=jax.ShapeDtypeStruct((plan.max_rows, PACKED), jnp.uint32),
        compiler_params=_cparams("arbitrary"),
        name="moe_dispatch",
    )(plan.n_pieces, plan.tail, plan.piece_rows, plan.lp_rows, v)


def _expert_body(tile_ref, exp_ref, lo_ref, hi_ref, xs_ref, w1_ref, w2_ref, ys_ref, w1b, w2b):
    j = pl.program_id(0)
    jp = jnp.maximum(j - 1, 0)
    new_expert = (j == 0) | (exp_ref[j] != exp_ref[jp])
    first_visit = (j == 0) | (tile_ref[j] != tile_ref[jp])
    lo, hi = lo_ref[j], hi_ref[j]

    @pl.when(new_expert)
    def _():
        w1b[...] = w1_ref[0, 0].astype(BF16)
        w2b[...] = w2_ref[0, 0].astype(BF16)

    def expert_rows():
        x_a, x_b = _unpack_rows(xs_ref[...])
        gu = (jnp.dot(x_a, w1b[:PACKED], preferred_element_type=F32)
              + jnp.dot(x_b, w1b[PACKED:], preferred_element_type=F32))
        gate, up = gu[:, :D_EXPERT], gu[:, D_EXPERT:]
        a = (gate * jax.nn.sigmoid(gate) * up).astype(BF16)
        y = jnp.dot(a, w2b[...], preferred_element_type=F32)
        rows = lax.broadcasted_iota(jnp.int32, (xs_ref.shape[0], 1), 0)
        return _pack_rows(y.astype(BF16).astype(F32)), (rows >= lo) & (rows < hi)

    @pl.when(first_visit)
    def _():
        y, mine = expert_rows()
        ys_ref[...] = jnp.where(mine, y, jnp.uint32(0))

    @pl.when(jnp.logical_not(first_visit) & (hi > lo))
    def _():
        y, mine = expert_rows()
        ys_ref[...] = jnp.where(mine, y, ys_ref[...])


def _moe_experts(xs, w1, w2, layer, items):
    tile_j, exp_j, lo_j, hi_j = items
    n_items = tile_j.shape[0]
    grid_spec = pltpu.PrefetchScalarGridSpec(
        num_scalar_prefetch=4,
        grid=(n_items,),
        in_specs=[
            pl.BlockSpec((EXPERT_TILE, PACKED), lambda j, t, e, lo, hi: (t[j], 0)),
            pl.BlockSpec((1, 1, D_MODEL, 2 * D_EXPERT), lambda j, t, e, lo, hi: (layer, e[j], 0, 0)),
            pl.BlockSpec((1, 1, D_EXPERT, D_MODEL), lambda j, t, e, lo, hi: (layer, e[j], 0, 0)),
        ],
        out_specs=pl.BlockSpec((EXPERT_TILE, PACKED), lambda j, t, e, lo, hi: (t[j], 0)),
        scratch_shapes=[pltpu.VMEM((D_MODEL, 2 * D_EXPERT), BF16), pltpu.VMEM((D_EXPERT, D_MODEL), BF16)],
    )
    return pl.pallas_call(
        _expert_body,
        grid_spec=grid_spec,
        out_shape=jax.ShapeDtypeStruct(xs.shape, xs.dtype),
        compiler_params=_cparams("arbitrary"),
        name="moe_experts",
    )(tile_j, exp_j, lo_j, hi_j, xs, w1, w2)


def _combine_body(has_next, np_ref, src_ref, srcn_ref, lpw_ref, h1_ref, mod_ref, modn_ref, lng_ref, lnb_ref,
                  ys_ref, *rest):
    if has_next:
        h2_ref, u_ref, buf, sem = rest
    else:
        h2_ref, buf, sem = rest
    tm = h1_ref.shape[0]
    i = pl.program_id(0)
    n = pl.num_programs(0)
    slot = i % 2

    def piece(src, s, q):
        return pltpu.make_async_copy(ys_ref.at[pl.ds(pl.multiple_of(src, RUN_PAD), RUN_PAD)],
                                     buf.at[s, pl.ds(pl.multiple_of(q * RUN_PAD, RUN_PAD), RUN_PAD)], sem.at[s])

    def gather(rows_ref, s, count):
        _for_each_piece(rows_ref, count, lambda q, row: piece(row, s, q).start())

    @pl.when(i == 0)
    def _():
        buf[...] = jnp.zeros_like(buf)
        gather(src_ref, 0, np_ref[0])

    @pl.when(i + 1 < n)
    def _():
        gather(srcn_ref, 1 - slot, np_ref[jnp.minimum(i + 1, n - 1)])

    def drain(q, carry):
        piece(0, slot, 0).wait()
        return carry
    lax.fori_loop(0, np_ref[i], drain, 0)

    lpw = lpw_ref[...]
    lane = lax.broadcasted_iota(jnp.int32, (tm, LOCAL_ROWS), 1).astype(F32)
    wmat = jnp.where(lane == lpw[:, 0:1], lpw[:, 2:3], 0.0) + jnp.where(lane == lpw[:, 1:2], lpw[:, 3:4], 0.0)
    wmat = wmat.astype(BF16)
    f = jnp.concatenate([jnp.dot(wmat, yb, preferred_element_type=F32) for yb in _unpack_rows(buf[slot])],
                        axis=1)
    mod = mod_ref[0, 0]
    h2 = _layer_norm_rows(ALPHA * h1_ref[...] + mod[5:6] * f, lng_ref[...], lnb_ref[...])
    h2_ref[...] = h2
    if has_next:
        modn = modn_ref[0, 0]
        u_ref[...] = (h2 * (1.0 + modn[1:2]) + modn[0:1]).astype(u_ref.dtype)


def _moe_combine(lay, n_tiles, ys, plan, h1, mods, layer, lng, lnb, u_spec):
    has_next = u_spec is not None
    n_rows = n_tiles * TOK_TILE
    nxt = min(layer + 1, DEPTH - 1)
    out_shape = [jax.ShapeDtypeStruct((n_rows, D_MODEL), F32)]
    out_specs = [_row_spec()]
    if has_next:
        out_shape.append(jax.ShapeDtypeStruct((n_rows, D_MODEL), BF16))
        out_specs.append(u_spec)
    grid_spec = pltpu.PrefetchScalarGridSpec(
        num_scalar_prefetch=1,
        grid=(n_tiles,),
        in_specs=[
            _piece_spec(n_tiles), _piece_spec(n_tiles, 1),
            _row_spec(d=4), _row_spec(), _mod_spec(lay, layer), _mod_spec(lay, nxt),
            _const_spec((1, D_MODEL)), _const_spec((1, D_MODEL)),
            pl.BlockSpec(memory_space=pl.ANY),
        ],
        out_specs=out_specs,
        scratch_shapes=[pltpu.VMEM((2, LOCAL_ROWS, PACKED), jnp.uint32), pltpu.SemaphoreType.DMA((2,))],
    )
    outs = pl.pallas_call(
        functools.partial(_combine_body, has_next),
        grid_spec=grid_spec,
        out_shape=out_shape,
        compiler_params=_cparams("arbitrary"),
        name="moe_combine",
    )(plan.n_pieces, plan.piece_rows, plan.piece_rows, plan.lp_w, h1, mods, mods, lng, lnb, ys)
    return outs if has_next else (outs[0], None)


class _MoePlan:
    pass


def _exclusive_cumsum(a, axis):
    return jnp.cumsum(a, axis=axis) - a


def _moe_plan(meta, tile_counts, n_tiles):
    plan = _MoePlan()
    plan.max_rows = _round_up(n_tiles * (2 * TOK_TILE + N_EXPERTS * (RUN_PAD - 1)), EXPERT_TILE)
    cnt = tile_counts[:, 0, N_EXPERT_GROUPS:N_EXPERT_GROUPS + N_EXPERTS].astype(jnp.int32)
    run = (cnt + RUN_PAD - 1) // RUN_PAD * RUN_PAD
    counts = jnp.sum(run, axis=0)
    total = jnp.sum(counts)
    tail = (-total) % EXPERT_TILE
    counts = counts.at[N_EXPERTS - 1].add(tail)
    ends = jnp.cumsum(counts)
    starts = ends - counts
    run_start = starts[None, :] + _exclusive_cumsum(run, 0)
    plan.tail = jnp.stack([total, tail // RUN_PAD]).astype(jnp.int32)

    plan.lp_rows = meta[:, 2:4].reshape(n_tiles, TOK_TILE, 2).transpose(0, 2, 1)
    plan.lp_w = meta[:, 2:6]

    experts = jnp.arange(N_EXPERTS, dtype=jnp.int32)
    pieces = run // RUN_PAD
    piece_end = jnp.cumsum(pieces, axis=1)
    plan.n_pieces = piece_end[:, -1].astype(jnp.int32)
    q = jnp.arange(MAX_PIECES, dtype=jnp.int32)
    owner = jnp.minimum(jnp.sum(q[None, :, None] >= piece_end[:, None, :], axis=-1), N_EXPERTS - 1)
    pick = owner[..., None] == experts
    first = jnp.sum(jnp.where(pick, (piece_end - pieces)[:, None, :], 0), axis=-1)
    base = jnp.sum(jnp.where(pick, run_start[:, None, :], 0), axis=-1)
    rows = jnp.where(q[None, :] < plan.n_pieces[:, None], base + (q[None, :] - first) * RUN_PAD, 0)
    plan.piece_rows = rows.astype(jnp.int32).reshape(n_tiles, 1, MAX_PIECES)

    n_etiles = plan.max_rows // EXPERT_TILE
    first_tile = starts // EXPERT_TILE
    last_tile = jnp.maximum(ends - 1, 0) // EXPERT_TILE
    n_items_e = jnp.where(counts > 0, last_tile - first_tile + 1, 0)
    item_end = jnp.cumsum(n_items_e)
    item_start = item_end - n_items_e
    n_items = n_etiles + N_EXPERTS
    j = jnp.arange(n_items, dtype=jnp.int32)
    e_j = jnp.minimum(jnp.sum(j[:, None] >= item_end[None, :], axis=1), N_EXPERTS - 1).astype(jnp.int32)
    active = j < item_end[-1]
    tile_j = jnp.take(first_tile, e_j) + (j - jnp.take(item_start, e_j))
    lo = jnp.maximum(jnp.take(starts, e_j), tile_j * EXPERT_TILE) - tile_j * EXPERT_TILE
    hi = jnp.minimum(jnp.take(ends, e_j), (tile_j + 1) * EXPERT_TILE) - tile_j * EXPERT_TILE
    last = jnp.maximum(item_end[-1] - 1, 0)
    tile_j = jnp.where(active, tile_j, tile_j[last]).astype(jnp.int32)
    e_j = jnp.where(active, e_j, e_j[last]).astype(jnp.int32)
    lo = jnp.where(active, lo, 0).astype(jnp.int32)
    hi = jnp.where(active, hi, 0).astype(jnp.int32)
    plan.items = (tile_j, e_j, lo, hi)
    return plan


def _route_outputs(n_rows):
    n_tiles = n_rows // TOK_TILE
    shapes = [jax.ShapeDtypeStruct((n_rows, D_MODEL), F32),
              jax.ShapeDtypeStruct((n_rows, D_MODEL), BF16),
              jax.ShapeDtypeStruct((n_rows, META_COLS), F32),
              jax.ShapeDtypeStruct((n_tiles, 1, ROUTE_LANES), F32)]
    specs = [_row_spec(), _row_spec(), _row_spec(d=META_COLS),
             pl.BlockSpec((1, 1, ROUTE_LANES), lambda i: (i, 0, 0))]
    return shapes, specs


def _route_inputs(lay, layer):
    return [_row_spec(), _mod_spec(lay, layer), _const_spec((1, D_MODEL)), _const_spec((1, D_MODEL)),
            _const_spec((2, D_MODEL, ROUTE_LANES)), _const_spec((1, ROUTE_LANES))]


def _finish_route(y, h_ref, mod_ref, lng_ref, lnb_ref, wr_ref, br_ref, h1_ref, v_ref, meta_ref, cnt_ref):
    h1, v, meta, counts = _post_norm_and_route(h_ref[...], y, mod_ref[0, 0], lng_ref[...], lnb_ref[...],
                                               wr_ref[...], br_ref[...])
    h1_ref[...] = h1
    v_ref[...] = v.astype(v_ref.dtype)
    meta_ref[...] = meta
    cnt_ref[0] = counts


def _attn_out_body(o_ref, wo_ref, *rest):
    y = jnp.dot(o_ref[...], wo_ref[...], preferred_element_type=F32)
    _finish_route(y, *rest)


def _attn_out(lay, n_tiles, layer, o, w_o, h, mods, lng, lnb, wr, br):
    shapes, specs = _route_outputs(n_tiles * TOK_TILE)
    return pl.pallas_call(
        _attn_out_body,
        grid=(n_tiles,),
        in_specs=[_row_spec(), _const_spec((D_MODEL, D_MODEL))] + _route_inputs(lay, layer),
        out_specs=specs,
        out_shape=shapes,
        compiler_params=_cparams("arbitrary"),
        name="attn_out_route",
    )(o, w_o, h, mods, lng, lnb, wr, br)


def _rope_tables(seq):
    quarter = HEAD_DIM // 4
    inv_freq = ROPE_BASE ** (-jnp.arange(quarter, dtype=F32) / quarter)
    t = jnp.arange(seq)
    rows = (t // GRID_W).astype(F32)
    cols = (t % GRID_W).astype(F32)
    ang_r = rows[:, None] * inv_freq
    ang_c = cols[:, None] * inv_freq
    cos_h = jnp.concatenate([jnp.cos(ang_r), jnp.cos(ang_r), jnp.cos(ang_c), jnp.cos(ang_c)], axis=-1)
    sin_h = jnp.concatenate([-jnp.sin(ang_r), jnp.sin(ang_r), -jnp.sin(ang_c), jnp.sin(ang_c)], axis=-1)
    rep = LANES // HEAD_DIM
    return jnp.tile(cos_h, (1, rep)), jnp.tile(sin_h, (1, rep))


def _rope(x, cos, sin):
    quarter = HEAD_DIM // 4
    width = x.shape[1]
    n = width // LANES
    c = jnp.tile(cos, (1, n))
    s = jnp.tile(sin, (1, n))
    lane = lax.broadcasted_iota(jnp.int32, x.shape, 1)
    is_lo = (lane % (2 * quarter)) < quarter
    partner = jnp.where(is_lo, pltpu.roll(x, width - quarter, 1), pltpu.roll(x, quarter, 1))
    return x * c + partner * s


def _qkv_body(lat_tiles, u_ref, w_ref, cos_ref, sin_ref, q_ref, k_ref, v_ref):
    is_ctx = pl.program_id(0) >= lat_tiles
    qkv = jnp.dot(u_ref[...], w_ref[...], preferred_element_type=F32)
    nq, nkv = N_HEADS * HEAD_DIM, N_KV_HEADS * HEAD_DIM
    q, k, v = qkv[:, :nq], qkv[:, nq:nq + nkv], qkv[:, nq + nkv:]
    cos, sin = cos_ref[...], sin_ref[...]
    q = jnp.where(is_ctx, q, _rope(q, cos, sin))
    k = jnp.where(is_ctx, k, _rope(k, cos, sin))
    q_ref[...] = (q * HEAD_DIM ** -0.5).astype(q_ref.dtype)
    k_ref[...] = k.astype(k_ref.dtype)
    v_ref[...] = v.astype(v_ref.dtype)


def _qkv_rope(lay, u, w_qkv, cos_t, sin_t):
    nq, nkv = N_HEADS * HEAD_DIM, N_KV_HEADS * HEAD_DIM
    tab = pl.BlockSpec((TOK_TILE, LANES),
                       lambda i: (jnp.where(i < lay.lat_tiles, i % lay.tiles_per_seq, 0), 0))
    return pl.pallas_call(
        functools.partial(_qkv_body, lay.lat_tiles),
        grid=(lay.tiles,),
        in_specs=[_row_spec(), _const_spec((D_MODEL, nq + 2 * nkv)), tab, tab],
        out_specs=[_row_spec(), _row_spec(d=nkv), _row_spec(d=nkv)],
        out_shape=[jax.ShapeDtypeStruct((lay.n_tok, nq), BF16),
                   jax.ShapeDtypeStruct((lay.n_tok, nkv), BF16),
                   jax.ShapeDtypeStruct((lay.n_tok, nkv), BF16)],
        compiler_params=_cparams("arbitrary"),
        name="attn_qkv_rope",
    )(u, w_qkv, cos_t, sin_t)


def _attn_body(nqb, sink_ref, q_ref, kp_ref, kc_ref, kn_ref, kx_ref, vp_ref, vc_ref, vn_ref, vx_ref,
               o_ref, kcat, vcat):
    j = pl.program_id(1)
    blk = ATT_BLOCK
    n_win = 3 * blk
    for dst, parts in ((kcat, (kp_ref, kc_ref, kn_ref)), (vcat, (vp_ref, vc_ref, vn_ref))):
        for n, part in enumerate(parts):
            dst[n * blk:(n + 1) * blk] = part[...]
    kcat[n_win:] = kx_ref[...]
    vcat[n_win:] = vx_ref[...]
    rows = KV_REP * blk
    r = lax.broadcasted_iota(jnp.int32, (rows, n_win), 0) % blk
    c = lax.broadcasted_iota(jnp.int32, (rows, n_win), 1)
    valid = (jnp.abs(r + blk - c) <= WINDOW) & (j < nqb)
    valid = valid & ((c >= blk) | (j > 0)) & ((c < 2 * blk) | (j < nqb - 1))
    rep = lax.broadcasted_iota(jnp.int32, (rows, 1), 0) // blk
    for g in range(N_KV_HEADS):
        heads = [slice((g * KV_REP + n) * HEAD_DIM, (g * KV_REP + n + 1) * HEAD_DIM) for n in range(KV_REP)]
        kv_head = slice(g * HEAD_DIM, (g + 1) * HEAD_DIM)
        q4 = jnp.concatenate([q_ref[:, head] for head in heads], axis=0)
        s = lax.dot_general(q4, kcat[:, kv_head], (((1,), (1,)), ((), ())), preferred_element_type=F32)
        sink = jnp.zeros((rows, 1), F32)
        for n in range(KV_REP):
            sink = jnp.where(rep == n, sink_ref[0, g * KV_REP + n], sink)
        sw = jnp.where(valid, s[:, :n_win], -jnp.inf)
        sc = s[:, n_win:]
        m = jnp.maximum(jnp.max(sw, axis=-1, keepdims=True), jnp.max(sc, axis=-1, keepdims=True))
        m = jnp.maximum(m, sink)
        pw = jnp.exp(sw - m)
        pc = jnp.exp(sc - m)
        denom = (jnp.sum(pw, axis=-1, keepdims=True) + jnp.sum(pc, axis=-1, keepdims=True)
                 + jnp.exp(sink - m))
        acc = jnp.dot(pw.astype(BF16), vcat[:n_win, kv_head], preferred_element_type=F32)
        acc = acc + jnp.dot(pc.astype(BF16), vcat[n_win:, kv_head], preferred_element_type=F32)
        out = (acc / denom).astype(o_ref.dtype)
        for n, head in enumerate(heads):
            o_ref[:, head] = out[n * blk:(n + 1) * blk]


def _attention(lay, q, k, v, sink, with_ctx_out):
    nq, nkv = N_HEADS * HEAD_DIM, N_KV_HEADS * HEAD_DIM
    blk = ATT_BLOCK
    nqb = lay.seq // blk
    lc = lay.ctx_len
    assert lay.n_lat % lc == 0 and lay.seq % blk == 0 and lc % blk == 0
    ctx0 = lay.n_lat // lc
    ncb = lc // blk
    steps = nqb + (ncb if with_ctx_out else 0)
    q_blk = pl.BlockSpec(
        (blk, nq), lambda b, j: (jnp.where(j < nqb, b * nqb + j, (ctx0 + b) * ncb + j - nqb), 0))
    kv_blk = lambda shift: pl.BlockSpec(
        (blk, nkv), lambda b, j: (b * nqb + jnp.clip(j + shift, 0, nqb - 1), 0))
    kv_ctx = pl.BlockSpec((lc, nkv), lambda b, j: (ctx0 + b, 0))
    return pl.pallas_call(
        functools.partial(_attn_body, nqb),
        grid=(lay.batch, steps),
        in_specs=[pl.BlockSpec(memory_space=pltpu.SMEM), q_blk,
                  kv_blk(-1), kv_blk(0), kv_blk(1), kv_ctx,
                  kv_blk(-1), kv_blk(0), kv_blk(1), kv_ctx],
        out_specs=q_blk,
        out_shape=jax.ShapeDtypeStruct((lay.n_tok if with_ctx_out else lay.n_lat, nq), BF16),
        scratch_shapes=[pltpu.VMEM((3 * blk + lc, nkv), BF16), pltpu.VMEM((3 * blk + lc, nkv), BF16)],
        compiler_params=_cparams("arbitrary", "arbitrary"),
        name="attn_window",
    )(sink, q, k, k, k, k, v, v, v, v)


def _pw1_body(u_ref, wa_ref, wg_ref, ba_ref, bg_ref, z_ref):
    u = u_ref[...]
    a = jnp.dot(u, wa_ref[...], preferred_element_type=F32) + ba_ref[...]
    g = jnp.dot(u, wg_ref[...], preferred_element_type=F32) + bg_ref[...]
    z_ref[...] = (a * jax.nn.sigmoid(g)).astype(z_ref.dtype)


def _conv_pw1(lay, u, wa, wg, ba, bg):
    return pl.pallas_call(
        _pw1_body,
        grid=(lay.tiles,),
        in_specs=[_row_spec(), _const_spec((D_MODEL, D_MODEL)), _const_spec((D_MODEL, D_MODEL)),
                  _const_spec((1, D_MODEL)), _const_spec((1, D_MODEL))],
        out_specs=_row_spec(),
        out_shape=jax.ShapeDtypeStruct((lay.n_tok, D_MODEL), BF16),
        compiler_params=_cparams("arbitrary"),
        name="conv_pw1_glu",
    )(u, wa, wg, ba, bg)


def _conv_body(lat_tiles, tps, tpc, zp_ref, z_ref, zn_ref, wdw_ref, bdw_ref, cg_ref, cb_ref, w2_ref, b2_ref,
               h_ref, mod_ref, lng_ref, lnb_ref, wr_ref, br_ref, h1_ref, v_ref, meta_ref, cnt_ref, win, shifted):
    i = pl.program_id(0)
    tm = z_ref.shape[0]
    halo = CONV_HALO
    is_ctx = i >= lat_tiles
    pos = jnp.where(is_ctx, (i - lat_tiles) % tpc, i % tps)
    first = pos == 0
    last = pos == jnp.where(is_ctx, tpc, tps) - 1
    win[0:halo] = jnp.where(first, 0.0, zp_ref[...].astype(F32))
    win[halo:halo + tm] = z_ref[...].astype(F32)
    win[halo + tm:] = jnp.where(last, 0.0, zn_ref[...].astype(F32))
    off = halo - CONV_WIDTH // 2
    span = _round_up(off + CONV_WIDTH - 1, SUBLANES) - SUBLANES
    conv = jnp.zeros((tm, D_MODEL), F32)
    for res in range(SUBLANES):
        taps = [o - off for o in range(res, off + CONV_WIDTH, SUBLANES) if o >= off]
        if not taps:
            continue
        shifted[...] = win[res:res + tm + span, :]
        for tap in taps:
            base = off + tap - res
            conv = conv + wdw_ref[tap:tap + 1, :] * shifted[base:base + tm, :]
    conv = conv + bdw_ref[...]
    nrm = _layer_norm_rows(conv, cg_ref[...], cb_ref[...])
    act = (nrm * jax.nn.sigmoid(nrm)).astype(BF16)
    y = jnp.dot(act, w2_ref[...], preferred_element_type=F32) + b2_ref[...]
    _finish_route(y, h_ref, mod_ref, lng_ref, lnb_ref, wr_ref, br_ref, h1_ref, v_ref, meta_ref, cnt_ref)


def _conv_out(lay, n_tiles, layer, z, w_dw, b_dw, cg, cb, w2, b2, h, mods, lng, lnb, wr, br):
    per_tile = TOK_TILE // CONV_HALO
    n_halo = lay.n_tok // CONV_HALO
    shapes, specs = _route_outputs(n_tiles * TOK_TILE)
    halo_spec = lambda f: pl.BlockSpec((CONV_HALO, D_MODEL), lambda i: (jnp.clip(f(i), 0, n_halo - 1), 0))
    return pl.pallas_call(
        functools.partial(_conv_body, lay.lat_tiles, lay.tiles_per_seq, lay.tiles_per_ctx),
        grid=(n_tiles,),
        in_specs=[halo_spec(lambda i: i * per_tile - 1), _row_spec(), halo_spec(lambda i: (i + 1) * per_tile),
                  _const_spec((CONV_WIDTH, D_MODEL)), _const_spec((1, D_MODEL)), _const_spec((1, D_MODEL)),
                  _const_spec((1, D_MODEL)), _const_spec((D_MODEL, D_MODEL)), _const_spec((1, D_MODEL))]
                 + _route_inputs(lay, layer),
        out_specs=specs,
        out_shape=shapes,
        scratch_shapes=[pltpu.VMEM((TOK_TILE + 2 * CONV_HALO, D_MODEL), F32),
                        pltpu.VMEM((TOK_TILE + 2 * CONV_HALO - SUBLANES, D_MODEL), F32)],
        compiler_params=_cparams("arbitrary"),
        name="conv_dw_out_route",
    )(z, z, z, w_dw, b_dw, cg, cb, w2, b2, h, mods, lng, lnb, wr, br)


def _iota(shape, axis):
    return lax.broadcasted_iota(jnp.int32, shape, axis)


def _s5_prep_body(lrow_ref, lcol_ref, bt_ref, ct_ref, kk_ref, min_ref, mout_ref, arec_ref):
    tc, ch, ns = S5_CHUNK, SSM_CH_PER_GROUP, SSM_STATE
    kw = 2 * tc
    lrow = lrow_ref[0]
    lcol = lcol_ref[0]
    exact_dot = functools.partial(jnp.dot, preferred_element_type=F32, precision=HIGHEST)
    t_lane = _iota((1, tc), 1).astype(F32)
    s_row = _iota((tc, 1), 0).astype(F32)
    lag_lane = _iota((1, kw), 1) - (tc - 1)
    mout_ref[...] = jnp.zeros_like(mout_ref)
    kk = jnp.zeros((ch, ch * kw), F32)
    a_re, a_im = [], []
    for d in range(2):
        lr2 = jnp.minimum(lrow[3 * d:3 * d + 1], -1e-4)
        li2 = lrow[3 * d + 1:3 * d + 2]
        dt2 = jnp.exp(lrow[3 * d + 2:3 * d + 3])
        lr, li, dt = lr2[:, :ns], li2[:, :ns], dt2[:, :ns]
        lrc = jnp.minimum(lcol[:, 3 * d:3 * d + 1], -1e-4)
        lic = lcol[:, 3 * d + 1:3 * d + 2]
        dtc = jnp.exp(lcol[:, 3 * d + 2:3 * d + 3])
        er = jnp.exp(lr * dt)
        xr = er * jnp.cos(li * dt) - 1.0
        xi = er * jnp.sin(li * dt)
        den = lr * lr + li * li
        qr = (xr * lr + xi * li) / den
        qi = (xi * lr - xr * li) / den
        btr, bti = bt_ref[0, 2 * d], bt_ref[0, 2 * d + 1]
        bbr = qr * btr - qi * bti
        bbi = qr * bti + qi * btr
        n_in = (tc - 1.0 - s_row) if d == 0 else s_row
        mg = jnp.exp(n_in * (lr * dt))
        pr, pi = mg * jnp.cos(n_in * (li * dt)), mg * jnp.sin(n_in * (li * dt))
        re0, im0 = ns * d, 2 * ns + ns * d
        for c in range(ch):
            br = jnp.broadcast_to(bbr[c:c + 1], (tc, ns))
            bi = jnp.broadcast_to(bbi[c:c + 1], (tc, ns))
            min_ref[0, tc * c:tc * (c + 1), re0:re0 + ns] = (pr * br - pi * bi).astype(BF16)
            min_ref[0, tc * c:tc * (c + 1), im0:im0 + ns] = (pr * bi + pi * br).astype(BF16)

        ctr, cti = ct_ref[0, 2 * d], ct_ref[0, 2 * d + 1]

        def readout(n_of_lane, keep=None):
            width = n_of_lane.shape[1]
            mag = jnp.exp(n_of_lane * (lrc * dtc))
            w_re, w_im = mag * jnp.cos(n_of_lane * (lic * dtc)), mag * jnp.sin(n_of_lane * (lic * dtc))
            if keep is not None:
                w_re, w_im = jnp.where(keep, w_re, 0.0), jnp.where(keep, w_im, 0.0)
            res_re, res_im = [], []
            for c in range(ch):
                cr = jnp.broadcast_to(ctr[:, c:c + 1], (ns, width))
                ci = jnp.broadcast_to(cti[:, c:c + 1], (ns, width))
                res_re.append(cr * w_re - ci * w_im)
                res_im.append(-(cr * w_im + ci * w_re))
            return jnp.concatenate(res_re, axis=1), jnp.concatenate(res_im, axis=1)

        o_re, o_imneg = readout(t_lane + 1.0 if d == 0 else tc - t_lane)
        base = 4 * ns * d
        mout_ref[0, base + re0:base + re0 + ns, :] = o_re.astype(BF16)
        mout_ref[0, base + im0:base + im0 + ns, :] = o_imneg.astype(BF16)
        lag = lag_lane if d == 0 else -lag_lane
        k_re, k_imneg = readout(jnp.maximum(lag, 0).astype(F32), lag >= 0)
        kk = kk + exact_dot(bbr, k_re) + exact_dot(bbi, k_imneg)
        e2 = jnp.exp(tc * lr2 * dt2)
        a_re.append(e2 * jnp.cos(tc * li2 * dt2))
        a_im.append(e2 * jnp.sin(tc * li2 * dt2))
    kk_ref[0] = kk
    is_fwd = _iota((1, 2 * ns), 1) < ns
    arec_ref[0, 0:1] = jnp.where(is_fwd, a_re[0], a_re[1])
    arec_ref[0, 1:2] = jnp.where(is_fwd, a_im[0], a_im[1])


def _s5_prepare(lam_re, lam_im, log_dt, b_re, b_im, c_re, c_im):
    g, ns, ch, tc = SSM_GROUPS, SSM_STATE, SSM_CH_PER_GROUP, S5_CHUNK
    ldt = jnp.broadcast_to(log_dt[:, :, None], lam_re.shape)
    stack = jnp.stack([lam_re[0], lam_im[0], ldt[0], lam_re[1], lam_im[1], ldt[1]], axis=1)
    lrow = jnp.concatenate([stack, stack], axis=-1)
    lcol = jnp.swapaxes(stack, 1, 2)
    bt = jnp.stack([b_re[0], b_im[0], b_re[1], b_im[1]], axis=1).swapaxes(2, 3)
    ct = jnp.stack([c_re[0], c_im[0], c_re[1], c_im[1]], axis=1).swapaxes(2, 3)
    blk = lambda *s: pl.BlockSpec((1,) + s, lambda i: (i,) + (0,) * len(s))
    return pl.pallas_call(
        _s5_prep_body,
        grid=(g,),
        in_specs=[blk(6, 2 * ns), blk(ns, 6), blk(4, ch, ns), blk(4, ns, ch)],
        out_specs=[blk(ch, 2 * tc * ch), blk(tc * ch, 4 * ns), blk(8 * ns, tc * ch), blk(2, 2 * ns)],
        out_shape=[jax.ShapeDtypeStruct((g, ch, 2 * tc * ch), F32),
                   jax.ShapeDtypeStruct((g, tc * ch, 4 * ns), BF16),
                   jax.ShapeDtypeStruct((g, 8 * ns, tc * ch), BF16),
                   jax.ShapeDtypeStruct((g, 2, 2 * ns), F32)],
        compiler_params=_cparams("arbitrary"),
        name="s5_prepare",
    )(lrow, lcol, bt, ct)


def _s5_core_body(nb, nk, nkc, x_ref, kk_ref, min_ref, mout_ref, arec_ref, y_ref,
                  trow_a, trow_b, yacc, s_ref, h_ref):
    tc, ch, ns = S5_CHUNK, SSM_CH_PER_GROUP, SSM_STATE
    ns2, kw = 2 * ns, 2 * tc
    n_pairs = ch // 2

    def plane_pair(i):
        return jnp.concatenate([x_ref[2 * i], x_ref[2 * i + 1]], axis=1)

    def add_state(i, acc):
        rows = pl.ds(pl.multiple_of(i * kw, kw), kw)
        return acc + jnp.dot(plane_pair(i), min_ref[0, rows, :], preferred_element_type=F32)
    contrib = lax.fori_loop(0, n_pairs, add_state, jnp.zeros((nb * nk, 2 * ns2), F32))
    s_ref[0] = contrib[:, :ns2]
    s_ref[1] = contrib[:, ns2:]

    arec = arec_ref[0]
    a_re, a_im = arec[0:1], arec[1:2]
    is_fwd = _iota((nb, ns2), 1) < ns

    def step(i, carry):
        h_re, h_im = carry
        kr = jnp.where(i < nkc, nkc - 1 - i, nk - 1 - (i - nkc))
        fwd_rows = pl.ds(i, nb, stride=nk)
        rev_rows = pl.ds(kr, nb, stride=nk)
        h_ref[0, 0, fwd_rows, :] = h_re
        h_ref[0, 1, fwd_rows, :] = h_im
        h_ref[1, 0, rev_rows, :] = h_re
        h_ref[1, 1, rev_rows, :] = h_im
        s_re = jnp.where(is_fwd, s_ref[0, fwd_rows, :], s_ref[0, rev_rows, :])
        s_im = jnp.where(is_fwd, s_ref[1, fwd_rows, :], s_ref[1, rev_rows, :])
        return a_re * h_re - a_im * h_im + s_re, a_re * h_im + a_im * h_re + s_im

    zero = jnp.zeros((nb, ns2), F32)
    lax.fori_loop(0, nk, step, (zero, zero))

    y0 = jnp.zeros((nb * nk, ch * tc), F32)
    for d in range(2):
        hd = jnp.concatenate([h_ref[d, 0], h_ref[d, 1]], axis=1).astype(BF16)
        y0 = y0 + jnp.dot(hd, mout_ref[0, 2 * ns2 * d:2 * ns2 * (d + 1), :], preferred_element_type=F32)
    yacc[...] = y0

    def make_blocks(i, dst):
        for cc in range(2):
            lag_row = kk_ref[0, pl.ds(2 * i + cc, 1), :]
            for c in range(ch):
                lag = jnp.broadcast_to(lag_row[:, kw * c:kw * (c + 1)], (tc, kw))
                block = pltpu.roll(lag, tc + 1, 1, stride=1, stride_axis=0)[:, :tc]
                dst[tc * cc:tc * (cc + 1), tc * c:tc * (c + 1)] = block.astype(BF16)

    make_blocks(0, trow_a)

    def add_two_pairs(i2, carry):
        i = 2 * i2
        make_blocks(i + 1, trow_b)
        yacc[...] += jnp.dot(plane_pair(i), trow_a[...], preferred_element_type=F32)
        make_blocks(jnp.minimum(i + 2, n_pairs - 1), trow_a)
        yacc[...] += jnp.dot(plane_pair(i + 1), trow_b[...], preferred_element_type=F32)
        return carry
    lax.fori_loop(0, n_pairs // 2, add_two_pairs, 0)
    for c in range(ch):
        y_ref[c] = yacc[:, tc * c:tc * (c + 1)].astype(y_ref.dtype)


def _s5_core(lay, u, mats):
    kk, m_in, m_out, arec = mats
    g, ch, tc, ns = SSM_GROUPS, SSM_CH_PER_GROUP, S5_CHUNK, SSM_STATE
    b = lay.batch
    assert lay.seq % tc == 0 and lay.ctx_len % tc == 0
    nkl, nkc = lay.seq // tc, lay.ctx_len // tc
    nk = nkl + nkc
    rows = b * nk
    x = u.reshape(rows, tc, D_MODEL).transpose(2, 0, 1)
    blk = lambda *s: pl.BlockSpec((1,) + s, lambda i: (i,) + (0,) * len(s))
    planes = pl.BlockSpec((ch, rows, tc), lambda i: (i, 0, 0))
    y = pl.pallas_call(
        functools.partial(_s5_core_body, b, nk, nkc),
        grid=(g,),
        in_specs=[planes, blk(ch, 2 * tc * ch), blk(tc * ch, 4 * ns), blk(8 * ns, tc * ch), blk(2, 2 * ns)],
        out_specs=planes,
        out_shape=jax.ShapeDtypeStruct((D_MODEL, rows, tc), BF16),
        scratch_shapes=[pltpu.VMEM((2 * tc, ch * tc), BF16), pltpu.VMEM((2 * tc, ch * tc), BF16),
                        pltpu.VMEM((rows, ch * tc), F32),
                        pltpu.VMEM((2, rows, 2 * ns), F32), pltpu.VMEM((2, 2, rows, 2 * ns), F32)],
        compiler_params=_cparams("arbitrary"),
        name="s5_scan",
    )(x, kk, m_in, m_out, arec)
    return y.transpose(1, 2, 0).reshape(rows * tc, D_MODEL)


def _s5_out_body(y_ref, u_ref, d_ref, wv_ref, wg_ref, *rest):
    u = u_ref[...].astype(F32)
    act = _gelu_tanh(y_ref[...].astype(F32) + d_ref[...] * u).astype(BF16)
    val = jnp.dot(act, wv_ref[...], preferred_element_type=F32)
    gate = jnp.dot(act, wg_ref[...], preferred_element_type=F32)
    _finish_route(val * jax.nn.sigmoid(gate), *rest)


def _s5_out(lay, n_tiles, layer, y, u, d, wv, wg, h, mods, lng, lnb, wr, br):
    shapes, specs = _route_outputs(n_tiles * TOK_TILE)
    return pl.pallas_call(
        _s5_out_body,
        grid=(n_tiles,),
        in_specs=[lay.s5_row_spec(), lay.s5_row_spec(), _const_spec((1, D_MODEL)),
                  _const_spec((D_MODEL, D_MODEL)), _const_spec((D_MODEL, D_MODEL))] + _route_inputs(lay, layer),
        out_specs=specs,
        out_shape=shapes,
        compiler_params=_cparams("arbitrary"),
        name="s5_out_route",
    )(y, u, d, wv, wg, h, mods, lng, lnb, wr, br)


def kernel(x, c, ctx, c_ctx, ada_w, ada_b, ln_g, ln_b, s5_lam_re, s5_lam_im, s5_log_dt, s5_b_re, s5_b_im, s5_c_re, s5_c_im, s5_d, s5_w_glu, cv_w_pw1, cv_b_pw1, cv_w_dw, cv_b_dw, cv_ln_g, cv_ln_b, cv_w_pw2, cv_b_pw2, at_w_qkv, at_w_o, at_sink, moe_wg, moe_bg, moe_we, moe_be, moe_w1, moe_w2):
    batch, seq, d = x.shape
    lay = _Layout(batch, seq, ctx.shape[1])
    depth = ada_w.shape[0]
    row = lambda a: a.reshape(1, -1)

    c_all =jnp.concatenate([c, c_ctx[None], jnp.zeros((lay.mod_rows - batch - 1, d), F32)], axis=0)
    mods = _modulation(c_all, ada_w, ada_b)
    cos_t, sin_t = _rope_tables(seq)
    ng, ne = N_EXPERT_GROUPS, N_EXPERTS
    pad = jnp.zeros((d, ROUTE_LANES - ng - ne), F32)

    u_spec = lambda layer: lay.s5_row_spec() if layer % 3 == 0 else _row_spec()
    h, u = _modulate(lay, x.reshape(lay.n_lat, d), ctx.reshape(lay.n_ctx, d), mods, 0, u_spec(0))
    for i in range(depth):
        last = i == depth - 1
        kind, j = i % 3, i // 3
        n_tiles = lay.lat_tiles if last else lay.tiles
        wr = _split_bf16(jnp.concatenate([moe_wg[i], moe_we[i], pad], axis=1))
        br = jnp.concatenate([moe_bg[i], moe_be[i], pad[0]], axis=0)[None]
        route_args = (h, mods, row(ln_g[i, 0]), row(ln_b[i, 0]), wr, br)
        if kind == 0:
            mats = _s5_prepare(s5_lam_re[j], s5_lam_im[j], s5_log_dt[j], s5_b_re[j], s5_b_im[j],
                               s5_c_re[j], s5_c_im[j])
            y = _s5_core(lay, u, mats)
            wglu = s5_w_glu[j].astype(BF16)
            h1, v, meta, cnt = _s5_out(lay, n_tiles, i, y, u, row(s5_d[j]), wglu[:, :d], wglu[:, d:], *route_args)
        elif kind == 1:
            w1 = cv_w_pw1[j].astype(BF16)
            z = _conv_pw1(lay, u, w1[:, :d], w1[:, d:], row(cv_b_pw1[j, :d]), row(cv_b_pw1[j, d:]))
            h1, v, meta, cnt = _conv_out(lay, n_tiles, i, z, cv_w_dw[j], row(cv_b_dw[j]), row(cv_ln_g[j]),
                                         row(cv_ln_b[j]), cv_w_pw2[j].astype(BF16), row(cv_b_pw2[j]),
                                         *route_args)
        else:
            q, k, vv = _qkv_rope(lay, u, at_w_qkv[j].astype(BF16), cos_t, sin_t)
            o = _attention(lay, q, k, vv, row(at_sink[j]), not last)
            h1, v, meta, cnt = _attn_out(lay, n_tiles, i, o, at_w_o[j].astype(BF16), *route_args)
        plan = _moe_plan(meta, cnt, n_tiles)
        xs = _moe_dispatch(v, plan)
        ys = _moe_experts(xs, moe_w1, moe_w2, i, plan.items)
        h, u = _moe_combine(lay, n_tiles, ys, plan, h1, mods, i, row(ln_g[i, 1]), row(ln_b[i, 1]),
                            None if last else u_spec(i + 1))
    return h.reshape(batch, seq, d)
```
